```python
import math
import jax, jax.numpy as jnp
from jax import lax
import numpy as np

D_MODEL = 1024
BATCH = 8
SEQ = 2048
DEPTH = 2
DEC_BATCH = 128
DEC_SEQ = 8
PAST_LEN = 16384
PAGE_SIZE = 128

N_META = 16
D_MIX = D_MODEL
D_SSM = D_MIX // 2
D_POOL = D_MIX - D_SSM
SSM_GROUP = 16
N_SSM_GROUPS = D_SSM // SSM_GROUP
SSM_STATE = 64
POOL_WINDOWS = (2, 4, 8, 16)
N_POOL_GROUPS = len(POOL_WINDOWS)
POOL_GROUP = D_POOL // N_POOL_GROUPS
POOL_BUF = max(POOL_WINDOWS) - 1
N_EXPERT_GROUPS = 4
EXPERTS_PER_GROUP = 8
N_EXPERTS = N_EXPERT_GROUPS * EXPERTS_PER_GROUP
D_EXPERT = D_MODEL // 4
TOP_K_INNER = 2
DT_MIN = 0.001
DT_MAX = 0.1
EPS = 1e-6

kernel_name = 'hymba_s5_pool_hiermoe_step'


def _rmsnorm(x, g):
    xf = x.astype(jnp.float32)
    y = xf * lax.rsqrt(jnp.mean(xf * xf, axis=-1, keepdims=True) + EPS)
    return (y * g.astype(jnp.float32)).astype(x.dtype)


def _complex_affine_combine(e1, e2):
    a1r, a1i, b1r, b1i = e1
    a2r, a2i, b2r, b2i = e2
    ar = a1r * a2r - a1i * a2i
    ai = a1r * a2i + a1i * a2r
    br = a2r * b1r - a2i * b1i + b2r
    bi = a2r * b1i + a2i * b1r + b2i
    return (ar, ai, br, bi)


def _s5_discretise(a_re, a_im, log_dt, b_re, b_im):
    a_re = a_re.astype(jnp.float32)
    a_im = a_im.astype(jnp.float32)
    dt = jnp.exp(log_dt.astype(jnp.float32))[:, None]
    mag = jnp.exp(dt * a_re)
    abar_re = mag * jnp.cos(dt * a_im)
    abar_im = mag * jnp.sin(dt * a_im)
    den = a_re * a_re + a_im * a_im
    nr = abar_re - 1.0
    ni = abar_im
    f_re = (nr * a_re + ni * a_im) / den
    f_im = (ni * a_re - nr * a_im) / den
    b_re = b_re.astype(jnp.float32)
    b_im = b_im.astype(jnp.float32)
    bbar_re = f_re[..., None] * b_re - f_im[..., None] * b_im
    bbar_im = f_re[..., None] * b_im + f_im[..., None] * b_re
    return abar_re, abar_im, bbar_re, bbar_im


def _s5_mixer(u, h0_re, h0_im, a_re, a_im, log_dt, b_re, b_im, c_re, c_im, d, w_glu, b_glu):
    bsz, length, _ = u.shape
    uf = u.astype(jnp.float32)
    ug = uf.reshape(bsz, length, N_SSM_GROUPS, SSM_GROUP)
    abar_re, abar_im, bbar_re, bbar_im = _s5_discretise(a_re, a_im, log_dt, b_re, b_im)
    bu_re = jnp.einsum('blgc,gnc->blgn', ug, bbar_re)
    bu_im = jnp.einsum('blgc,gnc->blgn', ug, bbar_im)
    h0r = h0_re.astype(jnp.float32)
    h0i = h0_im.astype(jnp.float32)
    bu_re = bu_re.at[:, 0].add(abar_re * h0r - abar_im * h0i)
    bu_im = bu_im.at[:, 0].add(abar_re * h0i + abar_im * h0r)
    a_r = jnp.broadcast_to(abar_re, bu_re.shape)
    a_i = jnp.broadcast_to(abar_im, bu_im.shape)
    _, _, h_re, h_im = lax.associative_scan(_complex_affine_combine, (a_r, a_i, bu_re, bu_im), axis=1)
    y = (jnp.einsum('blgn,gcn->blgc', h_re, c_re.astype(jnp.float32))
         - jnp.einsum('blgn,gcn->blgc', h_im, c_im.astype(jnp.float32)))
    y = y.reshape(bsz, length, D_SSM) + d.astype(jnp.float32) * uf
    g = jax.nn.gelu(y)
    out = g * jax.nn.sigmoid(g @ w_glu.astype(jnp.float32) + b_glu.astype(jnp.float32))
    return out.astype(u.dtype), h_re[:, -1].astype(h0_re.dtype), h_im[:, -1].astype(h0_im.dtype)


def _pool_mixer(u, buf, pos, w_pool, scale):
    bsz, length, _ = u.shape
    cat = jnp.concatenate([buf.astype(u.dtype), u], axis=1)
    cf = cat.astype(jnp.float32)
    cs = jnp.concatenate([jnp.zeros((bsz, 1, D_POOL), jnp.float32), jnp.cumsum(cf, axis=1)], axis=1)
    start = POOL_BUF + 1
    outs = []
    for k, w in enumerate(POOL_WINDOWS):
        lo, hi = k * POOL_GROUP, (k + 1) * POOL_GROUP
        win_sum = cs[:, start:start + length, lo:hi] - cs[:, start - w:start - w + length, lo:hi]
        count = jnp.minimum(pos + 1, w).astype(jnp.float32)[None, :, None]
        diff = win_sum / count - cf[:, POOL_BUF:, lo:hi]
        outs.append(diff @ w_pool[k].astype(jnp.float32))
    out = jnp.concatenate(outs, axis=-1) * scale.astype(jnp.float32)
    return out.astype(u.dtype), cat[:, -POOL_BUF:].astype(buf.dtype)


def _hier_moe(h, w_rg, b_rg, w_re, b_re, w_gate, w_up, w_down):
    bsz, length, dm = h.shape
    t = h.reshape(-1, dm)
    lg = (t @ w_rg).astype(jnp.float32) + b_rg.astype(jnp.float32)
    p = jax.nn.softmax(lg, axis=-1)
    gsel = jnp.argmax(lg, axis=-1)
    p_sel = jnp.take_along_axis(p, gsel[:, None], axis=-1)
    le = ((t @ w_re).astype(jnp.float32) + b_re.astype(jnp.float32)).reshape(-1, N_EXPERT_GROUPS, EXPERTS_PER_GROUP)
    le_sel = jnp.take_along_axis(le, gsel[:, None, None], axis=1)[:, 0]
    vals, idx = lax.top_k(le_sel, TOP_K_INNER)
    q = jax.nn.softmax(vals, axis=-1) * p_sel
    eid = gsel[:, None] * EXPERTS_PER_GROUP + idx
    comb = jnp.sum(jax.nn.one_hot(eid, N_EXPERTS, dtype=jnp.float32) * q[..., None], axis=1)
    hg = jnp.einsum('td,edf->tef', t, w_gate)
    hu = jnp.einsum('td,edf->tef', t, w_up)
    act = jax.nn.silu(hg) * hu * comb[..., None].astype(hg.dtype)
    out = jnp.einsum('tef,efd->td', act, w_down)
    return out.reshape(bsz, length, dm).astype(h.dtype)


def setup_inputs(seed: int = 0) -> dict:
    key = jax.random.key(seed)
    ks = iter(jax.random.split(key, 40))

    def nrm(shape, scale):
        return jax.random.normal(next(ks), shape, jnp.float32) * scale

    n_idx = jnp.arange(SSM_STATE, dtype=jnp.float32)
    gns = (DEPTH, N_SSM_GROUPS, SSM_STATE)
    return {
        'x_prompt': nrm((BATCH, SEQ, D_MODEL), 1.0),
        'x_sample': nrm((DEC_BATCH, DEC_SEQ, D_MODEL), 1.0),
        'state_ssm_re': nrm((DEPTH, DEC_BATCH, N_SSM_GROUPS, SSM_STATE), 0.1),
        'state_ssm_im': nrm((DEPTH, DEC_BATCH, N_SSM_GROUPS, SSM_STATE), 0.1),
        'state_pool': nrm((DEPTH, DEC_BATCH, POOL_BUF, D_POOL), 1.0),
        'meta_tokens': nrm((N_META, D_MODEL), 1.0),
        'norm_mix_g': 1.0 + nrm((DEPTH, D_MODEL), 0.02),
        'w_in': nrm((DEPTH, D_MODEL, D_MIX), D_MODEL ** -0.5),
        'ssm_a_re': -0.5 + nrm(gns, 0.01),
        'ssm_a_im': math.pi * n_idx + nrm(gns, 0.01),
        'ssm_log_dt': jax.random.uniform(next(ks), (DEPTH, N_SSM_GROUPS), jnp.float32, math.log(DT_MIN), math.log(DT_MAX)),
        'ssm_b_re': nrm((DEPTH, N_SSM_GROUPS, SSM_STATE, SSM_GROUP), (2 * SSM_GROUP) ** -0.5),
        'ssm_b_im': nrm((DEPTH, N_SSM_GROUPS, SSM_STATE, SSM_GROUP), (2 * SSM_GROUP) ** -0.5),
        'ssm_c_re': nrm((DEPTH, N_SSM_GROUPS, SSM_GROUP, SSM_STATE), (2 * SSM_STATE) ** -0.5),
        'ssm_c_im': nrm((DEPTH, N_SSM_GROUPS, SSM_GROUP, SSM_STATE), (2 * SSM_STATE) ** -0.5),
        'ssm_d': nrm((DEPTH, D_SSM), 1.0),
        'w_glu': nrm((DEPTH, D_SSM, D_SSM), D_SSM ** -0.5),
        'b_glu': nrm((DEPTH, D_SSM), 0.02),
        'w_pool': nrm((DEPTH, N_POOL_GROUPS, POOL_GROUP, POOL_GROUP), POOL_GROUP ** -0.5),
        'pool_scale': 1.0 + nrm((DEPTH, D_POOL), 0.1),
        'w_out': nrm((DEPTH, D_MIX, D_MODEL), D_MIX ** -0.5),
        'norm_ffn_g': 1.0 + nrm((DEPTH, D_MODEL), 0.02),
        'w_router_group': nrm((DEPTH, D_MODEL, N_EXPERT_GROUPS), D_MODEL ** -0.5),
        'b_router_group': nrm((DEPTH, N_EXPERT_GROUPS), 0.01),
        'w_router_expert': nrm((DEPTH, D_MODEL, N_EXPERTS), D_MODEL ** -0.5),
        'b_router_expert': nrm((DEPTH, N_EXPERTS), 0.01),
        'w_gate': nrm((DEPTH, N_EXPERTS, D_MODEL, D_EXPERT), D_MODEL ** -0.5),
        'w_up': nrm((DEPTH, N_EXPERTS, D_MODEL, D_EXPERT), D_MODEL ** -0.5),
        'w_down': nrm((DEPTH, N_EXPERTS, D_EXPERT, D_MODEL), D_EXPERT ** -0.5),
        'norm_final_g': 1.0 + nrm((D_MODEL,), 0.02),
    }


def reference(x_prompt, x_sample, state_ssm_re, state_ssm_im, state_pool, meta_tokens,
              norm_mix_g, w_in, ssm_a_re, ssm_a_im, ssm_log_dt, ssm_b_re, ssm_b_im, ssm_c_re, ssm_c_im,
              ssm_d, w_glu, b_glu, w_pool, pool_scale, w_out, norm_ffn_g,
              w_router_group, b_router_group, w_router_expert, b_router_expert,
              w_gate, w_up, w_down, norm_final_g):

    def run_layer(x, pos, h0_re, h0_im, buf, l):
        h = _rmsnorm(x, norm_mix_g[l])
        u = h @ w_in[l]
        y_ssm, hr, hi = _s5_mixer(u[..., :D_SSM], h0_re, h0_im, ssm_a_re[l], ssm_a_im[l], ssm_log_dt[l],
                                  ssm_b_re[l], ssm_b_im[l], ssm_c_re[l], ssm_c_im[l], ssm_d[l], w_glu[l], b_glu[l])
        y_pool, new_buf = _pool_mixer(u[..., D_SSM:], buf, pos, w_pool[l], pool_scale[l])
        x = x + jnp.concatenate([y_ssm, y_pool], axis=-1) @ w_out[l]
        x = x + _hier_moe(_rmsnorm(x, norm_ffn_g[l]), w_router_group[l], b_router_group[l],
                          w_router_expert[l], b_router_expert[l], w_gate[l], w_up[l], w_down[l])
        return x, hr, hi, new_buf

    xp = jnp.concatenate([jnp.broadcast_to(meta_tokens.astype(x_prompt.dtype), (BATCH, N_META, D_MODEL)), x_prompt], axis=1)
    xs = x_sample
    pos_p = jnp.arange(SEQ + N_META, dtype=jnp.int32)
    pos_s = PAST_LEN + jnp.arange(DEC_SEQ, dtype=jnp.int32)
    zero_h = jnp.zeros((BATCH, N_SSM_GROUPS, SSM_STATE), jnp.float32)
    zero_buf = jnp.zeros((BATCH, POOL_BUF, D_POOL), xp.dtype)

    ssm_re_p, ssm_im_p, pool_p = [], [], []
    ssm_re_s, ssm_im_s, pool_s = [], [], []
    for l in range(DEPTH):
        xp, hr, hi, nb = run_layer(xp, pos_p, zero_h, zero_h, zero_buf, l)
        ssm_re_p.append(hr); ssm_im_p.append(hi); pool_p.append(nb)
        xs, hr, hi, nb = run_layer(xs, pos_s, state_ssm_re[l], state_ssm_im[l], state_pool[l], l)
        ssm_re_s.append(hr); ssm_im_s.append(hi); pool_s.append(nb)

    y_prompt = _rmsnorm(xp, norm_final_g)[:, N_META:]
    y_sample = _rmsnorm(xs, norm_final_g)
    new_ssm_re_prompt = jnp.stack(ssm_re_p)
    new_ssm_im_prompt = jnp.stack(ssm_im_p)
    new_pool_prompt = jnp.stack(pool_p)
    new_ssm_re_sample = jnp.stack(ssm_re_s)
    new_ssm_im_sample = jnp.stack(ssm_im_s)
    new_pool_sample = jnp.stack(pool_s)
    return (y_prompt, y_sample, new_ssm_re_prompt, new_ssm_im_prompt, new_pool_prompt,
            new_ssm_re_sample, new_ssm_im_sample, new_pool_sample)
```

```python
import functools

import numpy as np
import jax
import jax.numpy as jnp
from jax import lax
from jax.experimental import pallas as pl
from jax.experimental.pallas import tpu as pltpu

F32 = jnp.float32
BF16 = jnp.bfloat16

LANES = 128
SUBLANES = 8
CHUNK = 8
POOL_HIST = 16
ROUTE_TILE = 128
EXPERT_TILE = 256
VMEM_LIMIT = 56 * 1024 * 1024
EPS = 1e-6
NEG = -1e30
PAST_LEN = 16384


def _dot(a, b):
    return jnp.dot(a, b, preferred_element_type=F32)


def _dot_nt_f32(a, b):
    return lax.dot_general(a, b, (((1,), (1,)), ((), ())), precision=lax.Precision.HIGHEST,
                           preferred_element_type=F32)


def _s5_prep_kernel(are_ref, aim_ref, ldt_ref, btr_ref, bti_ref, cr_ref, ci_ref,
                    k_ref, wtr_ref, wti_ref, er_ref, ei_ref, a8r_ref, a8i_ref):
    a_re = are_ref[...]
    a_im = aim_ref[...]
    dt = jnp.exp(ldt_ref[...])
    mag = jnp.exp(dt * a_re)
    abr = mag * jnp.cos(dt * a_im)
    abi = mag * jnp.sin(dt * a_im)
    den = a_re * a_re + a_im * a_im
    nr = abr - 1.0
    ni = abi
    f_re = (nr * a_re + ni * a_im) / den
    f_im = (ni * a_re - nr * a_im) / den
    btr = btr_ref[...]
    bti = bti_ref[...]
    bbr = f_re * btr - f_im * bti
    bbi = f_re * bti + f_im * btr
    cr = cr_ref[...]
    ci = ci_ref[...]
    pr = jnp.ones_like(a_re)
    pi = jnp.zeros_like(a_re)
    for k in range(CHUNK + 1):
        ekr = cr * pr - ci * pi
        eki = cr * pi + ci * pr
        if k < CHUNK:
            wtr_ref[k] = pr * bbr - pi * bbi
            wti_ref[k] = pr * bbi + pi * bbr
            k_ref[k] = _dot_nt_f32(ekr, bbr) - _dot_nt_f32(eki, bbi)
        if k >= 1:
            er_ref[k - 1] = ekr
            ei_ref[k - 1] = eki
        if k < CHUNK:
            pr, pi = pr * abr - pi * abi, pr * abi + pi * abr
    a8r_ref[...] = pr
    a8i_ref[...] = pi


def _s5_prepare(a_re, a_im, log_dt, b_re, b_im, c_re, c_im):
    depth, groups, nstate = a_re.shape
    ch = b_re.shape[-1]
    dg = depth * groups
    assert ch * CHUNK == LANES and 2 * nstate == LANES and groups % CHUNK == 0
    r3 = lambda x: x.reshape(dg, 1, -1).astype(F32)
    btr = jnp.swapaxes(b_re.reshape(dg, nstate, ch), 1, 2).astype(F32)
    bti = jnp.swapaxes(b_im.reshape(dg, nstate, ch), 1, 2).astype(F32)
    cr = c_re.reshape(dg, ch, nstate).astype(F32)
    ci = c_im.reshape(dg, ch, nstate).astype(F32)
    vec = pl.BlockSpec((None, 1, nstate), lambda g: (g, 0, 0))
    mat = pl.BlockSpec((None, ch, nstate), lambda g: (g, 0, 0))
    kmat = pl.BlockSpec((None, CHUNK, ch, nstate), lambda g: (g, 0, 0, 0))
    kk, wtr, wti, er, ei, a8r, a8i = pl.pallas_call(
        _s5_prep_kernel,
        grid=(dg,),
        in_specs=[vec, vec, pl.BlockSpec((None, 1, 1), lambda g: (g, 0, 0)), mat, mat, mat, mat],
        out_specs=[pl.BlockSpec((None, CHUNK, ch, ch), lambda g: (g, 0, 0, 0)), kmat, kmat, kmat, kmat, vec, vec],
        out_shape=[jax.ShapeDtypeStruct((dg, CHUNK, ch, ch), F32)]
        + [jax.ShapeDtypeStruct((dg, CHUNK, ch, nstate), F32)] * 4
        + [jax.ShapeDtypeStruct((dg, 1, nstate), F32)] * 2,
        name="s5_prep",
    )(r3(a_re), r3(a_im), r3(log_dt), btr, bti, cr, ci)

    slot = np.arange(CHUNK)
    s_of_g = np.arange(dg) % CHUNK
    jpos = (slot[None, :] - s_of_g[:, None]) % CHUNK
    lag = jpos[:, None, :] - jpos[:, :, None]
    causal = jnp.asarray(lag >= 0, F32)[:, :, :, None, None]
    gi = np.arange(dg)[:, None, None]
    m5 = kk[gi, np.maximum(lag, 0)] * causal
    m = m5.transpose(0, 1, 4, 2, 3).reshape(dg, LANES, LANES)
    gi2 = np.arange(dg)[:, None]
    qsel = CHUNK - 1 - jpos
    q = jnp.concatenate([wtr[gi2, qsel], wti[gi2, qsel]], axis=-1).reshape(dg, LANES, LANES)
    mq = jnp.concatenate([m, q], axis=-1).astype(BF16)
    ptr = er[gi2, jpos].reshape(dg, LANES, nstate)
    pti = ei[gi2, jpos].reshape(dg, LANES, nstate)
    pmat = jnp.concatenate([jnp.swapaxes(ptr, 1, 2), -jnp.swapaxes(pti, 1, 2)], axis=1).astype(BF16)
    a8 = jnp.concatenate([
        jnp.concatenate([a8r, a8r], axis=-1),
        jnp.concatenate([-a8i, a8i], axis=-1),
        jnp.concatenate([a8i, -a8i], axis=-1)], axis=1)
    shp = lambda x: x.reshape((depth, groups) + x.shape[1:])
    return shp(mq), shp(pmat), shp(a8)


def _mixer_kernel(x_ref, h0_ref, buf0_ref, gmix_ref, win_ref, mq_ref, p_ref, a8_ref, d_ref, wglu_ref,
                  bglu_ref, wpool_ref, pscale_ref, wout_ref, gffn_ref, wrhi_ref, wrlo_ref, br_ref,
                  route_in_ref, x1_ref, route_ref, h_ref, buf_ref, u_scr, y_scr, cat_scr,
                  *, bblk, tile_l, pos0, windows, n_expert_groups, experts_per_group):
    del route_in_ref
    li = pl.program_id(1)
    rows = tile_l * bblk
    nk = tile_l // CHUNK
    n = nk * bblk
    d_model = x_ref.shape[-1]
    d_ssm = y_scr.shape[-1]
    d_pool = d_model - d_ssm
    hist = POOL_HIST * bblk

    @pl.when(li == 0)
    def _init():
        h_ref[...] = h0_ref[...]
        cat_scr[0:hist, :] = buf0_ref[...].reshape(hist, d_pool)

    x = x_ref[...]
    ms = jnp.mean(x * x, axis=-1, keepdims=True)
    h = (x * lax.rsqrt(ms + EPS) * gmix_ref[...]).astype(BF16)
    u = _dot(h, win_ref[...])
    u_scr[...] = u.reshape(nk, CHUNK, bblk, d_model)

    slot = lax.broadcasted_iota(jnp.int32, (n, LANES), 1) // (LANES // CHUNK)
    for v in range(d_ssm // LANES):
        rolled = []
        for j in range(CHUNK):
            uj = u_scr[:, j, :, v * LANES:(v + 1) * LANES].reshape(n, LANES)
            rolled.append(pltpu.roll(uj, (LANES // CHUNK) * j, axis=1) if j else uj)
        ys = []
        for s in range(CHUNK):
            g = v * CHUNK + s
            ug = rolled[0]
            for j in range(1, CHUNK):
                ug = jnp.where(slot == (s + j) % CHUNK, rolled[j], ug)
            yq = _dot(ug.astype(BF16), mq_ref[g])
            sg = yq[:, LANES:]
            sg_sw = pltpu.roll(sg, LANES // 2, axis=1)
            a8c = a8_ref[g, 0:1, :]
            a8s = a8_ref[g, 1:2, :]
            a8w = a8_ref[g, 2:3, :]
            hcur = h_ref[g]
            hsw = pltpu.roll(hcur, LANES // 2, axis=1)
            hins = []
            for k in range(nk):
                hins.append(hcur)
                sk = sg[k * bblk:(k + 1) * bblk]
                skw = sg_sw[k * bblk:(k + 1) * bblk]
                hcur, hsw = a8c * hcur + a8s * hsw + sk, a8c * hsw + a8w * hcur + skw
            h_ref[g] = hcur
            hin = jnp.concatenate(hins, axis=0) if nk > 1 else hins[0]
            ys.append(yq[:, :LANES] + _dot(hin.astype(BF16), p_ref[g]))
        for j in range(CHUNK):
            z = ys[0]
            for s in range(1, CHUNK):
                z = jnp.where(slot == (s + j) % CHUNK, ys[s], z)
            if j:
                z = pltpu.roll(z, LANES - (LANES // CHUNK) * j, axis=1)
            y_scr[:, j, :, v * LANES:(v + 1) * LANES] = z.reshape(nk, bblk, LANES)

    u_ssm = u_scr[:, :, :, 0:d_ssm].reshape(rows, d_ssm)
    y = y_scr[...].reshape(rows, d_ssm) + d_ref[...] * u_ssm
    gl = jax.nn.gelu(y)
    ssm_out = gl * jax.nn.sigmoid(_dot(gl.astype(BF16), wglu_ref[...]) + bglu_ref[...])

    up = u_scr[:, :, :, d_ssm:].reshape(rows, d_pool)
    cat_scr[hist:hist + rows, :] = up
    row_pos = lax.broadcasted_iota(jnp.int32, (rows, 1), 0) // bblk
    pos = pos0 + li * tile_l + row_pos
    pool_parts = []
    pgrp = d_pool // len(windows)
    for kk, w in enumerate(windows):
        lo = kk * pgrp
        acc = cat_scr[hist:hist + rows, lo:lo + pgrp]
        cur = acc
        for sft in range(1, w):
            acc = acc + cat_scr[hist - sft * bblk:hist - sft * bblk + rows, lo:lo + pgrp]
        inv = 1.0 / jnp.minimum(pos + 1, w).astype(F32)
        diff = acc * inv - cur
        pool_parts.append(_dot(diff.astype(BF16), wpool_ref[kk]))
    pool_out = jnp.concatenate(pool_parts, axis=-1) * pscale_ref[...]
    tail = cat_scr[rows:rows + hist, :]
    cat_scr[0:hist, :] = tail
    buf_ref[...] = tail.reshape(POOL_HIST, bblk, d_pool)

    mix = jnp.concatenate([ssm_out.astype(BF16), pool_out.astype(BF16)], axis=-1)
    x1 = x + _dot(mix, wout_ref[...])
    x1_ref[...] = x1
    ms2 = jnp.mean(x1 * x1, axis=-1, keepdims=True)
    t = x1 * lax.rsqrt(ms2 + EPS) * gffn_ref[...]

    t_hi = t.astype(BF16)
    t_lo = (t - t_hi.astype(F32)).astype(BF16)
    logits = (_dot(t_hi, wrhi_ref[...]) + _dot(t_lo, wrhi_ref[...]) + _dot(t_hi, wrlo_ref[...])
              + br_ref[...])
    lane = lax.broadcasted_iota(jnp.int32, (rows, LANES), 1)
    big = jnp.int32(4 * LANES)
    lg = jnp.where(lane < n_expert_groups, logits, NEG)
    mg = jnp.max(lg, axis=-1, keepdims=True)
    gsel = jnp.min(jnp.where(lg == mg, lane, big), axis=-1, keepdims=True)
    p_sel = 1.0 / jnp.sum(jnp.exp(lg - mg), axis=-1, keepdims=True)
    e_lo = n_expert_groups + gsel * experts_per_group
    le = jnp.where((lane >= e_lo) & (lane < e_lo + experts_per_group), logits, NEG)
    v1 = jnp.max(le, axis=-1, keepdims=True)
    i1 = jnp.min(jnp.where(le == v1, lane, big), axis=-1, keepdims=True)
    le2 = jnp.where(lane == i1, NEG, le)
    v2 = jnp.max(le2, axis=-1, keepdims=True)
    i2 = jnp.min(jnp.where(le2 == v2, lane, big), axis=-1, keepdims=True)
    ex = jnp.exp(v2 - v1)
    q1 = p_sel / (1.0 + ex)
    q2 = q1 * ex
    e1 = (i1 - n_expert_groups).astype(F32)
    e2 = (i2 - n_expert_groups).astype(F32)
    route_ref[...] = jnp.where(lane == 0, e1, jnp.where(lane == 1, e2, jnp.where(lane == 2, q1,
                               jnp.where(lane == 3, q2, 0.0))))


def _mixer_call(x_flat, route_flat, h0, buf0, lw, *, row_block, batch, bblk, length, tile_l, pos0, cfg):
    total, d_model = x_flat.shape
    groups = h0.shape[0]
    d_pool = buf0.shape[-1]
    d_ssm = d_model - d_pool
    nb = batch // bblk
    nl = length // tile_l
    rows = tile_l * bblk

    def full(a):
        nd = a.ndim
        return pl.BlockSpec(a.shape, lambda b, l, _n=nd: (0,) * _n)

    tok = lambda width: pl.BlockSpec((rows, width), lambda b, l: (row_block + b * nl + l, 0))
    weights = [lw["g_mix"], lw["w_in"], lw["mq"], lw["pmat"], lw["a8"], lw["ssm_d"], lw["w_glu"], lw["b_glu"],
               lw["w_pool"], lw["pool_scale"], lw["w_out"], lw["g_ffn"], lw["wr_hi"], lw["wr_lo"], lw["b_r"]]
    in_specs = ([tok(d_model),
                 pl.BlockSpec((groups, bblk, LANES), lambda b, l: (0, b, 0)),
                 pl.BlockSpec((POOL_HIST, bblk, d_pool), lambda b, l: (0, b, 0))]
                + [full(w) for w in weights]
                + [pl.BlockSpec(memory_space=pl.ANY)])
    out_specs = [tok(d_model), tok(LANES),
                 pl.BlockSpec((groups, bblk, LANES), lambda b, l: (0, b, 0)),
                 pl.BlockSpec((POOL_HIST, bblk, d_pool), lambda b, l: (0, b, 0))]
    out_shape = [jax.ShapeDtypeStruct((total, d_model), F32), jax.ShapeDtypeStruct((total, LANES), F32),
                 jax.ShapeDtypeStruct((groups, batch, LANES), F32),
                 jax.ShapeDtypeStruct((POOL_HIST, batch, d_pool), F32)]
    n_in = 3 + len(weights)
    kern = functools.partial(_mixer_kernel, bblk=bblk, tile_l=tile_l, pos0=pos0, windows=cfg["windows"],
                             n_expert_groups=cfg["n_expert_groups"], experts_per_group=cfg["experts_per_group"])
    args = [x_flat, h0, buf0] + weights + [route_flat]
    return pl.pallas_call(
        kern,
        grid=(nb, nl),
        in_specs=in_specs,
        out_specs=out_specs,
        out_shape=out_shape,
        scratch_shapes=[pltpu.VMEM((tile_l // CHUNK, CHUNK, bblk, d_model), F32),
                        pltpu.VMEM((tile_l // CHUNK, CHUNK, bblk, d_ssm), F32),
                        pltpu.VMEM(((POOL_HIST + tile_l) * bblk, d_pool), F32)],
        input_output_aliases={0: 0, n_in: 1},
        compiler_params=pltpu.CompilerParams(dimension_semantics=("arbitrary", "arbitrary"),
                                             vmem_limit_bytes=VMEM_LIMIT),
        name="mixer",
    )(*args)


def _plan_kernel(route_ref, rank_ref, cnt_ref):
    @pl.when(pl.program_id(0) == 0)
    def _():
        cnt_ref[...] = jnp.zeros_like(cnt_ref)

    r = route_ref[...]
    tp = r.shape[0]
    lane = lax.broadcasted_iota(jnp.int32, (tp, LANES), 1)
    lanef = lane.astype(F32)
    oh1 = lanef == r[:, 0:1]
    oh2 = lanef == r[:, 1:2]
    oh = jnp.logical_or(oh1, oh2).astype(F32)
    earlier = lax.broadcasted_iota(jnp.int32, (tp, tp), 0) > lax.broadcasted_iota(jnp.int32, (tp, tp), 1)
    cum = _dot(earlier.astype(BF16), oh.astype(BF16)) + cnt_ref[...]
    rank1 = jnp.sum(jnp.where(oh1, cum, 0.0), axis=-1, keepdims=True)
    rank2 = jnp.sum(jnp.where(oh2, cum, 0.0), axis=-1, keepdims=True)
    rank_ref[...] = jnp.where(lane == 0, rank1, jnp.where(lane == 1, rank2, 0.0))
    cnt_ref[...] += jnp.sum(oh, axis=0, keepdims=True)


def _plan_call(route):
    total = route.shape[0]
    return pl.pallas_call(
        _plan_kernel,
        grid=(total // ROUTE_TILE,),
        in_specs=[pl.BlockSpec((ROUTE_TILE, LANES), lambda i: (i, 0))],
        out_specs=[pl.BlockSpec((ROUTE_TILE, LANES), lambda i: (i, 0)),
                   pl.BlockSpec((1, LANES), lambda i: (0, 0))],
        out_shape=[jax.ShapeDtypeStruct((total, LANES), F32), jax.ShapeDtypeStruct((1, LANES), F32)],
        compiler_params=pltpu.CompilerParams(dimension_semantics=("arbitrary",)),
        name="moe_plan",
    )(route)


def _row_copy(src_ref, src_row, dst_ref, dst_row, sem):
    return pltpu.make_async_copy(src_ref.at[pl.ds(src_row, 1), :], dst_ref.at[pl.ds(dst_row, 1), :], sem)


def _dispatch_kernel(pos1_ref, pos2_ref, pstart_ref, plen_ref, nact_ref, x_ref, xs_ref, zero_scr, sem, zsem,
                     *, n_exp, n_tiles):
    step = pl.program_id(0)
    base = step * ROUTE_TILE

    def pad_copies(e, op):
        start = pstart_ref[e]
        length = plen_ref[e]
        lead = (-start) & (SUBLANES - 1)
        for r in range(SUBLANES - 1):
            @pl.when(r < lead)
            def _():
                op(_row_copy(zero_scr, 0, xs_ref, start + r, zsem))

        aligned = start + lead
        rest = length - lead
        for b in range(SUBLANES.bit_length() - 1, EXPERT_TILE.bit_length() - 1):
            size = 1 << b

            @pl.when(((rest >> b) & 1) == 1)
            def _():
                off = pl.multiple_of(aligned + (rest & (size - 1)), SUBLANES)
                op(pltpu.make_async_copy(zero_scr.at[pl.ds(0, size), :], xs_ref.at[pl.ds(off, size), :], zsem))

    def tail_copy(tile):
        row0 = pl.multiple_of(tile * EXPERT_TILE, EXPERT_TILE)
        return pltpu.make_async_copy(zero_scr, xs_ref.at[pl.ds(row0, EXPERT_TILE), :], zsem)

    def for_all_fill(op):
        def per_expert(e, c):
            pad_copies(e, op)
            return c

        def per_tile(tile, c):
            op(tail_copy(tile))
            return c

        lax.fori_loop(0, n_exp, per_expert, 0)
        lax.fori_loop(nact_ref[0], n_tiles, per_tile, 0)

    @pl.when(step == 0)
    def _fill():
        zero_scr[...] = jnp.zeros_like(zero_scr)
        for_all_fill(lambda cp: cp.start())
        for_all_fill(lambda cp: cp.wait())

    def issue(r, c):
        _row_copy(x_ref, r, xs_ref, pos1_ref[base + r], sem).start()
        _row_copy(x_ref, r, xs_ref, pos2_ref[base + r], sem).start()
        return c

    def drain(r, c):
        _row_copy(x_ref, r, xs_ref, pos1_ref[base + r], sem).wait()
        _row_copy(x_ref, r, xs_ref, pos2_ref[base + r], sem).wait()
        return c

    lax.fori_loop(0, ROUTE_TILE, issue, 0)
    lax.fori_loop(0, ROUTE_TILE, drain, 0)


def _dispatch_call(pos1, pos2, pad_start, pad_len, n_active, x_flat, n_tiles):
    total, d_model = x_flat.shape
    n_exp = pad_start.shape[0]
    return pl.pallas_call(
        functools.partial(_dispatch_kernel, n_exp=n_exp, n_tiles=n_tiles),
        grid_spec=pltpu.PrefetchScalarGridSpec(
            num_scalar_prefetch=5,
            grid=(total // ROUTE_TILE,),
            in_specs=[pl.BlockSpec((ROUTE_TILE, d_model), lambda i, *_: (i, 0))],
            out_specs=pl.BlockSpec(memory_space=pl.ANY),
            scratch_shapes=[pltpu.VMEM((EXPERT_TILE, d_model), F32), pltpu.SemaphoreType.DMA,
                            pltpu.SemaphoreType.DMA],
        ),
        out_shape=jax.ShapeDtypeStruct((n_tiles * EXPERT_TILE, d_model), F32),
        compiler_params=pltpu.CompilerParams(dimension_semantics=("arbitrary",)),
        name="moe_dispatch",
    )(pos1, pos2, pad_start, pad_len, n_active, x_flat)


def _experts_kernel(blk_ref, exp_ref, nact_ref, xs_ref, gffn_ref, wg_ref, wu_ref, wd_ref, ys_ref, wgu_scr, wd_scr):
    i = pl.program_id(0)
    f = wg_ref.shape[-1]
    active = i < nact_ref[0]
    changed = jnp.logical_or(i == 0, exp_ref[i] != exp_ref[jnp.maximum(i - 1, 0)])

    @pl.when(jnp.logical_and(active, changed))
    def _():
        wgu_scr[:, 0:f] = wg_ref[...].astype(BF16)
        wgu_scr[:, f:2 * f] = wu_ref[...].astype(BF16)
        wd_scr[...] = wd_ref[...].astype(BF16)

    @pl.when(active)
    def _():
        x = xs_ref[...]
        ms = jnp.mean(x * x, axis=-1, keepdims=True)
        t = (x * lax.rsqrt(ms + EPS) * gffn_ref[...]).astype(BF16)
        gu = _dot(t, wgu_scr[...])
        act = jax.nn.silu(gu[:, 0:f]) * gu[:, f:2 * f]
        ys_ref[...] = _dot(act.astype(BF16), wd_scr[...])

    @pl.when(jnp.logical_not(active))
    def _():
        ys_ref[...] = jnp.zeros_like(ys_ref)


def _experts_call(tile_blk, tile_exp, n_active, xs, g_ffn, w_gate, w_up, w_down):
    nrows, d_model = xs.shape
    n_exp, _, f = w_gate.shape
    n_tiles = nrows // EXPERT_TILE
    return pl.pallas_call(
        _experts_kernel,
        grid_spec=pltpu.PrefetchScalarGridSpec(
            num_scalar_prefetch=3,
            grid=(n_tiles,),
            in_specs=[pl.BlockSpec((EXPERT_TILE, d_model), lambda i, blk, ex, na: (blk[i], 0)),
                      pl.BlockSpec((1, d_model), lambda i, blk, ex, na: (0, 0)),
                      pl.BlockSpec((None, d_model, f), lambda i, blk, ex, na: (ex[i], 0, 0)),
                      pl.BlockSpec((None, d_model, f), lambda i, blk, ex, na: (ex[i], 0, 0)),
                      pl.BlockSpec((None, f, d_model), lambda i, blk, ex, na: (ex[i], 0, 0))],
            out_specs=pl.BlockSpec((EXPERT_TILE, d_model), lambda i, blk, ex, na: (i, 0)),
            scratch_shapes=[pltpu.VMEM((d_model, 2 * f), BF16), pltpu.VMEM((f, d_model), BF16)],
        ),
        out_shape=jax.ShapeDtypeStruct((nrows, d_model), F32),
        compiler_params=pltpu.CompilerParams(dimension_semantics=("arbitrary",), vmem_limit_bytes=VMEM_LIMIT),
        name="moe_experts",
    )(tile_blk, tile_exp, n_active, xs, g_ffn, w_gate, w_up, w_down)


def _combine_kernel(pos1_ref, pos2_ref, x1_ref, route_ref, gfin_ref, ys_ref, out_ref, buf1, buf2, sem,
                    *, final_norm):
    base = pl.program_id(0) * ROUTE_TILE

    def issue(r, c):
        _row_copy(ys_ref, pos1_ref[base + r], buf1, r, sem).start()
        _row_copy(ys_ref, pos2_ref[base + r], buf2, r, sem).start()
        return c

    def drain(r, c):
        _row_copy(ys_ref, pos1_ref[base + r], buf1, r, sem).wait()
        _row_copy(ys_ref, pos2_ref[base + r], buf2, r, sem).wait()
        return c

    lax.fori_loop(0, ROUTE_TILE, issue, 0)
    lax.fori_loop(0, ROUTE_TILE, drain, 0)
    route = route_ref[...]
    x2 = x1_ref[...] + route[:, 2:3] * buf1[...] + route[:, 3:4] * buf2[...]
    if final_norm:
        ms = jnp.mean(x2 * x2, axis=-1, keepdims=True)
        x2 = x2 * lax.rsqrt(ms + EPS) * gfin_ref[...]
    out_ref[...] = x2


def _combine_call(pos1, pos2, x1, route, g_final, ys, *, final_norm):
    total, d_model = x1.shape
    return pl.pallas_call(
        functools.partial(_combine_kernel, final_norm=final_norm),
        grid_spec=pltpu.PrefetchScalarGridSpec(
            num_scalar_prefetch=2,
            grid=(total // ROUTE_TILE,),
            in_specs=[pl.BlockSpec((ROUTE_TILE, d_model), lambda i, p1, p2: (i, 0)),
                      pl.BlockSpec((ROUTE_TILE, LANES), lambda i, p1, p2: (i, 0)),
                      pl.BlockSpec((1, d_model), lambda i, p1, p2: (0, 0)),
                      pl.BlockSpec(memory_space=pl.ANY)],
            out_specs=pl.BlockSpec((ROUTE_TILE, d_model), lambda i, p1, p2: (i, 0)),
            scratch_shapes=[pltpu.VMEM((ROUTE_TILE, d_model), F32), pltpu.VMEM((ROUTE_TILE, d_model), F32),
                            pltpu.SemaphoreType.DMA],
        ),
        out_shape=jax.ShapeDtypeStruct((total, d_model), F32),
        input_output_aliases={2: 0},
        compiler_params=pltpu.CompilerParams(dimension_semantics=("arbitrary",)),
        name="moe_combine",
    )(pos1, pos2, x1, route, g_final, ys)


def _moe(x1, route, g_ffn, w_gate, w_up, w_down, g_final, *, final_norm):
    total, d_model = x1.shape
    n_exp = w_gate.shape[0]
    rank, counts = _plan_call(route)
    counts = counts[0, :n_exp].astype(jnp.int32)
    seg_tiles = (counts + EXPERT_TILE - 1) // EXPERT_TILE
    seg_end = jnp.cumsum(seg_tiles)
    seg_off = (seg_end - seg_tiles) * EXPERT_TILE
    e1 = route[:, 0].astype(jnp.int32)
    e2 = route[:, 1].astype(jnp.int32)
    pos1 = seg_off[e1] + rank[:, 0].astype(jnp.int32)
    pos2 = seg_off[e2] + rank[:, 1].astype(jnp.int32)
    n_tiles = (2 * total + n_exp * (EXPERT_TILE - 1)) // EXPERT_TILE + 1
    n_active = seg_end[-1].reshape(1).astype(jnp.int32)
    tile_blk = jnp.minimum(jnp.arange(n_tiles, dtype=jnp.int32), n_active - 1)
    tile_exp = jnp.sum((tile_blk[:, None] >= seg_end[None, :]).astype(jnp.int32), axis=1)
    tile_exp = jnp.minimum(tile_exp, n_exp - 1)
    xs = _dispatch_call(pos1, pos2, seg_off + counts, seg_tiles * EXPERT_TILE - counts, n_active, x1, n_tiles)
    ys = _experts_call(tile_blk, tile_exp, n_active, xs, g_ffn, w_gate, w_up, w_down)
    return _combine_call(pos1, pos2, x1, route, g_final, ys, final_norm=final_norm)


def _to_rows(x, bblk):
    b, l, d = x.shape
    return x.reshape(b // bblk, bblk, l, d).transpose(0, 2, 1, 3).reshape(b * l, d)


def _from_rows(rows, b, bblk, l):
    d = rows.shape[-1]
    return rows.reshape(b // bblk, l, bblk, d).transpose(0, 2, 1, 3).reshape(b, l, d)


def kernel(x_prompt, x_sample, state_ssm_re, state_ssm_im, state_pool, meta_tokens, norm_mix_g, w_in, ssm_a_re,
           ssm_a_im, ssm_log_dt, ssm_b_re, ssm_b_im, ssm_c_re, ssm_c_im, ssm_d, w_glu, b_glu, w_pool, pool_scale,
           w_out, norm_ffn_g, w_router_group, b_router_group, w_router_expert, b_router_expert, w_gate, w_up,
           w_down, norm_final_g):
    batch, seq, d_model = x_prompt.shape
    dec_batch, dec_seq, _ = x_sample.shape
    depth, _, groups, nstate = state_ssm_re.shape
    n_meta = meta_tokens.shape[0]
    pool_buf, d_pool = state_pool.shape[2], state_pool.shape[3]
    n_pool_groups = w_pool.shape[1]
    windows = tuple(2 ** (k + 1) for k in range(n_pool_groups))
    assert pool_buf == max(windows) - 1 == POOL_HIST - 1
    n_expert_groups = w_router_group.shape[-1]
    n_experts = w_router_expert.shape[-1]
    cfg = dict(windows=windows, n_expert_groups=n_expert_groups, experts_per_group=n_experts // n_expert_groups)

    step_rows = 512
    p_tile_l, s_bblk, m_batch = step_rows // batch, step_rows // dec_seq, step_rows // n_meta
    n_p, n_s, n_m = batch * seq, dec_batch * dec_seq, m_batch * n_meta
    assert m_batch >= batch and seq % p_tile_l == 0 and dec_batch % s_bblk == 0
    total = n_p + n_s + n_m
    assert total % step_rows == 0 and step_rows % ROUTE_TILE == 0

    mq, pmat, a8 = _s5_prepare(ssm_a_re, ssm_a_im, ssm_log_dt, ssm_b_re, ssm_b_im, ssm_c_re, ssm_c_im)

    meta_b = jnp.broadcast_to(meta_tokens.astype(F32)[None], (m_batch, n_meta, d_model))
    x = jnp.concatenate([_to_rows(x_prompt, batch), _to_rows(x_sample, s_bblk), _to_rows(meta_b, m_batch)], axis=0)

    row2 = lambda v: v.reshape(1, -1).astype(F32)
    outs = {k: [] for k in ("hp", "bp", "hs", "bs")}
    for l in range(depth):
        w_r = jnp.concatenate([w_router_group[l], w_router_expert[l]], axis=1).astype(F32)
        w_r = jnp.pad(w_r, ((0, 0), (0, LANES - w_r.shape[1])))
        wr_hi = w_r.astype(BF16)
        b_r = jnp.concatenate([b_router_group[l], b_router_expert[l]]).astype(F32)
        lw = dict(g_mix=row2(norm_mix_g[l]), w_in=w_in[l].astype(BF16), mq=mq[l], pmat=pmat[l], a8=a8[l],
                  ssm_d=row2(ssm_d[l]), w_glu=w_glu[l].astype(BF16), b_glu=row2(b_glu[l]),
                  w_pool=w_pool[l].astype(BF16), pool_scale=row2(pool_scale[l]), w_out=w_out[l].astype(BF16),
                  g_ffn=row2(norm_ffn_g[l]), wr_hi=wr_hi, wr_lo=(w_r - wr_hi.astype(F32)).astype(BF16),
                  b_r=jnp.pad(b_r, (0, LANES - b_r.shape[0])).reshape(1, LANES))

        zero_h = jnp.zeros((groups, m_batch, LANES), F32)
        zero_buf = jnp.zeros((POOL_HIST, m_batch, d_pool), F32)
        route = jnp.zeros((total, LANES), F32)
        x, route, h_m, buf_m = _mixer_call(
            x, route, zero_h, zero_buf, lw, row_block=(n_p + n_s) // step_rows, batch=m_batch, bblk=m_batch,
            length=n_meta, tile_l=n_meta, pos0=0, cfg=cfg)
        x, route, h_p, buf_p = _mixer_call(
            x, route, h_m[:, :batch], buf_m[:, :batch], lw, row_block=0, batch=batch, bblk=batch,
            length=seq, tile_l=p_tile_l, pos0=n_meta, cfg=cfg)
        h0_s = jnp.concatenate([state_ssm_re[l], state_ssm_im[l]], axis=-1).transpose(1, 0, 2).astype(F32)
        buf0_s = jnp.pad(state_pool[l].astype(F32).transpose(1, 0, 2), ((1, 0), (0, 0), (0, 0)))
        x, route, h_s, buf_s = _mixer_call(
            x, route, h0_s, buf0_s, lw, row_block=n_p // step_rows, batch=dec_batch, bblk=s_bblk,
            length=dec_seq, tile_l=dec_seq, pos0=PAST_LEN, cfg=cfg)
        outs["hp"].append(h_p)
        outs["bp"].append(buf_p)
        outs["hs"].append(h_s)
        outs["bs"].append(buf_s)
        x = _moe(x, route, lw["g_ffn"], w_gate[l], w_up[l], w_down[l], row2(norm_final_g),
                 final_norm=(l == depth - 1))

    y_prompt = _from_rows(x[:n_p], batch, batch, seq)
    y_sample = _from_rows(x[n_p:n_p + n_s], dec_batch, s_bblk, dec_seq)
    st = lambda hs: jnp.stack(hs).transpose(0, 2, 1, 3)
    pl_out = lambda bs: jnp.stack(bs)[:, 1:].transpose(0, 2, 1, 3)
    hp, hs = st(outs["hp"]), st(outs["hs"])
    return (y_prompt, y_sample, hp[..., :nstate], hp[..., nstate:], pl_out(outs["bp"]),
            hs[..., :nstate], hs[..., nstate:], pl_out(outs["bs"]))
```

```python
import functools

import numpy as np
import jax
import jax.numpy as jnp
from jax import lax
from jax.experimental import pallas as pl
from jax.experimental.pallas import tpu as pltpu

F32 = jnp.float32
BF16 = jnp.bfloat16

LANES = 128
SUBLANES = 8
CHUNK = 8
POOL_HIST = 16
MOE_TILE = 256
EXPERT_TILE = 256
VMEM_LIMIT = 56 * 1024 * 1024
EPS = 1e-6
NEG = -1e30
PAST_LEN = 16384


def _dot(a, b):
    return jnp.dot(a, b, preferred_element_type=F32)


def _mm(a, w, precise):
    if precise:
        return jnp.dot(a, w, precision=lax.Precision.HIGHEST, preferred_element_type=F32)
    return _dot(a.astype(BF16), w)


def _dot_nt_f32(a, b):
    return lax.dot_general(a, b, (((1,), (1,)), ((), ())), precision=lax.Precision.HIGHEST,
                           preferred_element_type=F32)


def _s5_prep_kernel(are_ref, aim_ref, ldt_ref, btr_ref, bti_ref, cr_ref, ci_ref,
                    k_ref, wtr_ref, wti_ref, er_ref, ei_ref, a8r_ref, a8i_ref):
    a_re = are_ref[...]
    a_im = aim_ref[...]
    dt = jnp.exp(ldt_ref[...])
    mag = jnp.exp(dt * a_re)
    abr = mag * jnp.cos(dt * a_im)
    abi = mag * jnp.sin(dt * a_im)
    den = a_re * a_re + a_im * a_im
    nr = abr - 1.0
    ni = abi
    f_re = (nr * a_re + ni * a_im) / den
    f_im = (ni * a_re - nr * a_im) / den
    btr = btr_ref[...]
    bti = bti_ref[...]
    bbr = f_re * btr - f_im * bti
    bbi = f_re * bti + f_im * btr
    cr = cr_ref[...]
    ci = ci_ref[...]
    pr = jnp.ones_like(a_re)
    pi = jnp.zeros_like(a_re)
    for k in range(CHUNK + 1):
        ekr = cr * pr - ci * pi
        eki = cr * pi + ci * pr
        if k < CHUNK:
            wtr_ref[k] = pr * bbr - pi * bbi
            wti_ref[k] = pr * bbi + pi * bbr
            k_ref[k] = _dot_nt_f32(ekr, bbr) - _dot_nt_f32(eki, bbi)
        if k >= 1:
            er_ref[k - 1] = ekr
            ei_ref[k - 1] = eki
        if k < CHUNK:
            pr, pi = pr * abr - pi * abi, pr * abi + pi * abr
    a8r_ref[...] = pr
    a8i_ref[...] = pi


def _s5_prepare(a_re, a_im, log_dt, b_re, b_im, c_re, c_im):
    depth, groups, nstate = a_re.shape
    ch = b_re.shape[-1]
    dg = depth * groups
    assert ch * CHUNK == LANES and 2 * nstate == LANES and groups % CHUNK == 0
    r3 = lambda x: x.reshape(dg, 1, -1).astype(F32)
    btr = jnp.swapaxes(b_re.reshape(dg, nstate, ch), 1, 2).astype(F32)
    bti = jnp.swapaxes(b_im.reshape(dg, nstate, ch), 1, 2).astype(F32)
    cr = c_re.reshape(dg, ch, nstate).astype(F32)
    ci = c_im.reshape(dg, ch, nstate).astype(F32)
    vec = pl.BlockSpec((None, 1, nstate), lambda g: (g, 0, 0))
    mat = pl.BlockSpec((None, ch, nstate), lambda g: (g, 0, 0))
    kmat = pl.BlockSpec((None, CHUNK, ch, nstate), lambda g: (g, 0, 0, 0))
    kk, wtr, wti, er, ei, a8r, a8i = pl.pallas_call(
        _s5_prep_kernel,
        grid=(dg,),
        in_specs=[vec, vec, pl.BlockSpec((None, 1, 1), lambda g: (g, 0, 0)), mat, mat, mat, mat],
        out_specs=[pl.BlockSpec((None, CHUNK, ch, ch), lambda g: (g, 0, 0, 0)), kmat, kmat, kmat, kmat, vec, vec],
        out_shape=[jax.ShapeDtypeStruct((dg, CHUNK, ch, ch), F32)]
        + [jax.ShapeDtypeStruct((dg, CHUNK, ch, nstate), F32)] * 4
        + [jax.ShapeDtypeStruct((dg, 1, nstate), F32)] * 2,
        name="s5_prep",
    )(r3(a_re), r3(a_im), r3(log_dt), btr, bti, cr, ci)

    slot = np.arange(CHUNK)
    s_of_g = np.arange(dg) % CHUNK
    jpos = (slot[None, :] - s_of_g[:, None]) % CHUNK
    lag = jpos[:, None, :] - jpos[:, :, None]
    causal = jnp.asarray(lag >= 0, F32)[:, :, :, None, None]
    gi = np.arange(dg)[:, None, None]
    m5 = kk[gi, np.maximum(lag, 0)] * causal
    m = m5.transpose(0, 1, 4, 2, 3).reshape(dg, LANES, LANES)
    gi2 = np.arange(dg)[:, None]
    qsel = CHUNK - 1 - jpos
    q = jnp.concatenate([wtr[gi2, qsel], wti[gi2, qsel]], axis=-1).reshape(dg, LANES, LANES)
    mq = jnp.concatenate([m, q], axis=-1)
    ptr = er[gi2, jpos].reshape(dg, LANES, nstate)
    pti = ei[gi2, jpos].reshape(dg, LANES, nstate)
    pmat = jnp.concatenate([jnp.swapaxes(ptr, 1, 2), -jnp.swapaxes(pti, 1, 2)], axis=1)
    a8 = jnp.concatenate([
        jnp.concatenate([a8r, a8r], axis=-1),
        jnp.concatenate([-a8i, a8i], axis=-1),
        jnp.concatenate([a8i, -a8i], axis=-1)], axis=1)
    shp = lambda x: x.reshape((depth, groups) + x.shape[1:])
    return shp(mq), shp(pmat), shp(a8)


def _mixer_kernel(x_ref, h0_ref, buf0_ref, gmix_ref, win_ref, mq_ref, p_ref, a8_ref, d_ref, wglu_ref,
                  bglu_ref, wpool_ref, pscale_ref, wout_ref, gffn_ref, wrhi_ref, wrlo_ref, br_ref,
                  route_in_ref, routet_in_ref, x1_ref, route_ref, routet_ref, h_ref, buf_ref, u_scr, y_scr, cat_scr,
                  *, bblk, tile_l, pos0, windows, n_expert_groups, experts_per_group, precise):
    del route_in_ref, routet_in_ref
    li = pl.program_id(1)
    rows = tile_l * bblk
    nk = tile_l // CHUNK
    n = nk * bblk
    d_model = x_ref.shape[-1]
    d_ssm = y_scr.shape[-1]
    d_pool = d_model - d_ssm
    hist = POOL_HIST * bblk

    @pl.when(li == 0)
    def _init():
        h_ref[...] = h0_ref[...]
        cat_scr[0:hist, :] = buf0_ref[...].reshape(hist, d_pool)

    x = x_ref[...]
    ms = jnp.mean(x * x, axis=-1, keepdims=True)
    h = x * lax.rsqrt(ms + EPS) * gmix_ref[...]
    u = _mm(h, win_ref[...], precise)
    u_scr[...] = u.reshape(nk, CHUNK, bblk, d_model)

    slot = lax.broadcasted_iota(jnp.int32, (n, LANES), 1) // (LANES // CHUNK)
    for v in range(d_ssm // LANES):
        rolled = []
        for j in range(CHUNK):
            uj = u_scr[:, j, :, v * LANES:(v + 1) * LANES].reshape(n, LANES)
            rolled.append(pltpu.roll(uj, (LANES // CHUNK) * j, axis=1) if j else uj)
        ys = []
        for s in range(CHUNK):
            g = v * CHUNK + s
            ug = rolled[0]
            for j in range(1, CHUNK):
                ug = jnp.where(slot == (s + j) % CHUNK, rolled[j], ug)
            yq = _mm(ug, mq_ref[g], precise)
            sg = yq[:, LANES:]
            sg_sw = pltpu.roll(sg, LANES // 2, axis=1)
            a8c = a8_ref[g, 0:1, :]
            a8s = a8_ref[g, 1:2, :]
            a8w = a8_ref[g, 2:3, :]
            hcur = h_ref[g]
            hsw = pltpu.roll(hcur, LANES // 2, axis=1)
            hins = []
            for k in range(nk):
                hins.append(hcur)
                sk = sg[k * bblk:(k + 1) * bblk]
                skw = sg_sw[k * bblk:(k + 1) * bblk]
                hcur, hsw = a8c * hcur + a8s * hsw + sk, a8c * hsw + a8w * hcur + skw
            h_ref[g] = hcur
            hin = jnp.concatenate(hins, axis=0) if nk > 1 else hins[0]
            ys.append(yq[:, :LANES] + _mm(hin, p_ref[g], precise))
        for j in range(CHUNK):
            z = ys[0]
            for s in range(1, CHUNK):
                z = jnp.where(slot == (s + j) % CHUNK, ys[s], z)
            if j:
                z = pltpu.roll(z, LANES - (LANES // CHUNK) * j, axis=1)
            y_scr[:, j, :, v * LANES:(v + 1) * LANES] = z.reshape(nk, bblk, LANES)

    u_ssm = u_scr[:, :, :, 0:d_ssm].reshape(rows, d_ssm)
    y = y_scr[...].reshape(rows, d_ssm) + d_ref[...] * u_ssm
    gl = jax.nn.gelu(y)
    ssm_out = gl * jax.nn.sigmoid(_mm(gl, wglu_ref[...], precise) + bglu_ref[...])

    up = u_scr[:, :, :, d_ssm:].reshape(rows, d_pool)
    cat_scr[hist:hist + rows, :] = up
    row_pos = lax.broadcasted_iota(jnp.int32, (rows, 1), 0) // bblk
    pos = pos0 + li * tile_l + row_pos
    pool_parts = []
    pgrp = d_pool // len(windows)
    for kk, w in enumerate(windows):
        lo = kk * pgrp
        acc = cat_scr[hist:hist + rows, lo:lo + pgrp]
        cur = acc
        for sft in range(1, w):
            acc = acc + cat_scr[hist - sft * bblk:hist - sft * bblk + rows, lo:lo + pgrp]
        inv = 1.0 / jnp.minimum(pos + 1, w).astype(F32)
        diff = acc * inv - cur
        pool_parts.append(_mm(diff, wpool_ref[kk], precise))
    pool_out = jnp.concatenate(pool_parts, axis=-1) * pscale_ref[...]
    tail = cat_scr[rows:rows + hist, :]
    cat_scr[0:hist, :] = tail
    buf_ref[...] = tail.reshape(POOL_HIST, bblk, d_pool)

    if precise:
        mix = jnp.concatenate([ssm_out, pool_out], axis=-1)
    else:
        mix = jnp.concatenate([ssm_out.astype(BF16), pool_out.astype(BF16)], axis=-1)
    x1 = x + _mm(mix, wout_ref[...], precise)
    x1_ref[...] = x1
    ms2 = jnp.mean(x1 * x1, axis=-1, keepdims=True)
    t = x1 * lax.rsqrt(ms2 + EPS) * gffn_ref[...]

    t_hi = t.astype(BF16)
    t_lo = (t - t_hi.astype(F32)).astype(BF16)
    logits = (_dot(t_hi, wrhi_ref[...]) + _dot(t_lo, wrhi_ref[...]) + _dot(t_hi, wrlo_ref[...])
              + br_ref[...])
    lane = lax.broadcasted_iota(jnp.int32, (rows, LANES), 1)
    big = jnp.int32(4 * LANES)
    lg = jnp.where(lane < n_expert_groups, logits, NEG)
    mg = jnp.max(lg, axis=-1, keepdims=True)
    gsel = jnp.min(jnp.where(lg == mg, lane, big), axis=-1, keepdims=True)
    p_sel = 1.0 / jnp.sum(jnp.exp(lg - mg), axis=-1, keepdims=True)
    e_lo = n_expert_groups + gsel * experts_per_group
    le = jnp.where((lane >= e_lo) & (lane < e_lo + experts_per_group), logits, NEG)
    v1 = jnp.max(le, axis=-1, keepdims=True)
    i1 = jnp.min(jnp.where(le == v1, lane, big), axis=-1, keepdims=True)
    le2 = jnp.where(lane == i1, NEG, le)
    v2 = jnp.max(le2, axis=-1, keepdims=True)
    i2 = jnp.min(jnp.where(le2 == v2, lane, big), axis=-1, keepdims=True)
    ex = jnp.exp(v2 - v1)
    q1 = p_sel / (1.0 + ex)
    q2 = q1 * ex
    e1 = (i1 - n_expert_groups).astype(F32)
    e2 = (i2 - n_expert_groups).astype(F32)
    route = jnp.where(lane == 0, e1, jnp.where(lane == 1, e2, jnp.where(lane == 2, q1,
                      jnp.where(lane == 3, q2, 0.0))))
    route_ref[...] = route
    routet_ref[...] = route.T[0:SUBLANES, :]


def _mixer_call(x_flat, route_flat, routet_flat, h0, buf0, lw, *, row_block, batch, bblk, length, tile_l, pos0,
                cfg, precise=False):
    total, d_model = x_flat.shape
    groups = h0.shape[0]
    d_pool = buf0.shape[-1]
    d_ssm = d_model - d_pool
    nb = batch // bblk
    nl = length // tile_l
    rows = tile_l * bblk

    def full(a):
        nd = a.ndim
        return pl.BlockSpec(a.shape, lambda b, l, _n=nd: (0,) * _n)

    tok = lambda width: pl.BlockSpec((rows, width), lambda b, l: (row_block + b * nl + l, 0))
    weights = [lw["g_mix"], lw["w_in"], lw["mq"], lw["pmat"], lw["a8"], lw["ssm_d"], lw["w_glu"], lw["b_glu"],
               lw["w_pool"], lw["pool_scale"], lw["w_out"], lw["g_ffn"], lw["wr_hi"], lw["wr_lo"], lw["b_r"]]
    in_specs = ([tok(d_model),
                 pl.BlockSpec((groups, bblk, LANES), lambda b, l: (0, b, 0)),
                 pl.BlockSpec((POOL_HIST, bblk, d_pool), lambda b, l: (0, b, 0))]
                + [full(w) for w in weights]
                + [pl.BlockSpec(memory_space=pl.ANY)] * 2)
    out_specs = [tok(d_model), tok(LANES),
                 pl.BlockSpec((SUBLANES, rows), lambda b, l: (0, row_block + b * nl + l)),
                 pl.BlockSpec((groups, bblk, LANES), lambda b, l: (0, b, 0)),
                 pl.BlockSpec((POOL_HIST, bblk, d_pool), lambda b, l: (0, b, 0))]
    out_shape = [jax.ShapeDtypeStruct((total, d_model), F32), jax.ShapeDtypeStruct((total, LANES), F32),
                 jax.ShapeDtypeStruct((SUBLANES, total), F32),
                 jax.ShapeDtypeStruct((groups, batch, LANES), F32),
                 jax.ShapeDtypeStruct((POOL_HIST, batch, d_pool), F32)]
    n_in = 3 + len(weights)
    kern = functools.partial(_mixer_kernel, bblk=bblk, tile_l=tile_l, pos0=pos0, windows=cfg["windows"],
                             n_expert_groups=cfg["n_expert_groups"], experts_per_group=cfg["experts_per_group"],
                             precise=precise)
    args = [x_flat, h0, buf0] + weights + [route_flat, routet_flat]
    return pl.pallas_call(
        kern,
        grid=(nb, nl),
        in_specs=in_specs,
        out_specs=out_specs,
        out_shape=out_shape,
        scratch_shapes=[pltpu.VMEM((tile_l // CHUNK, CHUNK, bblk, d_model), F32),
                        pltpu.VMEM((tile_l // CHUNK, CHUNK, bblk, d_ssm), F32),
                        pltpu.VMEM(((POOL_HIST + tile_l) * bblk, d_pool), F32)],
        input_output_aliases={0: 0, n_in: 1, n_in + 1: 2},
        compiler_params=pltpu.CompilerParams(dimension_semantics=("arbitrary", "arbitrary"),
                                             vmem_limit_bytes=VMEM_LIMIT),
        name="mixer",
    )(*args)


def _plan_kernel(route_ref, cnt_ref):
    r = route_ref[...]
    lanef = lax.broadcasted_iota(jnp.int32, r.shape, 1).astype(F32)
    oh = jnp.logical_or(lanef == r[:, 0:1], lanef == r[:, 1:2]).astype(F32)
    cnt_ref[...] = jnp.sum(oh, axis=0, keepdims=True)


def _plan_call(route):
    total = route.shape[0]
    nt = total // MOE_TILE
    return pl.pallas_call(
        _plan_kernel,
        grid=(nt,),
        in_specs=[pl.BlockSpec((MOE_TILE, LANES), lambda i: (i, 0))],
        out_specs=pl.BlockSpec((None, 1, LANES), lambda i: (i, 0, 0)),
        out_shape=jax.ShapeDtypeStruct((nt, 1, LANES), F32),
        compiler_params=pltpu.CompilerParams(dimension_semantics=("arbitrary",)),
        name="moe_plan",
    )(route)


def _chunk_copy(src_ref, src_row, dst_ref, dst_row, sem):
    src = src_ref.at[pl.ds(pl.multiple_of(src_row, SUBLANES), SUBLANES), :]
    dst = dst_ref.at[pl.ds(pl.multiple_of(dst_row, SUBLANES), SUBLANES), :]
    return pltpu.make_async_copy(src, dst, sem)


def _dispatch_kernel(cmap_ref, nch_ref, pstart_ref, plen_ref, nact_ref, x_ref, rt_ref, loff_ref, gffn_ref, xs_ref,
                     xloc, zero_scr, sem, zsem, *, n_exp, n_tiles, n_chunks):
    step = pl.program_id(0)

    def pad_copies(e, op):
        start = pstart_ref[e]
        length = plen_ref[e]
        for b in range(SUBLANES.bit_length() - 1, EXPERT_TILE.bit_length() - 1):
            size = 1 << b

            @pl.when(((length >> b) & 1) == 1)
            def _():
                off = pl.multiple_of(start + (length & (size - 1)), SUBLANES)
                op(pltpu.make_async_copy(zero_scr.at[pl.ds(0, size), :], xs_ref.at[pl.ds(off, size), :], zsem))

    def tail_copy(tile):
        row0 = pl.multiple_of(tile * EXPERT_TILE, EXPERT_TILE)
        return pltpu.make_async_copy(zero_scr, xs_ref.at[pl.ds(row0, EXPERT_TILE), :], zsem)

    def for_all_fill(op):
        def per_expert(e, c):
            pad_copies(e, op)
            return c

        def per_tile(tile, c):
            op(tail_copy(tile))
            return c

        lax.fori_loop(0, n_exp, per_expert, 0)
        lax.fori_loop(nact_ref[0], n_tiles, per_tile, 0)

    @pl.when(step == 0)
    def _fill():
        zero_scr[...] = jnp.zeros_like(zero_scr)
        for_all_fill(lambda cp: cp.start())
        for_all_fill(lambda cp: cp.wait())

    x = x_ref[...]
    tt = x.shape[0]
    ms = jnp.mean(x * x, axis=-1, keepdims=True)
    t = (x * lax.rsqrt(ms + EPS) * gffn_ref[...]).astype(BF16)

    e1 = rt_ref[0:1, :]
    e2 = rt_ref[1:2, :]
    eio = lax.broadcasted_iota(jnp.int32, (n_exp, tt), 0).astype(F32)
    oh1 = eio == e1
    oh2 = eio == e2
    before = lax.broadcasted_iota(jnp.int32, (tt, tt), 0) < lax.broadcasted_iota(jnp.int32, (tt, tt), 1)
    rank = _dot(jnp.logical_or(oh1, oh2).astype(BF16), before.astype(BF16))
    where_to = loff_ref[:, 0:1] + rank
    lpos1 = jnp.sum(jnp.where(oh1, where_to, 0.0), axis=0, keepdims=True).astype(jnp.int32)
    lpos2 = jnp.sum(jnp.where(oh2, where_to, 0.0), axis=0, keepdims=True).astype(jnp.int32)
    rio = lax.broadcasted_iota(jnp.int32, (xloc.shape[0], tt), 0)
    perm = jnp.logical_or(rio == lpos1, rio == lpos2).astype(BF16)
    xloc[...] = _dot(perm, t)

    base = step * n_chunks

    def chunk(c):
        return _chunk_copy(xloc, c * SUBLANES, xs_ref, cmap_ref[base + c], sem)

    def issue(c, carry):
        chunk(c).start()
        return carry

    def drain(c, carry):
        chunk(c).wait()
        return carry

    lax.fori_loop(0, nch_ref[step], issue, 0)
    lax.fori_loop(0, nch_ref[step], drain, 0)


def _dispatch_call(cmap, nch, pad_start, pad_len, n_active, x_flat, route_t, loff_col, g_ffn, n_tiles, n_chunks):
    total, d_model = x_flat.shape
    n_exp = pad_start.shape[0]
    return pl.pallas_call(
        functools.partial(_dispatch_kernel, n_exp=n_exp, n_tiles=n_tiles, n_chunks=n_chunks),
        grid_spec=pltpu.PrefetchScalarGridSpec(
            num_scalar_prefetch=5,
            grid=(total // MOE_TILE,),
            in_specs=[pl.BlockSpec((MOE_TILE, d_model), lambda i, *_: (i, 0)),
                      pl.BlockSpec((SUBLANES, MOE_TILE), lambda i, *_: (0, i)),
                      pl.BlockSpec((None, n_exp, LANES), lambda i, *_: (i, 0, 0)),
                      pl.BlockSpec((1, d_model), lambda i, *_: (0, 0))],
            out_specs=pl.BlockSpec(memory_space=pl.ANY),
            scratch_shapes=[pltpu.VMEM((n_chunks * SUBLANES, d_model), F32),
                            pltpu.VMEM((EXPERT_TILE, d_model), F32),
                            pltpu.SemaphoreType.DMA, pltpu.SemaphoreType.DMA],
        ),
        out_shape=jax.ShapeDtypeStruct((n_tiles * EXPERT_TILE, d_model), F32),
        compiler_params=pltpu.CompilerParams(dimension_semantics=("arbitrary",), vmem_limit_bytes=VMEM_LIMIT),
        name="moe_dispatch",
    )(cmap, nch, pad_start, pad_len, n_active, x_flat, route_t, loff_col, g_ffn)


def _experts_kernel(blk_ref, exp_ref, nact_ref, xs_ref, wg_ref, wu_ref, wd_ref, ys_ref, wgu_scr, wd_scr):
    i = pl.program_id(0)
    f = wg_ref.shape[-1]
    active = i < nact_ref[0]
    changed = jnp.logical_or(i == 0, exp_ref[i] != exp_ref[jnp.maximum(i - 1, 0)])

    @pl.when(jnp.logical_and(active, changed))
    def _():
        wgu_scr[:, 0:f] = wg_ref[...].astype(BF16)
        wgu_scr[:, f:2 * f] = wu_ref[...].astype(BF16)
        wd_scr[...] = wd_ref[...].astype(BF16)

    @pl.when(active)
    def _():
        gu = _dot(xs_ref[...].astype(BF16), wgu_scr[...])
        act = jax.nn.silu(gu[:, 0:f]) * gu[:, f:2 * f]
        ys_ref[...] = _dot(act.astype(BF16), wd_scr[...])

    @pl.when(jnp.logical_not(active))
    def _():
        ys_ref[...] = jnp.zeros_like(ys_ref)


def _experts_call(tile_blk, tile_exp, n_active, xs, w_gate, w_up, w_down):
    nrows, d_model = xs.shape
    n_exp, _, f = w_gate.shape
    n_tiles = nrows // EXPERT_TILE
    return pl.pallas_call(
        _experts_kernel,
        grid_spec=pltpu.PrefetchScalarGridSpec(
            num_scalar_prefetch=3,
            grid=(n_tiles,),
            in_specs=[pl.BlockSpec((EXPERT_TILE, d_model), lambda i, blk, ex, na: (blk[i], 0)),
                      pl.BlockSpec((None, d_model, f), lambda i, blk, ex, na: (ex[i], 0, 0)),
                      pl.BlockSpec((None, d_model, f), lambda i, blk, ex, na: (ex[i], 0, 0)),
                      pl.BlockSpec((None, f, d_model), lambda i, blk, ex, na: (ex[i], 0, 0))],
            out_specs=pl.BlockSpec((EXPERT_TILE, d_model), lambda i, blk, ex, na: (i, 0)),
            scratch_shapes=[pltpu.VMEM((d_model, 2 * f), BF16), pltpu.VMEM((f, d_model), BF16)],
        ),
        out_shape=jax.ShapeDtypeStruct((nrows, d_model), F32),
        compiler_params=pltpu.CompilerParams(dimension_semantics=("arbitrary",), vmem_limit_bytes=VMEM_LIMIT),
        name="moe_experts",
    )(tile_blk, tile_exp, n_active, xs, w_gate, w_up, w_down)


def _combine_kernel(cmap_ref, nch_ref, x1_ref, route_ref, loff_ref, gfin_ref, ys_ref, out_ref, yloc, sem,
                    *, final_norm, n_chunks):
    step = pl.program_id(0)
    base = step * n_chunks

    @pl.when(step == 0)
    def _():
        yloc[...] = jnp.zeros_like(yloc)

    def chunk(c):
        return _chunk_copy(ys_ref, cmap_ref[base + c], yloc, c * SUBLANES, sem)

    def issue(c, carry):
        chunk(c).start()
        return carry

    def drain(c, carry):
        chunk(c).wait()
        return carry

    lax.fori_loop(0, nch_ref[step], issue, 0)

    route = route_ref[...]
    tt = route.shape[0]
    lanef = lax.broadcasted_iota(jnp.int32, route.shape, 1).astype(F32)
    oh1 = lanef == route[:, 0:1]
    oh2 = lanef == route[:, 1:2]
    earlier = lax.broadcasted_iota(jnp.int32, (tt, tt), 0) > lax.broadcasted_iota(jnp.int32, (tt, tt), 1)
    rank = _dot(earlier.astype(BF16), jnp.logical_or(oh1, oh2).astype(BF16))
    where_to = loff_ref[...] + rank
    lpos1 = jnp.sum(jnp.where(oh1, where_to, 0.0), axis=-1, keepdims=True).astype(jnp.int32)
    lpos2 = jnp.sum(jnp.where(oh2, where_to, 0.0), axis=-1, keepdims=True).astype(jnp.int32)
    lio = lax.broadcasted_iota(jnp.int32, (tt, yloc.shape[0]), 1)
    wperm = jnp.where(lio == lpos1, route[:, 2:3], jnp.where(lio == lpos2, route[:, 3:4], 0.0)).astype(BF16)

    lax.fori_loop(0, nch_ref[step], drain, 0)
    x2 = x1_ref[...] + _dot(wperm, yloc[...].astype(BF16))
    if final_norm:
        ms = jnp.mean(x2 * x2, axis=-1, keepdims=True)
        x2 = x2 * lax.rsqrt(ms + EPS) * gfin_ref[...]
    out_ref[...] = x2


def _combine_call(cmap, nch, x1, route, loff_row, g_final, ys, *, final_norm, n_chunks):
    total, d_model = x1.shape
    return pl.pallas_call(
        functools.partial(_combine_kernel, final_norm=final_norm, n_chunks=n_chunks),
        grid_spec=pltpu.PrefetchScalarGridSpec(
            num_scalar_prefetch=2,
            grid=(total // MOE_TILE,),
            in_specs=[pl.BlockSpec((MOE_TILE, d_model), lambda i, *_: (i, 0)),
                      pl.BlockSpec((MOE_TILE, LANES), lambda i, *_: (i, 0)),
                      pl.BlockSpec((None, 1, LANES), lambda i, *_: (i, 0, 0)),
                      pl.BlockSpec((1, d_model), lambda i, *_: (0, 0)),
                      pl.BlockSpec(memory_space=pl.ANY)],
            out_specs=pl.BlockSpec((MOE_TILE, d_model), lambda i, *_: (i, 0)),
            scratch_shapes=[pltpu.VMEM((n_chunks * SUBLANES, d_model), F32), pltpu.SemaphoreType.DMA],
        ),
        out_shape=jax.ShapeDtypeStruct((total, d_model), F32),
        input_output_aliases={2: 0},
        compiler_params=pltpu.CompilerParams(dimension_semantics=("arbitrary",), vmem_limit_bytes=VMEM_LIMIT),
        name="moe_combine",
    )(cmap, nch, x1, route, loff_row, g_final, ys)


def _moe(x1, route, route_t, g_ffn, w_gate, w_up, w_down, g_final, *, final_norm):
    total, d_model = x1.shape
    n_exp = w_gate.shape[0]
    nt = total // MOE_TILE
    counts = _plan_call(route)[:, 0, :n_exp].astype(jnp.int32)
    run = (counts + SUBLANES - 1) // SUBLANES * SUBLANES
    lend = jnp.cumsum(run, axis=1)
    loff = lend - run
    gend = jnp.cumsum(run, axis=0)
    seg_rows = gend[-1]
    seg_tiles = (seg_rows + EXPERT_TILE - 1) // EXPERT_TILE
    seg_end = jnp.cumsum(seg_tiles)
    seg_off = (seg_end - seg_tiles) * EXPERT_TILE
    goff = seg_off[None, :] + gend - run
    n_chunks = (2 * MOE_TILE + n_exp * (SUBLANES - 1) + SUBLANES - 1) // SUBLANES
    n_chunks = (n_chunks + 15) // 16 * 16
    chunk_row = jnp.arange(n_chunks, dtype=jnp.int32) * SUBLANES
    e_of = jnp.sum((chunk_row[None, :, None] >= lend[:, None, :]).astype(jnp.int32), axis=2)
    e_of = jnp.minimum(e_of, n_exp - 1)
    cmap = (jnp.take_along_axis(goff, e_of, axis=1) + chunk_row[None, :]
            - jnp.take_along_axis(loff, e_of, axis=1)).reshape(-1)
    nch = lend[:, -1] // SUBLANES
    n_tiles = (2 * total + nt * n_exp * (SUBLANES - 1) + n_exp * (EXPERT_TILE - 1)) // EXPERT_TILE + 1
    n_active = seg_end[-1].reshape(1).astype(jnp.int32)
    tile_blk = jnp.minimum(jnp.arange(n_tiles, dtype=jnp.int32), n_active - 1)
    tile_exp = jnp.sum((tile_blk[:, None] >= seg_end[None, :]).astype(jnp.int32), axis=1)
    tile_exp = jnp.minimum(tile_exp, n_exp - 1)
    lofff = loff.astype(F32)
    loff_col = jnp.broadcast_to(lofff[:, :, None], (nt, n_exp, LANES))
    loff_row = jnp.pad(lofff, ((0, 0), (0, LANES - n_exp)))[:, None, :]
    xs = _dispatch_call(cmap, nch, seg_off + seg_rows, seg_tiles * EXPERT_TILE - seg_rows, n_active, x1, route_t,
                        loff_col, g_ffn, n_tiles, n_chunks)
    ys = _experts_call(tile_blk, tile_exp, n_active, xs, w_gate, w_up, w_down)
    return _combine_call(cmap, nch, x1, route, loff_row, g_final, ys, final_norm=final_norm, n_chunks=n_chunks)


def _to_rows(x, bblk):
    b, l, d = x.shape
    return x.reshape(b // bblk, bblk, l, d).transpose(0, 2, 1, 3).reshape(b * l, d)


def _from_rows(rows, b, bblk, l):
    d = rows.shape[-1]
    return rows.reshape(b // bblk, l, bblk, d).transpose(0, 2, 1, 3).reshape(b, l, d)


def kernel(x_prompt, x_sample, state_ssm_re, state_ssm_im, state_pool, meta_tokens, norm_mix_g, w_in, ssm_a_re,
           ssm_a_im, ssm_log_dt, ssm_b_re, ssm_b_im, ssm_c_re, ssm_c_im, ssm_d, w_glu, b_glu, w_pool, pool_scale,
           w_out, norm_ffn_g, w_router_group, b_router_group, w_router_expert, b_router_expert, w_gate, w_up,
           w_down, norm_final_g):
    batch, seq, d_model = x_prompt.shape
    dec_batch, dec_seq, _ = x_sample.shape
    depth, _, groups, nstate = state_ssm_re.shape
    n_meta = meta_tokens.shape[0]
    pool_buf, d_pool = state_pool.shape[2], state_pool.shape[3]
    n_pool_groups = w_pool.shape[1]
    windows = tuple(2 ** (k + 1) for k in range(n_pool_groups))
    assert pool_buf == max(windows) - 1 == POOL_HIST - 1
    n_expert_groups = w_router_group.shape[-1]
    n_experts = w_router_expert.shape[-1]
    cfg = dict(windows=windows, n_expert_groups=n_expert_groups, experts_per_group=n_experts // n_expert_groups)

    step_rows = 512
    p_tile_l, s_bblk, m_batch = step_rows // batch, step_rows // dec_seq, step_rows // n_meta
    n_p, n_s, n_m = batch * seq, dec_batch * dec_seq, m_batch * n_meta
    assert m_batch >= batch and seq % p_tile_l == 0 and dec_batch % s_bblk == 0
    total = n_p + n_s + n_m
    assert total % step_rows == 0 and step_rows % MOE_TILE == 0

    mq, pmat, a8 = _s5_prepare(ssm_a_re, ssm_a_im, ssm_log_dt, ssm_b_re, ssm_b_im, ssm_c_re, ssm_c_im)

    meta_b = jnp.broadcast_to(meta_tokens.astype(F32)[None], (m_batch, n_meta, d_model))
    x = jnp.concatenate([_to_rows(x_prompt, batch), _to_rows(x_sample, s_bblk), _to_rows(meta_b, m_batch)], axis=0)

    row2 = lambda v: v.reshape(1, -1).astype(F32)
    outs = {k: [] for k in ("hp", "bp", "hs", "bs")}
    for l in range(depth):
        w_r = jnp.concatenate([w_router_group[l], w_router_expert[l]], axis=1).astype(F32)
        w_r = jnp.pad(w_r, ((0, 0), (0, LANES - w_r.shape[1])))
        wr_hi = w_r.astype(BF16)
        b_r = jnp.concatenate([b_router_group[l], b_router_expert[l]]).astype(F32)
        lw32 = dict(g_mix=row2(norm_mix_g[l]), w_in=w_in[l].astype(F32), mq=mq[l], pmat=pmat[l], a8=a8[l],
                    ssm_d=row2(ssm_d[l]), w_glu=w_glu[l].astype(F32), b_glu=row2(b_glu[l]),
                    w_pool=w_pool[l].astype(F32), pool_scale=row2(pool_scale[l]), w_out=w_out[l].astype(F32),
                    g_ffn=row2(norm_ffn_g[l]), wr_hi=wr_hi, wr_lo=(w_r - wr_hi.astype(F32)).astype(BF16),
                    b_r=jnp.pad(b_r, (0, LANES - b_r.shape[0])).reshape(1, LANES))
        lw = dict(lw32, **{k: lw32[k].astype(BF16) for k in ("w_in", "mq", "pmat", "w_glu", "w_pool", "w_out")})

        zero_h = jnp.zeros((groups, m_batch, LANES), F32)
        zero_buf = jnp.zeros((POOL_HIST, m_batch, d_pool), F32)
        route = jnp.zeros((total, LANES), F32)
        route_t = jnp.zeros((SUBLANES, total), F32)
        x, route, route_t, h_m, buf_m = _mixer_call(
            x, route, route_t, zero_h, zero_buf, lw, row_block=(n_p + n_s) // step_rows, batch=m_batch,
            bblk=m_batch, length=n_meta, tile_l=n_meta, pos0=0, cfg=cfg)
        n_fast = seq - p_tile_l if l < depth - 1 else seq
        x, route, route_t, h_p, buf_p = _mixer_call(
            x, route, route_t, h_m[:, :batch], buf_m[:, :batch], lw, row_block=0, batch=batch, bblk=batch,
            length=n_fast, tile_l=p_tile_l, pos0=n_meta, cfg=cfg)
        if n_fast < seq:
            x, route, route_t, h_p, buf_p = _mixer_call(
                x, route, route_t, h_p, buf_p, lw32, row_block=n_fast // p_tile_l, batch=batch, bblk=batch,
                length=seq - n_fast, tile_l=p_tile_l, pos0=n_meta + n_fast, cfg=cfg, precise=True)
        h0_s = jnp.concatenate([state_ssm_re[l], state_ssm_im[l]], axis=-1).transpose(1, 0, 2).astype(F32)
        buf0_s = jnp.pad(state_pool[l].astype(F32).transpose(1, 0, 2), ((1, 0), (0, 0), (0, 0)))
        x, route, route_t, h_s, buf_s = _mixer_call(
            x, route, route_t, h0_s, buf0_s, lw, row_block=n_p // step_rows, batch=dec_batch, bblk=s_bblk,
            length=dec_seq, tile_l=dec_seq, pos0=PAST_LEN, cfg=cfg)
        outs["hp"].append(h_p)
        outs["bp"].append(buf_p)
        outs["hs"].append(h_s)
        outs["bs"].append(buf_s)
        x = _moe(x, route, route_t, lw["g_ffn"], w_gate[l], w_up[l], w_down[l], row2(norm_final_g),
                 final_norm=(l == depth - 1))

    y_prompt = _from_rows(x[:n_p], batch, batch, seq)
    y_sample = _from_rows(x[n_p:n_p + n_s], dec_batch, s_bblk, dec_seq)
    st = lambda hs: jnp.stack(hs).transpose(0, 2, 1, 3)
    pl_out = lambda bs: jnp.stack(bs)[:, 1:].transpose(0, 2, 1, 3)
    hp, hs = st(outs["hp"]), st(outs["hs"])
    return (y_prompt, y_sample, hp[..., :nstate], hp[..., nstate:], pl_out(outs["bp"]),
            hs[..., :nstate], hs[..., nstate:], pl_out(outs["bs"]))
```

```python
import functools

import numpy as np
import jax
import jax.numpy as jnp
from jax import lax
from jax.experimental import pallas as pl
from jax.experimental.pallas import tpu as pltpu

F32 = jnp.float32
BF16 = jnp.bfloat16

LANES = 128
SUBLANES = 8
CHUNK = 8
POOL_HIST = 16
MOE_TILE = 256
EXPERT_TILE = 256
VMEM_LIMIT = 56 * 1024 * 1024
EPS = 1e-6
NEG = -1e30
PAST_LEN = 16384


def _dot(a, b):
    return jnp.dot(a, b, preferred_element_type=F32)


def _mm(a, w, precise):
    if precise:
        return jnp.dot(a, w, precision=lax.Precision.HIGHEST, preferred_element_type=F32)
    return _dot(a.astype(BF16), w)


def _pack_halves(x):
    k = x.shape[-1] // 2
    return pltpu.pack_elementwise([x[:, :k], x[:, k:]], packed_dtype=BF16)


def _unpack_halves(p):
    lo = pltpu.unpack_elementwise(p, index=0, packed_dtype=BF16, unpacked_dtype=F32)
    hi = pltpu.unpack_elementwise(p, index=1, packed_dtype=BF16, unpacked_dtype=F32)
    return lo.astype(BF16), hi.astype(BF16)


def _dot_nt_f32(a, b):
    return lax.dot_general(a, b, (((1,), (1,)), ((), ())), precision=lax.Precision.HIGHEST,
                           preferred_element_type=F32)


def _s5_prep_kernel(a_re_ref, a_im_ref, ldt_ref, bt_ref, btx_ref, c_ref, cx_ref, mq_ref, p_ref, a8_ref):
    ch = bt_ref.shape[1]
    first_half = lax.broadcasted_iota(jnp.int32, (1, LANES), 1) < LANES // 2
    sgn = jnp.where(first_half, 1.0, -1.0)
    for s in range(CHUNK):
        a_re = a_re_ref[s]
        a_im = a_im_ref[s]
        dt = jnp.exp(ldt_ref[s])
        mag = jnp.exp(dt * a_re)
        abr = mag * jnp.cos(dt * a_im)
        abi = mag * jnp.sin(dt * a_im)
        den = a_re * a_re + a_im * a_im
        nr = abr - 1.0
        f_re = (nr * a_re + abi * a_im) / den
        f_im = (abi * a_re - nr * a_im) / den
        bb = f_re * bt_ref[s] - sgn * f_im * btx_ref[s]
        cc = c_ref[s] * sgn
        ccx = -cx_ref[s]
        bbx = f_re * btx_ref[s] + sgn * f_im * bt_ref[s]
        pr = jnp.ones_like(a_re)
        pi = jnp.zeros_like(a_re)
        e2, w2 = [], []
        for k in range(CHUNK + 1):
            e2.append(pr * cc + pi * ccx)
            w2.append(pr * bb - sgn * pi * bbx)
            if k < CHUNK:
                pr, pi = pr * abr - pi * abi, pr * abi + pi * abr
        a8_ref[s, 0:1, :] = pr
        a8_ref[s, 1:2, :] = -sgn * pi
        a8_ref[s, 2:3, :] = sgn * pi
        zero = jnp.zeros_like(bb)
        jpos = [(p - s) % CHUNK for p in range(CHUNK)]
        for p in range(CHUNK):
            j = jpos[p]
            taps = jnp.concatenate([e2[jp - j] if jp >= j else zero for jp in jpos], axis=0)
            rows = slice(p * ch, (p + 1) * ch)
            mq_ref[s, rows, 0:LANES] = _dot_nt_f32(bb, taps)
            mq_ref[s, rows, LANES:2 * LANES] = w2[CHUNK - 1 - j]
        pt = jnp.concatenate([e2[jp + 1] for jp in jpos], axis=0)
        p_ref[s] = pt.T


def _s5_prepare(a_re, a_im, log_dt, b_re, b_im, c_re, c_im):
    depth, groups, nstate = a_re.shape
    ch = b_re.shape[-1]
    dg = depth * groups
    assert ch * CHUNK == LANES and 2 * nstate == LANES and groups % CHUNK == 0
    dup = lambda x: jnp.concatenate([x, x], axis=-1).reshape(dg, 1, LANES).astype(F32)
    btr = jnp.swapaxes(b_re.reshape(dg, nstate, ch), 1, 2).astype(F32)
    bti = jnp.swapaxes(b_im.reshape(dg, nstate, ch), 1, 2).astype(F32)
    cr = c_re.reshape(dg, ch, nstate).astype(F32)
    ci = c_im.reshape(dg, ch, nstate).astype(F32)
    pair = lambda x, y: jnp.concatenate([x, y], axis=-1)
    blk = lambda *shape: pl.BlockSpec((CHUNK,) + shape, lambda g: (g,) + (0,) * len(shape))
    mq, pmat, a8 = pl.pallas_call(
        _s5_prep_kernel,
        grid=(dg // CHUNK,),
        in_specs=[blk(1, LANES), blk(1, LANES), blk(1, 1)] + [blk(ch, LANES)] * 4,
        out_specs=[blk(LANES, 2 * LANES), blk(LANES, LANES), blk(3, LANES)],
        out_shape=[jax.ShapeDtypeStruct((dg, LANES, 2 * LANES), F32), jax.ShapeDtypeStruct((dg, LANES, LANES), F32),
                   jax.ShapeDtypeStruct((dg, 3, LANES), F32)],
        name="s5_prep",
    )(dup(a_re), dup(a_im), log_dt.reshape(dg, 1, 1).astype(F32), pair(btr, bti), pair(bti, btr),
      pair(cr, ci), pair(ci, cr))
    shp = lambda x: x.reshape((depth, groups) + x.shape[1:])
    return shp(mq), shp(pmat), shp(a8)


def _mixer_kernel(x_ref, h0_ref, buf0_ref, gmix_ref, win_ref, mq_ref, p_ref, a8_ref, d_ref, wglu_ref,
                  bglu_ref, wpool_ref, pscale_ref, wout_ref, gffn_ref, wr_ref, br_ref,
                  route_in_ref, routet_in_ref, x1_ref, route_ref, routet_ref, h_ref, buf_ref, u_scr, y_scr, cat_scr,
                  *, bblk, tile_l, pos0, windows, n_expert_groups, experts_per_group, precise):
    del route_in_ref, routet_in_ref
    li = pl.program_id(1)
    rows = tile_l * bblk
    nk = tile_l // CHUNK
    n = nk * bblk
    d_model = x_ref.shape[-1]
    d_ssm = y_scr.shape[-1]
    d_pool = d_model - d_ssm
    hist = POOL_HIST * bblk

    @pl.when(li == 0)
    def _init():
        h_ref[...] = h0_ref[...]
        cat_scr[0:hist, :] = buf0_ref[...].reshape(hist, d_pool)

    x = x_ref[...]
    ms = jnp.mean(x * x, axis=-1, keepdims=True)
    h = x * lax.rsqrt(ms + EPS) * gmix_ref[...]
    u = _mm(h, win_ref[...], precise)
    u_scr[...] = u.reshape(nk, CHUNK, bblk, d_model)

    slot = lax.broadcasted_iota(jnp.int32, (n, LANES), 1) // (LANES // CHUNK)
    for v in range(d_ssm // LANES):
        rolled = []
        for j in range(CHUNK):
            uj = u_scr[:, j, :, v * LANES:(v + 1) * LANES].reshape(n, LANES)
            rolled.append(pltpu.roll(uj, (LANES // CHUNK) * j, axis=1) if j else uj)
        ys = []
        for s in range(CHUNK):
            g = v * CHUNK + s
            ug = rolled[0]
            for j in range(1, CHUNK):
                ug = jnp.where(slot == (s + j) % CHUNK, rolled[j], ug)
            yq = _mm(ug, mq_ref[g], precise)
            sg = yq[:, LANES:]
            sg_sw = pltpu.roll(sg, LANES // 2, axis=1)
            a8c = a8_ref[g, 0:1, :]
            a8s = a8_ref[g, 1:2, :]
            a8w = a8_ref[g, 2:3, :]
            hcur = h_ref[g]
            hsw = pltpu.roll(hcur, LANES // 2, axis=1)
            hins = []
            for k in range(nk):
                hins.append(hcur)
                sk = sg[k * bblk:(k + 1) * bblk]
                skw = sg_sw[k * bblk:(k + 1) * bblk]
                hcur, hsw = a8c * hcur + a8s * hsw + sk, a8c * hsw + a8w * hcur + skw
            h_ref[g] = hcur
            hin = jnp.concatenate(hins, axis=0) if nk > 1 else hins[0]
            ys.append(yq[:, :LANES] + _mm(hin, p_ref[g], precise))
        for j in range(CHUNK):
            z = ys[0]
            for s in range(1, CHUNK):
                z = jnp.where(slot == (s + j) % CHUNK, ys[s], z)
            if j:
                z = pltpu.roll(z, LANES - (LANES // CHUNK) * j, axis=1)
            y_scr[:, j, :, v * LANES:(v + 1) * LANES] = z.reshape(nk, bblk, LANES)

    u_ssm = u_scr[:, :, :, 0:d_ssm].reshape(rows, d_ssm)
    y = y_scr[...].reshape(rows, d_ssm) + d_ref[...] * u_ssm
    gl = jax.nn.gelu(y)
    ssm_out = gl * jax.nn.sigmoid(_mm(gl, wglu_ref[...], precise) + bglu_ref[...])

    up = u_scr[:, :, :, d_ssm:].reshape(rows, d_pool)
    cat_scr[hist:hist + rows, :] = up
    row_pos = lax.broadcasted_iota(jnp.int32, (rows, 1), 0) // bblk
    pos = pos0 + li * tile_l + row_pos
    pool_parts = []
    pgrp = d_pool // len(windows)
    for kk, w in enumerate(windows):
        lo = kk * pgrp
        acc = cat_scr[hist:hist + rows, lo:lo + pgrp]
        cur = acc
        for sft in range(1, w):
            acc = acc + cat_scr[hist - sft * bblk:hist - sft * bblk + rows, lo:lo + pgrp]
        inv = 1.0 / jnp.minimum(pos + 1, w).astype(F32)
        diff = acc * inv - cur
        pool_parts.append(_mm(diff, wpool_ref[kk], precise))
    pool_out = jnp.concatenate(pool_parts, axis=-1) * pscale_ref[...]
    tail = cat_scr[rows:rows + hist, :]
    cat_scr[0:hist, :] = tail
    buf_ref[...] = tail.reshape(POOL_HIST, bblk, d_pool)

    if precise:
        mix = jnp.concatenate([ssm_out, pool_out], axis=-1)
    else:
        mix = jnp.concatenate([ssm_out.astype(BF16), pool_out.astype(BF16)], axis=-1)
    x1 = x + _mm(mix, wout_ref[...], precise)
    x1_ref[...] = x1
    ms2 = jnp.mean(x1 * x1, axis=-1, keepdims=True)
    t = x1 * lax.rsqrt(ms2 + EPS) * gffn_ref[...]

    t_hi = t.astype(BF16)
    t_lo = (t - t_hi.astype(F32)).astype(BF16)
    w_r = wr_ref[...]
    w_hi = w_r.astype(BF16)
    w_lo = (w_r - w_hi.astype(F32)).astype(BF16)
    logits = _dot(t_hi, w_hi) + _dot(t_lo, w_hi) + _dot(t_hi, w_lo) + br_ref[...]
    lane = lax.broadcasted_iota(jnp.int32, (rows, LANES), 1)
    big = jnp.int32(4 * LANES)
    lg = jnp.where(lane < n_expert_groups, logits, NEG)
    mg = jnp.max(lg, axis=-1, keepdims=True)
    gsel = jnp.min(jnp.where(lg == mg, lane, big), axis=-1, keepdims=True)
    p_sel = 1.0 / jnp.sum(jnp.exp(lg - mg), axis=-1, keepdims=True)
    e_lo = n_expert_groups + gsel * experts_per_group
    le = jnp.where((lane >= e_lo) & (lane < e_lo + experts_per_group), logits, NEG)
    v1 = jnp.max(le, axis=-1, keepdims=True)
    i1 = jnp.min(jnp.where(le == v1, lane, big), axis=-1, keepdims=True)
    le2 = jnp.where(lane == i1, NEG, le)
    v2 = jnp.max(le2, axis=-1, keepdims=True)
    i2 = jnp.min(jnp.where(le2 == v2, lane, big), axis=-1, keepdims=True)
    ex = jnp.exp(v2 - v1)
    q1 = p_sel / (1.0 + ex)
    q2 = q1 * ex
    e1 = (i1 - n_expert_groups).astype(F32)
    e2 = (i2 - n_expert_groups).astype(F32)
    route = jnp.where(lane == 0, e1, jnp.where(lane == 1, e2, jnp.where(lane == 2, q1,
                      jnp.where(lane == 3, q2, 0.0))))
    route_ref[...] = route
    routet_ref[...] = route.T[0:SUBLANES, :]


def _mixer_call(x_flat, route_flat, routet_flat, h0, buf0, lw, *, row_block, batch, bblk, length, tile_l, pos0,
                cfg, precise=False):
    total, d_model = x_flat.shape
    groups = h0.shape[0]
    d_pool = buf0.shape[-1]
    d_ssm = d_model - d_pool
    nb = batch // bblk
    nl = length // tile_l
    rows = tile_l * bblk

    def full(a):
        nd = a.ndim
        return pl.BlockSpec(a.shape, lambda b, l, _n=nd: (0,) * _n)

    tok = lambda width: pl.BlockSpec((rows, width), lambda b, l: (row_block + b * nl + l, 0))
    weights = [lw["g_mix"], lw["w_in"], lw["mq"], lw["pmat"], lw["a8"], lw["ssm_d"], lw["w_glu"], lw["b_glu"],
               lw["w_pool"], lw["pool_scale"], lw["w_out"], lw["g_ffn"], lw["w_r"], lw["b_r"]]
    in_specs = ([tok(d_model),
                 pl.BlockSpec((groups, bblk, LANES), lambda b, l: (0, b, 0)),
                 pl.BlockSpec((POOL_HIST, bblk, d_pool), lambda b, l: (0, b, 0))]
                + [full(w) for w in weights]
                + [pl.BlockSpec(memory_space=pl.ANY)] * 2)
    out_specs = [tok(d_model), tok(LANES),
                 pl.BlockSpec((SUBLANES, rows), lambda b, l: (0, row_block + b * nl + l)),
                 pl.BlockSpec((groups, bblk, LANES), lambda b, l: (0, b, 0)),
                 pl.BlockSpec((POOL_HIST, bblk, d_pool), lambda b, l: (0, b, 0))]
    out_shape = [jax.ShapeDtypeStruct((total, d_model), F32), jax.ShapeDtypeStruct((total, LANES), F32),
                 jax.ShapeDtypeStruct((SUBLANES, total), F32),
                 jax.ShapeDtypeStruct((groups, batch, LANES), F32),
                 jax.ShapeDtypeStruct((POOL_HIST, batch, d_pool), F32)]
    n_in = 3 + len(weights)
    kern = functools.partial(_mixer_kernel, bblk=bblk, tile_l=tile_l, pos0=pos0, windows=cfg["windows"],
                             n_expert_groups=cfg["n_expert_groups"], experts_per_group=cfg["experts_per_group"],
                             precise=precise)
    args = [x_flat, h0, buf0] + weights + [route_flat, routet_flat]
    return pl.pallas_call(
        kern,
        grid=(nb, nl),
        in_specs=in_specs,
        out_specs=out_specs,
        out_shape=out_shape,
        scratch_shapes=[pltpu.VMEM((tile_l // CHUNK, CHUNK, bblk, d_model), F32),
                        pltpu.VMEM((tile_l // CHUNK, CHUNK, bblk, d_ssm), F32),
                        pltpu.VMEM(((POOL_HIST + tile_l) * bblk, d_pool), F32)],
        input_output_aliases={0: 0, n_in: 1, n_in + 1: 2},
        compiler_params=pltpu.CompilerParams(dimension_semantics=("arbitrary", "arbitrary"),
                                             vmem_limit_bytes=VMEM_LIMIT),
        name="mixer",
    )(*args)


def _plan_kernel(route_ref, cnt_ref):
    r = route_ref[...]
    lanef = lax.broadcasted_iota(jnp.int32, r.shape, 1).astype(F32)
    oh = jnp.logical_or(lanef == r[:, 0:1], lanef == r[:, 1:2]).astype(F32)
    cnt_ref[...] = jnp.sum(oh, axis=0, keepdims=True)


def _plan_call(route):
    total = route.shape[0]
    nt = total // MOE_TILE
    return pl.pallas_call(
        _plan_kernel,
        grid=(nt,),
        in_specs=[pl.BlockSpec((MOE_TILE, LANES), lambda i: (i, 0))],
        out_specs=pl.BlockSpec((None, 1, LANES), lambda i: (i, 0, 0)),
        out_shape=jax.ShapeDtypeStruct((nt, 1, LANES), F32),
        compiler_params=pltpu.CompilerParams(dimension_semantics=("arbitrary",)),
        name="moe_plan",
    )(route)


def _chunk_copy(src_ref, src_row, dst_ref, dst_row, sem):
    src = src_ref.at[pl.ds(pl.multiple_of(src_row, SUBLANES), SUBLANES), :]
    dst = dst_ref.at[pl.ds(pl.multiple_of(dst_row, SUBLANES), SUBLANES), :]
    return pltpu.make_async_copy(src, dst, sem)


def _dispatch_kernel(cmap_ref, nch_ref, pstart_ref, plen_ref, nact_ref, x_ref, rt_ref, loff_ref, gffn_ref, xs_ref,
                     xloc, zero_scr, sem, zsem, *, n_exp, n_tiles, n_chunks):
    step = pl.program_id(0)

    def pad_copies(e, op):
        start = pstart_ref[e]
        length = plen_ref[e]
        for b in range(SUBLANES.bit_length() - 1, EXPERT_TILE.bit_length() - 1):
            size = 1 << b

            @pl.when(((length >> b) & 1) == 1)
            def _():
                off = pl.multiple_of(start + (length & (size - 1)), SUBLANES)
                op(pltpu.make_async_copy(zero_scr.at[pl.ds(0, size), :], xs_ref.at[pl.ds(off, size), :], zsem))

    def tail_copy(tile):
        row0 = pl.multiple_of(tile * EXPERT_TILE, EXPERT_TILE)
        return pltpu.make_async_copy(zero_scr, xs_ref.at[pl.ds(row0, EXPERT_TILE), :], zsem)

    def for_all_fill(op):
        def per_expert(e, c):
            pad_copies(e, op)
            return c

        def per_tile(tile, c):
            op(tail_copy(tile))
            return c

        lax.fori_loop(0, n_exp, per_expert, 0)
        lax.fori_loop(nact_ref[0], n_tiles, per_tile, 0)

    @pl.when(step == 0)
    def _fill():
        zero_scr[...] = jnp.zeros_like(zero_scr)
        for_all_fill(lambda cp: cp.start())
        for_all_fill(lambda cp: cp.wait())

    x = x_ref[...]
    tt = x.shape[0]
    ms = jnp.mean(x * x, axis=-1, keepdims=True)
    t = (x * lax.rsqrt(ms + EPS) * gffn_ref[...]).astype(BF16)

    e1 = rt_ref[0:1, :]
    e2 = rt_ref[1:2, :]
    eio = lax.broadcasted_iota(jnp.int32, (n_exp, tt), 0).astype(F32)
    oh1 = eio == e1
    oh2 = eio == e2
    before = lax.broadcasted_iota(jnp.int32, (tt, tt), 0) < lax.broadcasted_iota(jnp.int32, (tt, tt), 1)
    rank = _dot(jnp.logical_or(oh1, oh2).astype(BF16), before.astype(BF16))
    where_to = loff_ref[:, 0:1] + rank
    lpos1 = jnp.sum(jnp.where(oh1, where_to, 0.0), axis=0, keepdims=True).astype(jnp.int32)
    lpos2 = jnp.sum(jnp.where(oh2, where_to, 0.0), axis=0, keepdims=True).astype(jnp.int32)
    rio = lax.broadcasted_iota(jnp.int32, (xloc.shape[0], tt), 0)
    perm = jnp.logical_or(rio == lpos1, rio == lpos2).astype(BF16)
    xloc[...] = _pack_halves(_dot(perm, t))

    base = step * n_chunks

    def chunk(c):
        return _chunk_copy(xloc, c * SUBLANES, xs_ref, cmap_ref[base + c], sem)

    def issue(c, carry):
        chunk(c).start()
        return carry

    def drain(c, carry):
        chunk(c).wait()
        return carry

    lax.fori_loop(0, nch_ref[step], issue, 0)
    lax.fori_loop(0, nch_ref[step], drain, 0)


def _dispatch_call(cmap, nch, pad_start, pad_len, n_active, x_flat, route_t, loff_col, g_ffn, n_tiles, n_chunks):
    total, d_model = x_flat.shape
    n_exp = pad_start.shape[0]
    return pl.pallas_call(
        functools.partial(_dispatch_kernel, n_exp=n_exp, n_tiles=n_tiles, n_chunks=n_chunks),
        grid_spec=pltpu.PrefetchScalarGridSpec(
            num_scalar_prefetch=5,
            grid=(total // MOE_TILE,),
            in_specs=[pl.BlockSpec((MOE_TILE, d_model), lambda i, *_: (i, 0)),
                      pl.BlockSpec((SUBLANES, MOE_TILE), lambda i, *_: (0, i)),
                      pl.BlockSpec((None, n_exp, LANES), lambda i, *_: (i, 0, 0)),
                      pl.BlockSpec((1, d_model), lambda i, *_: (0, 0))],
            out_specs=pl.BlockSpec(memory_space=pl.ANY),
            scratch_shapes=[pltpu.VMEM((n_chunks * SUBLANES, d_model // 2), jnp.uint32),
                            pltpu.VMEM((EXPERT_TILE, d_model // 2), jnp.uint32),
                            pltpu.SemaphoreType.DMA, pltpu.SemaphoreType.DMA],
        ),
        out_shape=jax.ShapeDtypeStruct((n_tiles * EXPERT_TILE, d_model // 2), jnp.uint32),
        compiler_params=pltpu.CompilerParams(dimension_semantics=("arbitrary",), vmem_limit_bytes=VMEM_LIMIT),
        name="moe_dispatch",
    )(cmap, nch, pad_start, pad_len, n_active, x_flat, route_t, loff_col, g_ffn)


def _experts_kernel(blk_ref, exp_ref, nact_ref, xs_ref, wg_ref, wu_ref, wd_ref, ys_ref, wgu_scr, wd_scr):
    i = pl.program_id(0)
    f = wg_ref.shape[-1]
    active = i < nact_ref[0]
    changed = jnp.logical_or(i == 0, exp_ref[i] != exp_ref[jnp.maximum(i - 1, 0)])

    @pl.when(jnp.logical_and(active, changed))
    def _():
        wgu_scr[:, 0:f] = wg_ref[...].astype(BF16)
        wgu_scr[:, f:2 * f] = wu_ref[...].astype(BF16)
        wd_scr[...] = wd_ref[...].astype(BF16)

    @pl.when(active)
    def _():
        half = wgu_scr.shape[0] // 2
        t_lo, t_hi = _unpack_halves(xs_ref[...])
        gu = _dot(t_lo, wgu_scr[0:half, :]) + _dot(t_hi, wgu_scr[half:2 * half, :])
        act = jax.nn.silu(gu[:, 0:f]) * gu[:, f:2 * f]
        ys_ref[...] = _pack_halves(_dot(act.astype(BF16), wd_scr[...]))

    @pl.when(jnp.logical_not(active))
    def _():
        ys_ref[...] = jnp.zeros_like(ys_ref)


def _experts_call(tile_blk, tile_exp, n_active, xs, w_gate, w_up, w_down):
    nrows = xs.shape[0]
    n_exp, d_model, f = w_gate.shape
    n_tiles = nrows // EXPERT_TILE
    return pl.pallas_call(
        _experts_kernel,
        grid_spec=pltpu.PrefetchScalarGridSpec(
            num_scalar_prefetch=3,
            grid=(n_tiles,),
            in_specs=[pl.BlockSpec((EXPERT_TILE, d_model // 2), lambda i, blk, ex, na: (blk[i], 0)),
                      pl.BlockSpec((None, d_model, f), lambda i, blk, ex, na: (ex[i], 0, 0)),
                      pl.BlockSpec((None, d_model, f), lambda i, blk, ex, na: (ex[i], 0, 0)),
                      pl.BlockSpec((None, f, d_model), lambda i, blk, ex, na: (ex[i], 0, 0))],
            out_specs=pl.BlockSpec((EXPERT_TILE, d_model // 2), lambda i, blk, ex, na: (i, 0)),
            scratch_shapes=[pltpu.VMEM((d_model, 2 * f), BF16), pltpu.VMEM((f, d_model), BF16)],
        ),
        out_shape=jax.ShapeDtypeStruct((nrows, d_model // 2), jnp.uint32),
        compiler_params=pltpu.CompilerParams(dimension_semantics=("arbitrary",), vmem_limit_bytes=VMEM_LIMIT),
        name="moe_experts",
    )(tile_blk, tile_exp, n_active, xs, w_gate, w_up, w_down)


def _combine_kernel(cmap_ref, nch_ref, x1_ref, route_ref, loff_ref, gfin_ref, ys_ref, out_ref, yloc, sem,
                    *, final_norm, n_chunks):
    step = pl.program_id(0)
    base = step * n_chunks

    @pl.when(step == 0)
    def _():
        yloc[...] = jnp.zeros_like(yloc)

    def chunk(c):
        return _chunk_copy(ys_ref, cmap_ref[base + c], yloc, c * SUBLANES, sem)

    def issue(c, carry):
        chunk(c).start()
        return carry

    def drain(c, carry):
        chunk(c).wait()
        return carry

    lax.fori_loop(0, nch_ref[step], issue, 0)

    route = route_ref[...]
    tt = route.shape[0]
    lanef = lax.broadcasted_iota(jnp.int32, route.shape, 1).astype(F32)
    oh1 = lanef == route[:, 0:1]
    oh2 = lanef == route[:, 1:2]
    earlier = lax.broadcasted_iota(jnp.int32, (tt, tt), 0) > lax.broadcasted_iota(jnp.int32, (tt, tt), 1)
    rank = _dot(earlier.astype(BF16), jnp.logical_or(oh1, oh2).astype(BF16))
    where_to = loff_ref[...] + rank
    lpos1 = jnp.sum(jnp.where(oh1, where_to, 0.0), axis=-1, keepdims=True).astype(jnp.int32)
    lpos2 = jnp.sum(jnp.where(oh2, where_to, 0.0), axis=-1, keepdims=True).astype(jnp.int32)
    lio = lax.broadcasted_iota(jnp.int32, (tt, yloc.shape[0]), 1)
    wperm = jnp.where(lio == lpos1, route[:, 2:3], jnp.where(lio == lpos2, route[:, 3:4], 0.0)).astype(BF16)

    lax.fori_loop(0, nch_ref[step], drain, 0)
    y_lo, y_hi = _unpack_halves(yloc[...])
    x2 = x1_ref[...] + jnp.concatenate([_dot(wperm, y_lo), _dot(wperm, y_hi)], axis=-1)
    if final_norm:
        ms = jnp.mean(x2 * x2, axis=-1, keepdims=True)
        x2 = x2 * lax.rsqrt(ms + EPS) * gfin_ref[...]
    out_ref[...] = x2


def _combine_call(cmap, nch, x1, route, loff_row, g_final, ys, *, final_norm, n_chunks):
    total, d_model = x1.shape
    return pl.pallas_call(
        functools.partial(_combine_kernel, final_norm=final_norm, n_chunks=n_chunks),
        grid_spec=pltpu.PrefetchScalarGridSpec(
            num_scalar_prefetch=2,
            grid=(total // MOE_TILE,),
            in_specs=[pl.BlockSpec((MOE_TILE, d_model), lambda i, *_: (i, 0)),
                      pl.BlockSpec((MOE_TILE, LANES), lambda i, *_: (i, 0)),
                      pl.BlockSpec((None, 1, LANES), lambda i, *_: (i, 0, 0)),
                      pl.BlockSpec((1, d_model), lambda i, *_: (0, 0)),
                      pl.BlockSpec(memory_space=pl.ANY)],
            out_specs=pl.BlockSpec((MOE_TILE, d_model), lambda i, *_: (i, 0)),
            scratch_shapes=[pltpu.VMEM((n_chunks * SUBLANES, d_model // 2), jnp.uint32), pltpu.SemaphoreType.DMA],
        ),
        out_shape=jax.ShapeDtypeStruct((total, d_model), F32),
        input_output_aliases={2: 0},
        compiler_params=pltpu.CompilerParams(dimension_semantics=("arbitrary",), vmem_limit_bytes=VMEM_LIMIT),
        name="moe_combine",
    )(cmap, nch, x1, route, loff_row, g_final, ys)


def _moe(x1, route, route_t, g_ffn, w_gate, w_up, w_down, g_final, *, final_norm):
    total, d_model = x1.shape
    n_exp = w_gate.shape[0]
    nt = total // MOE_TILE
    counts = _plan_call(route)[:, 0, :n_exp].astype(jnp.int32)
    run = (counts + SUBLANES - 1) // SUBLANES * SUBLANES
    lend = jnp.cumsum(run, axis=1)
    loff = lend - run
    gend = jnp.cumsum(run, axis=0)
    seg_rows = gend[-1]
    seg_tiles = (seg_rows + EXPERT_TILE - 1) // EXPERT_TILE
    seg_end = jnp.cumsum(seg_tiles)
    seg_off = (seg_end - seg_tiles) * EXPERT_TILE
    goff = seg_off[None, :] + gend - run
    n_chunks = (2 * MOE_TILE + n_exp * (SUBLANES - 1) + SUBLANES - 1) // SUBLANES
    n_chunks = (n_chunks + 15) // 16 * 16
    chunk_row = jnp.arange(n_chunks, dtype=jnp.int32) * SUBLANES
    e_of = jnp.sum((chunk_row[None, :, None] >= lend[:, None, :]).astype(jnp.int32), axis=2)
    e_of = jnp.minimum(e_of, n_exp - 1)
    cmap = (jnp.take_along_axis(goff, e_of, axis=1) + chunk_row[None, :]
            - jnp.take_along_axis(loff, e_of, axis=1)).reshape(-1)
    nch = lend[:, -1] // SUBLANES
    n_tiles = (2 * total + nt * n_exp * (SUBLANES - 1) + n_exp * (EXPERT_TILE - 1)) // EXPERT_TILE + 1
    n_active = seg_end[-1].reshape(1).astype(jnp.int32)
    tile_blk = jnp.minimum(jnp.arange(n_tiles, dtype=jnp.int32), n_active - 1)
    tile_exp = jnp.sum((tile_blk[:, None] >= seg_end[None, :]).astype(jnp.int32), axis=1)
    tile_exp = jnp.minimum(tile_exp, n_exp - 1)
    lofff = loff.astype(F32)
    loff_col = jnp.broadcast_to(lofff[:, :, None], (nt, n_exp, LANES))
    loff_row = jnp.pad(lofff, ((0, 0), (0, LANES - n_exp)))[:, None, :]
    xs = _dispatch_call(cmap, nch, seg_off + seg_rows, seg_tiles * EXPERT_TILE - seg_rows, n_active, x1, route_t,
                        loff_col, g_ffn, n_tiles, n_chunks)
    ys = _experts_call(tile_blk, tile_exp, n_active, xs, w_gate, w_up, w_down)
    return _combine_call(cmap, nch, x1, route, loff_row, g_final, ys, final_norm=final_norm, n_chunks=n_chunks)


def _to_rows_kernel(xp_ref, xs_ref, meta_ref, o_ref, *, n_p_steps, n_s_steps):
    i = pl.program_id(0)
    pb, pl_ = xp_ref.shape[0], xp_ref.shape[1]
    sb, sl = xs_ref.shape[0], xs_ref.shape[1]
    ml = meta_ref.shape[0]
    mb = o_ref.shape[0] // ml

    @pl.when(i < n_p_steps)
    def _():
        for l in range(pl_):
            o_ref[l * pb:(l + 1) * pb, :] = xp_ref[:, l, :]

    @pl.when(jnp.logical_and(i >= n_p_steps, i < n_p_steps + n_s_steps))
    def _():
        for l in range(sl):
            o_ref[l * sb:(l + 1) * sb, :] = xs_ref[:, l, :]

    @pl.when(i >= n_p_steps + n_s_steps)
    def _():
        for l in range(ml):
            o_ref[l * mb:(l + 1) * mb, :] = jnp.broadcast_to(meta_ref[l:l + 1, :], (mb, o_ref.shape[1]))


def _to_rows_call(x_prompt, x_sample, meta, *, step_rows):
    batch, seq, d_model = x_prompt.shape
    dec_batch, dec_seq, _ = x_sample.shape
    p_tile_l, s_bblk = step_rows // batch, step_rows // dec_seq
    n_p_steps, n_s_steps = seq // p_tile_l, dec_batch // s_bblk
    steps = n_p_steps + n_s_steps + 1
    return pl.pallas_call(
        functools.partial(_to_rows_kernel, n_p_steps=n_p_steps, n_s_steps=n_s_steps),
        grid=(steps,),
        in_specs=[pl.BlockSpec((batch, p_tile_l, d_model), lambda i: (0, jnp.minimum(i, n_p_steps - 1), 0)),
                  pl.BlockSpec((s_bblk, dec_seq, d_model),
                               lambda i: (jnp.clip(i - n_p_steps, 0, n_s_steps - 1), 0, 0)),
                  pl.BlockSpec(meta.shape, lambda i: (0, 0))],
        out_specs=pl.BlockSpec((step_rows, d_model), lambda i: (i, 0)),
        out_shape=jax.ShapeDtypeStruct((steps * step_rows, d_model), F32),
        compiler_params=pltpu.CompilerParams(dimension_semantics=("arbitrary",)),
        name="to_rows",
    )(x_prompt, x_sample, meta)


def _from_rows_kernel(y_ref, yp_ref, ys_ref, *, n_p_steps):
    i = pl.program_id(0)
    pb, pl_ = yp_ref.shape[0], yp_ref.shape[1]
    sb, sl = ys_ref.shape[0], ys_ref.shape[1]

    @pl.when(i < n_p_steps)
    def _():
        for l in range(pl_):
            yp_ref[:, l, :] = y_ref[l * pb:(l + 1) * pb, :]

    @pl.when(i >= n_p_steps)
    def _():
        for l in range(sl):
            ys_ref[:, l, :] = y_ref[l * sb:(l + 1) * sb, :]


def _from_rows_call(y_flat, batch, seq, dec_batch, dec_seq, *, step_rows):
    d_model = y_flat.shape[-1]
    p_tile_l, s_bblk = step_rows // batch, step_rows // dec_seq
    n_p_steps, n_s_steps = seq // p_tile_l, dec_batch // s_bblk
    return pl.pallas_call(
        functools.partial(_from_rows_kernel, n_p_steps=n_p_steps),
        grid=(n_p_steps + n_s_steps,),
        in_specs=[pl.BlockSpec((step_rows, d_model), lambda i: (i, 0))],
        out_specs=[pl.BlockSpec((batch, p_tile_l, d_model), lambda i: (0, jnp.minimum(i, n_p_steps - 1), 0)),
                   pl.BlockSpec((s_bblk, dec_seq, d_model),
                                lambda i: (jnp.clip(i - n_p_steps, 0, n_s_steps - 1), 0, 0))],
        out_shape=[jax.ShapeDtypeStruct((batch, seq, d_model), F32),
                   jax.ShapeDtypeStruct((dec_batch, dec_seq, d_model), F32)],
        compiler_params=pltpu.CompilerParams(dimension_semantics=("arbitrary",)),
        name="from_rows",
    )(y_flat)


def kernel(x_prompt, x_sample, state_ssm_re, state_ssm_im, state_pool, meta_tokens, norm_mix_g, w_in, ssm_a_re,
           ssm_a_im, ssm_log_dt, ssm_b_re, ssm_b_im, ssm_c_re, ssm_c_im, ssm_d, w_glu, b_glu, w_pool, pool_scale,
           w_out, norm_ffn_g, w_router_group, b_router_group, w_router_expert, b_router_expert, w_gate, w_up,
           w_down, norm_final_g):
    batch, seq, d_model = x_prompt.shape
    dec_batch, dec_seq, _ = x_sample.shape
    depth, _, groups, nstate = state_ssm_re.shape
    n_meta = meta_tokens.shape[0]
    pool_buf, d_pool = state_pool.shape[2], state_pool.shape[3]
    n_pool_groups = w_pool.shape[1]
    windows = tuple(2 ** (k + 1) for k in range(n_pool_groups))
    assert pool_buf == max(windows) - 1 == POOL_HIST - 1
    n_expert_groups = w_router_group.shape[-1]
    n_experts = w_router_expert.shape[-1]
    cfg = dict(windows=windows, n_expert_groups=n_expert_groups, experts_per_group=n_experts // n_expert_groups)

    step_rows = 512
    p_tile_l, s_bblk, m_batch = step_rows // batch, step_rows // dec_seq, step_rows // n_meta
    n_p, n_s, n_m = batch * seq, dec_batch * dec_seq, m_batch * n_meta
    assert m_batch >= batch and seq % p_tile_l == 0 and dec_batch % s_bblk == 0
    total = n_p + n_s + n_m
    assert total % step_rows == 0 and step_rows % MOE_TILE == 0

    mq, pmat, a8 = _s5_prepare(ssm_a_re, ssm_a_im, ssm_log_dt, ssm_b_re, ssm_b_im, ssm_c_re, ssm_c_im)

    x = _to_rows_call(x_prompt.astype(F32), x_sample.astype(F32), meta_tokens.astype(F32), step_rows=step_rows)
    assert x.shape[0] == total

    row2 = lambda v: v.reshape(1, -1).astype(F32)
    outs = {k: [] for k in ("hp", "bp", "hs", "bs")}
    for l in range(depth):
        w_r = jnp.concatenate([w_router_group[l], w_router_expert[l]], axis=1).astype(F32)
        w_r = jnp.pad(w_r, ((0, 0), (0, LANES - w_r.shape[1])))
        b_r = jnp.concatenate([b_router_group[l], b_router_expert[l]]).astype(F32)
        lw32 = dict(g_mix=row2(norm_mix_g[l]), w_in=w_in[l].astype(F32), mq=mq[l], pmat=pmat[l], a8=a8[l],
                    ssm_d=row2(ssm_d[l]), w_glu=w_glu[l].astype(F32), b_glu=row2(b_glu[l]),
                    w_pool=w_pool[l].astype(F32), pool_scale=row2(pool_scale[l]), w_out=w_out[l].astype(F32),
                    g_ffn=row2(norm_ffn_g[l]), w_r=w_r,
                    b_r=jnp.pad(b_r, (0, LANES - b_r.shape[0])).reshape(1, LANES))
        lw = dict(lw32, **{k: lw32[k].astype(BF16) for k in ("w_in", "mq", "pmat", "w_glu", "w_pool", "w_out")})

        zero_h = jnp.zeros((groups, m_batch, LANES), F32)
        zero_buf = jnp.zeros((POOL_HIST, m_batch, d_pool), F32)
        route = jnp.zeros((total, LANES), F32)
        route_t = jnp.zeros((SUBLANES, total), F32)
        x, route, route_t, h_m, buf_m = _mixer_call(
            x, route, route_t, zero_h, zero_buf, lw, row_block=(n_p + n_s) // step_rows, batch=m_batch,
            bblk=m_batch, length=n_meta, tile_l=n_meta, pos0=0, cfg=cfg)
        n_fast = seq - p_tile_l if l < depth - 1 else seq
        x, route, route_t, h_p, buf_p = _mixer_call(
            x, route, route_t, h_m[:, :batch], buf_m[:, :batch], lw, row_block=0, batch=batch, bblk=batch,
            length=n_fast, tile_l=p_tile_l, pos0=n_meta, cfg=cfg)
        if n_fast < seq:
            x, route, route_t, h_p, buf_p = _mixer_call(
                x, route, route_t, h_p, buf_p, lw32, row_block=n_fast // p_tile_l, batch=batch, bblk=batch,
                length=seq - n_fast, tile_l=p_tile_l, pos0=n_meta + n_fast, cfg=cfg, precise=True)
        h0_s = jnp.concatenate([state_ssm_re[l], state_ssm_im[l]], axis=-1).transpose(1, 0, 2).astype(F32)
        buf0_s = jnp.pad(state_pool[l].astype(F32).transpose(1, 0, 2), ((1, 0), (0, 0), (0, 0)))
        x, route, route_t, h_s, buf_s = _mixer_call(
            x, route, route_t, h0_s, buf0_s, lw, row_block=n_p // step_rows, batch=dec_batch, bblk=s_bblk,
            length=dec_seq, tile_l=dec_seq, pos0=PAST_LEN, cfg=cfg)
        outs["hp"].append(h_p)
        outs["bp"].append(buf_p)
        outs["hs"].append(h_s)
        outs["bs"].append(buf_s)
        x = _moe(x, route, route_t, lw["g_ffn"], w_gate[l], w_up[l], w_down[l], row2(norm_final_g),
                 final_norm=(l == depth - 1))

    y_prompt, y_sample = _from_rows_call(x, batch, seq, dec_batch, dec_seq, step_rows=step_rows)
    st = lambda hs: jnp.stack(hs).transpose(0, 2, 1, 3)
    pl_out = lambda bs: jnp.stack(bs)[:, 1:].transpose(0, 2, 1, 3)
    hp, hs = st(outs["hp"]), st(outs["hs"])
    return (y_prompt, y_sample, hp[..., :nstate], hp[..., nstate:], pl_out(outs["bp"]),
            hs[..., :nstate], hs[..., nstate:], pl_out(outs["bs"]))
```

```python
import functools

import numpy as np
import jax
import jax.numpy as jnp
from jax import lax
from jax.experimental import pallas as pl
from jax.experimental.pallas import tpu as pltpu

F32 = jnp.float32
BF16 = jnp.bfloat16

LANES = 128
SUBLANES = 8
CHUNK = 8
POOL_HIST = 16
MOE_TILE = 256
EXPERT_TILE = 256
VMEM_LIMIT = 56 * 1024 * 1024
EPS = 1e-6
NEG = -1e30
PAST_LEN = 16384


def _dot(a, b):
    return jnp.dot(a, b, preferred_element_type=F32)


def _mm(a, w, precise):
    if precise:
        return jnp.dot(a, w, precision=lax.Precision.HIGHEST, preferred_element_type=F32)
    return _dot(a.astype(BF16), w)


def _pack_halves(x):
    k = x.shape[-1] // 2
    return pltpu.pack_elementwise([x[:, :k], x[:, k:]], packed_dtype=BF16)


def _unpack_halves(p):
    lo = pltpu.unpack_elementwise(p, index=0, packed_dtype=BF16, unpacked_dtype=F32)
    hi = pltpu.unpack_elementwise(p, index=1, packed_dtype=BF16, unpacked_dtype=F32)
    return lo.astype(BF16), hi.astype(BF16)


def _dot_nt_f32(a, b):
    return lax.dot_general(a, b, (((1,), (1,)), ((), ())), precision=lax.Precision.HIGHEST,
                           preferred_element_type=F32)


def _s5_prep_kernel(a_re_ref, a_im_ref, ldt_ref, bt_ref, btx_ref, c_ref, cx_ref, mq_ref, p_ref, a8_ref):
    ch = bt_ref.shape[1]
    first_half = lax.broadcasted_iota(jnp.int32, (1, LANES), 1) < LANES // 2
    sgn = jnp.where(first_half, 1.0, -1.0)
    for s in range(CHUNK):
        a_re = a_re_ref[s]
        a_im = a_im_ref[s]
        dt = jnp.exp(ldt_ref[s])
        mag = jnp.exp(dt * a_re)
        abr = mag * jnp.cos(dt * a_im)
        abi = mag * jnp.sin(dt * a_im)
        den = a_re * a_re + a_im * a_im
        nr = abr - 1.0
        f_re = (nr * a_re + abi * a_im) / den
        f_im = (abi * a_re - nr * a_im) / den
        bb = f_re * bt_ref[s] - sgn * f_im * btx_ref[s]
        cc = c_ref[s] * sgn
        ccx = -cx_ref[s]
        bbx = f_re * btx_ref[s] + sgn * f_im * bt_ref[s]
        pr = jnp.ones_like(a_re)
        pi = jnp.zeros_like(a_re)
        e2, w2 = [], []
        for k in range(CHUNK + 1):
            e2.append(pr * cc + pi * ccx)
            w2.append(pr * bb - sgn * pi * bbx)
            if k < CHUNK:
                pr, pi = pr * abr - pi * abi, pr * abi + pi * abr
        a8_ref[s, 0:1, :] = pr
        a8_ref[s, 1:2, :] = -sgn * pi
        a8_ref[s, 2:3, :] = sgn * pi
        zero = jnp.zeros_like(bb)
        jpos = [(p - s) % CHUNK for p in range(CHUNK)]
        for p in range(CHUNK):
            j = jpos[p]
            taps = jnp.concatenate([e2[jp - j] if jp >= j else zero for jp in jpos], axis=0)
            rows = slice(p * ch, (p + 1) * ch)
            mq_ref[s, rows, 0:LANES] = _dot_nt_f32(bb, taps)
            mq_ref[s, rows, LANES:2 * LANES] = w2[CHUNK - 1 - j]
        pt = jnp.concatenate([e2[jp + 1] for jp in jpos], axis=0)
        p_ref[s] = pt.T


def _s5_prepare(a_re, a_im, log_dt, b_re, b_im, c_re, c_im):
    depth, groups, nstate = a_re.shape
    ch = b_re.shape[-1]
    dg = depth * groups
    assert ch * CHUNK == LANES and 2 * nstate == LANES and groups % CHUNK == 0
    dup = lambda x: jnp.concatenate([x, x], axis=-1).reshape(dg, 1, LANES).astype(F32)
    btr = jnp.swapaxes(b_re.reshape(dg, nstate, ch), 1, 2).astype(F32)
    bti = jnp.swapaxes(b_im.reshape(dg, nstate, ch), 1, 2).astype(F32)
    cr = c_re.reshape(dg, ch, nstate).astype(F32)
    ci = c_im.reshape(dg, ch, nstate).astype(F32)
    pair = lambda x, y: jnp.concatenate([x, y], axis=-1)
    blk = lambda *shape: pl.BlockSpec((CHUNK,) + shape, lambda g: (g,) + (0,) * len(shape))
    mq, pmat, a8 = pl.pallas_call(
        _s5_prep_kernel,
        grid=(dg // CHUNK,),
        in_specs=[blk(1, LANES), blk(1, LANES), blk(1, 1)] + [blk(ch, LANES)] * 4,
        out_specs=[blk(LANES, 2 * LANES), blk(LANES, LANES), blk(3, LANES)],
        out_shape=[jax.ShapeDtypeStruct((dg, LANES, 2 * LANES), F32), jax.ShapeDtypeStruct((dg, LANES, LANES), F32),
                   jax.ShapeDtypeStruct((dg, 3, LANES), F32)],
        name="s5_prep",
    )(dup(a_re), dup(a_im), log_dt.reshape(dg, 1, 1).astype(F32), pair(btr, bti), pair(bti, btr),
      pair(cr, ci), pair(ci, cr))
    shp = lambda x: x.reshape((depth, groups) + x.shape[1:])
    return shp(mq), shp(pmat), shp(a8)


def _mixer_kernel(x_ref, h0_ref, buf0_ref, gmix_ref, win_ref, mq_ref, p_ref, a8_ref, d_ref, wglu_ref,
                  bglu_ref, wpool_ref, pscale_ref, wout_ref, gffn_ref, wr_ref, br_ref,
                  route_in_ref, routet_in_ref, x1_ref, route_ref, routet_ref, h_ref, buf_ref, u_scr, y_scr, cat_scr,
                  *, bblk, tile_l, pos0, windows, n_expert_groups, experts_per_group, precise):
    del route_in_ref, routet_in_ref
    li = pl.program_id(1)
    rows = tile_l * bblk
    nk = tile_l // CHUNK
    n = nk * bblk
    d_model = x_ref.shape[-1]
    d_ssm = y_scr.shape[-1]
    d_pool = d_model - d_ssm
    hist = POOL_HIST * bblk

    @pl.when(li == 0)
    def _init():
        h_ref[...] = h0_ref[...]
        cat_scr[0:hist, :] = buf0_ref[...].reshape(hist, d_pool)

    x = x_ref[...]
    ms = jnp.mean(x * x, axis=-1, keepdims=True)
    h = x * lax.rsqrt(ms + EPS) * gmix_ref[...]
    u = _mm(h, win_ref[...], precise)
    u_scr[...] = u.reshape(nk, CHUNK, bblk, d_model)

    slot = lax.broadcasted_iota(jnp.int32, (n, LANES), 1) // (LANES // CHUNK)
    for v in range(d_ssm // LANES):
        rolled = []
        for j in range(CHUNK):
            uj = u_scr[:, j, :, v * LANES:(v + 1) * LANES].reshape(n, LANES)
            rolled.append(pltpu.roll(uj, (LANES // CHUNK) * j, axis=1) if j else uj)
        ys = []
        for s in range(CHUNK):
            g = v * CHUNK + s
            ug = rolled[0]
            for j in range(1, CHUNK):
                ug = jnp.where(slot == (s + j) % CHUNK, rolled[j], ug)
            yq = _mm(ug, mq_ref[g], precise)
            sg = yq[:, LANES:]
            sg_sw = pltpu.roll(sg, LANES // 2, axis=1)
            a8c = a8_ref[g, 0:1, :]
            a8s = a8_ref[g, 1:2, :]
            a8w = a8_ref[g, 2:3, :]
            hcur = h_ref[g]
            hsw = pltpu.roll(hcur, LANES // 2, axis=1)
            hins = []
            for k in range(nk):
                hins.append(hcur)
                sk = sg[k * bblk:(k + 1) * bblk]
                skw = sg_sw[k * bblk:(k + 1) * bblk]
                hcur, hsw = a8c * hcur + a8s * hsw + sk, a8c * hsw + a8w * hcur + skw
            h_ref[g] = hcur
            hin = jnp.concatenate(hins, axis=0) if nk > 1 else hins[0]
            ys.append(yq[:, :LANES] + _mm(hin, p_ref[g], precise))
        for j in range(CHUNK):
            z = ys[0]
            for s in range(1, CHUNK):
                z = jnp.where(slot == (s + j) % CHUNK, ys[s], z)
            if j:
                z = pltpu.roll(z, LANES - (LANES // CHUNK) * j, axis=1)
            y_scr[:, j, :, v * LANES:(v + 1) * LANES] = z.reshape(nk, bblk, LANES)

    u_ssm = u_scr[:, :, :, 0:d_ssm].reshape(rows, d_ssm)
    y = y_scr[...].reshape(rows, d_ssm) + d_ref[...] * u_ssm
    gl = jax.nn.gelu(y)
    ssm_out = gl * jax.nn.sigmoid(_mm(gl, wglu_ref[...], precise) + bglu_ref[...])

    up = u_scr[:, :, :, d_ssm:].reshape(rows, d_pool)
    cat_scr[hist:hist + rows, :] = up
    row_pos = lax.broadcasted_iota(jnp.int32, (rows, 1), 0) // bblk
    pos = pos0 + li * tile_l + row_pos
    pool_parts = []
    pgrp = d_pool // len(windows)
    for kk, w in enumerate(windows):
        lo = kk * pgrp
        acc = cat_scr[hist:hist + rows, lo:lo + pgrp]
        cur = acc
        for sft in range(1, w):
            acc = acc + cat_scr[hist - sft * bblk:hist - sft * bblk + rows, lo:lo + pgrp]
        inv = 1.0 / jnp.minimum(pos + 1, w).astype(F32)
        diff = acc * inv - cur
        pool_parts.append(_mm(diff, wpool_ref[kk], precise))
    pool_out = jnp.concatenate(pool_parts, axis=-1) * pscale_ref[...]
    tail = cat_scr[rows:rows + hist, :]
    cat_scr[0:hist, :] = tail
    buf_ref[...] = tail.reshape(POOL_HIST, bblk, d_pool)

    if precise:
        mix = jnp.concatenate([ssm_out, pool_out], axis=-1)
    else:
        mix = jnp.concatenate([ssm_out.astype(BF16), pool_out.astype(BF16)], axis=-1)
    x1 = x + _mm(mix, wout_ref[...], precise)
    x1_ref[...] = x1
    ms2 = jnp.mean(x1 * x1, axis=-1, keepdims=True)
    t = x1 * lax.rsqrt(ms2 + EPS) * gffn_ref[...]

    t_hi = t.astype(BF16)
    t_lo = (t - t_hi.astype(F32)).astype(BF16)
    w_r = wr_ref[...]
    w_hi = w_r.astype(BF16)
    w_lo = (w_r - w_hi.astype(F32)).astype(BF16)
    logits = _dot(t_hi, w_hi) + _dot(t_lo, w_hi) + _dot(t_hi, w_lo) + br_ref[...]
    lane = lax.broadcasted_iota(jnp.int32, (rows, LANES), 1)
    big = jnp.int32(4 * LANES)
    lg = jnp.where(lane < n_expert_groups, logits, NEG)
    mg = jnp.max(lg, axis=-1, keepdims=True)
    gsel = jnp.min(jnp.where(lg == mg, lane, big), axis=-1, keepdims=True)
    p_sel = 1.0 / jnp.sum(jnp.exp(lg - mg), axis=-1, keepdims=True)
    e_lo = n_expert_groups + gsel * experts_per_group
    le = jnp.where((lane >= e_lo) & (lane < e_lo + experts_per_group), logits, NEG)
    v1 = jnp.max(le, axis=-1, keepdims=True)
    i1 = jnp.min(jnp.where(le == v1, lane, big), axis=-1, keepdims=True)
    le2 = jnp.where(lane == i1, NEG, le)
    v2 = jnp.max(le2, axis=-1, keepdims=True)
    i2 = jnp.min(jnp.where(le2 == v2, lane, big), axis=-1, keepdims=True)
    ex = jnp.exp(v2 - v1)
    q1 = p_sel / (1.0 + ex)
    q2 = q1 * ex
    e1 = (i1 - n_expert_groups).astype(F32)
    e2 = (i2 - n_expert_groups).astype(F32)
    route = jnp.where(lane == 0, e1, jnp.where(lane == 1, e2, jnp.where(lane == 2, q1,
                      jnp.where(lane == 3, q2, 0.0))))
    route_ref[...] = route
    routet_ref[...] = route.T[0:SUBLANES, :]


def _mixer_call(x_flat, route_flat, routet_flat, h0, buf0, lw, *, row_block, batch, bblk, length, tile_l, pos0,
                cfg, precise=False):
    total, d_model = x_flat.shape
    groups = h0.shape[0]
    d_pool = buf0.shape[-1]
    d_ssm = d_model - d_pool
    nb = batch // bblk
    nl = length // tile_l
    rows = tile_l * bblk

    def full(a):
        nd = a.ndim
        return pl.BlockSpec(a.shape, lambda b, l, _n=nd: (0,) * _n)

    tok = lambda width: pl.BlockSpec((rows, width), lambda b, l: (row_block + b * nl + l, 0))
    weights = [lw["g_mix"], lw["w_in"], lw["mq"], lw["pmat"], lw["a8"], lw["ssm_d"], lw["w_glu"], lw["b_glu"],
               lw["w_pool"], lw["pool_scale"], lw["w_out"], lw["g_ffn"], lw["w_r"], lw["b_r"]]
    in_specs = ([tok(d_model),
                 pl.BlockSpec((groups, bblk, LANES), lambda b, l: (0, b, 0)),
                 pl.BlockSpec((POOL_HIST, bblk, d_pool), lambda b, l: (0, b, 0))]
                + [full(w) for w in weights]
                + [pl.BlockSpec(memory_space=pl.ANY)] * 2)
    out_specs = [tok(d_model), tok(LANES),
                 pl.BlockSpec((SUBLANES, rows), lambda b, l: (0, row_block + b * nl + l)),
                 pl.BlockSpec((groups, bblk, LANES), lambda b, l: (0, b, 0)),
                 pl.BlockSpec((POOL_HIST, bblk, d_pool), lambda b, l: (0, b, 0))]
    out_shape = [jax.ShapeDtypeStruct((total, d_model), F32), jax.ShapeDtypeStruct((total, LANES), F32),
                 jax.ShapeDtypeStruct((SUBLANES, total), F32),
                 jax.ShapeDtypeStruct((groups, batch, LANES), F32),
                 jax.ShapeDtypeStruct((POOL_HIST, batch, d_pool), F32)]
    n_in = 3 + len(weights)
    kern = functools.partial(_mixer_kernel, bblk=bblk, tile_l=tile_l, pos0=pos0, windows=cfg["windows"],
                             n_expert_groups=cfg["n_expert_groups"], experts_per_group=cfg["experts_per_group"],
                             precise=precise)
    args = [x_flat, h0, buf0] + weights + [route_flat, routet_flat]
    return pl.pallas_call(
        kern,
        grid=(nb, nl),
        in_specs=in_specs,
        out_specs=out_specs,
        out_shape=out_shape,
        scratch_shapes=[pltpu.VMEM((tile_l // CHUNK, CHUNK, bblk, d_model), F32),
                        pltpu.VMEM((tile_l // CHUNK, CHUNK, bblk, d_ssm), F32),
                        pltpu.VMEM(((POOL_HIST + tile_l) * bblk, d_pool), F32)],
        input_output_aliases={0: 0, n_in: 1, n_in + 1: 2},
        compiler_params=pltpu.CompilerParams(dimension_semantics=("arbitrary", "arbitrary"),
                                             vmem_limit_bytes=VMEM_LIMIT),
        name="mixer",
    )(*args)


def _plan_kernel(route_ref, cnt_ref):
    r = route_ref[...]
    lanef = lax.broadcasted_iota(jnp.int32, r.shape, 1).astype(F32)
    oh = jnp.logical_or(lanef == r[:, 0:1], lanef == r[:, 1:2]).astype(F32)
    cnt_ref[...] = jnp.sum(oh, axis=0, keepdims=True)


def _plan_call(route):
    total = route.shape[0]
    nt = total // MOE_TILE
    return pl.pallas_call(
        _plan_kernel,
        grid=(nt,),
        in_specs=[pl.BlockSpec((MOE_TILE, LANES), lambda i: (i, 0))],
        out_specs=pl.BlockSpec((None, 1, LANES), lambda i: (i, 0, 0)),
        out_shape=jax.ShapeDtypeStruct((nt, 1, LANES), F32),
        compiler_params=pltpu.CompilerParams(dimension_semantics=("arbitrary",)),
        name="moe_plan",
    )(route)


def _chunk_copy(src_ref, src_row, dst_ref, dst_row, sem):
    src = src_ref.at[pl.ds(pl.multiple_of(src_row, SUBLANES), SUBLANES), :]
    dst = dst_ref.at[pl.ds(pl.multiple_of(dst_row, SUBLANES), SUBLANES), :]
    return pltpu.make_async_copy(src, dst, sem)


def _dispatch_kernel(cmap_ref, nch_ref, pstart_ref, plen_ref, nact_ref, x_ref, rt_ref, loff_ref, gffn_ref, xs_ref,
                     xloc, zero_scr, sem, zsem, *, n_exp, n_tiles, n_chunks):
    step = pl.program_id(0)

    def pad_copies(e, op):
        start = pstart_ref[e]
        length = plen_ref[e]
        for b in range(SUBLANES.bit_length() - 1, EXPERT_TILE.bit_length() - 1):
            size = 1 << b

            @pl.when(((length >> b) & 1) == 1)
            def _():
                off = pl.multiple_of(start + (length & (size - 1)), SUBLANES)
                op(pltpu.make_async_copy(zero_scr.at[pl.ds(0, size), :], xs_ref.at[pl.ds(off, size), :], zsem))

    def tail_copy(tile):
        row0 = pl.multiple_of(tile * EXPERT_TILE, EXPERT_TILE)
        return pltpu.make_async_copy(zero_scr, xs_ref.at[pl.ds(row0, EXPERT_TILE), :], zsem)

    def for_all_fill(op):
        def per_expert(e, c):
            pad_copies(e, op)
            return c

        def per_tile(tile, c):
            op(tail_copy(tile))
            return c

        lax.fori_loop(0, n_exp, per_expert, 0)
        lax.fori_loop(nact_ref[0], n_tiles, per_tile, 0)

    @pl.when(step == 0)
    def _fill():
        zero_scr[...] = jnp.zeros_like(zero_scr)
        for_all_fill(lambda cp: cp.start())
        for_all_fill(lambda cp: cp.wait())

    x = x_ref[...]
    tt = x.shape[0]
    ms = jnp.mean(x * x, axis=-1, keepdims=True)
    t = (x * lax.rsqrt(ms + EPS) * gffn_ref[...]).astype(BF16)

    e1 = rt_ref[0:1, :]
    e2 = rt_ref[1:2, :]
    eio = lax.broadcasted_iota(jnp.int32, (n_exp, tt), 0).astype(F32)
    oh1 = eio == e1
    oh2 = eio == e2
    before = lax.broadcasted_iota(jnp.int32, (tt, tt), 0) < lax.broadcasted_iota(jnp.int32, (tt, tt), 1)
    rank = _dot(jnp.logical_or(oh1, oh2).astype(BF16), before.astype(BF16))
    where_to = loff_ref[:, 0:1] + rank
    lpos1 = jnp.sum(jnp.where(oh1, where_to, 0.0), axis=0, keepdims=True).astype(jnp.int32)
    lpos2 = jnp.sum(jnp.where(oh2, where_to, 0.0), axis=0, keepdims=True).astype(jnp.int32)
    rio = lax.broadcasted_iota(jnp.int32, (xloc.shape[0], tt), 0)
    perm = jnp.logical_or(rio == lpos1, rio == lpos2).astype(BF16)
    xloc[...] = _pack_halves(_dot(perm, t))

    base = step * n_chunks

    def chunk(c):
        return _chunk_copy(xloc, c * SUBLANES, xs_ref, cmap_ref[base + c], sem)

    def issue(c, carry):
        chunk(c).start()
        return carry

    def drain(c, carry):
        chunk(c).wait()
        return carry

    lax.fori_loop(0, nch_ref[step], issue, 0)
    lax.fori_loop(0, nch_ref[step], drain, 0)


def _dispatch_call(cmap, nch, pad_start, pad_len, n_active, x_flat, route_t, loff_col, g_ffn, n_tiles, n_chunks):
    total, d_model = x_flat.shape
    n_exp = pad_start.shape[0]
    return pl.pallas_call(
        functools.partial(_dispatch_kernel, n_exp=n_exp, n_tiles=n_tiles, n_chunks=n_chunks),
        grid_spec=pltpu.PrefetchScalarGridSpec(
            num_scalar_prefetch=5,
            grid=(total // MOE_TILE,),
            in_specs=[pl.BlockSpec((MOE_TILE, d_model), lambda i, *_: (i, 0)),
                      pl.BlockSpec((SUBLANES, MOE_TILE), lambda i, *_: (0, i)),
                      pl.BlockSpec((None, n_exp, LANES), lambda i, *_: (i, 0, 0)),
                      pl.BlockSpec((1, d_model), lambda i, *_: (0, 0))],
            out_specs=pl.BlockSpec(memory_space=pl.ANY),
            scratch_shapes=[pltpu.VMEM((n_chunks * SUBLANES, d_model // 2), jnp.uint32),
                            pltpu.VMEM((EXPERT_TILE, d_model // 2), jnp.uint32),
                            pltpu.SemaphoreType.DMA, pltpu.SemaphoreType.DMA],
        ),
        out_shape=jax.ShapeDtypeStruct((n_tiles * EXPERT_TILE, d_model // 2), jnp.uint32),
        compiler_params=pltpu.CompilerParams(dimension_semantics=("arbitrary",), vmem_limit_bytes=VMEM_LIMIT),
        name="moe_dispatch",
    )(cmap, nch, pad_start, pad_len, n_active, x_flat, route_t, loff_col, g_ffn)


def _experts_kernel(blk_ref, exp_ref, nact_ref, xs_ref, wg_ref, wu_ref, wd_ref, ys_ref, wgu_scr, wd_scr):
    i = pl.program_id(0)
    f = wg_ref.shape[-1]
    active = i < nact_ref[0]
    changed = jnp.logical_or(i == 0, exp_ref[i] != exp_ref[jnp.maximum(i - 1, 0)])

    @pl.when(jnp.logical_and(active, changed))
    def _():
        wgu_scr[:, 0:f] = wg_ref[...].astype(BF16)
        wgu_scr[:, f:2 * f] = wu_ref[...].astype(BF16)
        wd_scr[...] = wd_ref[...].astype(BF16)

    @pl.when(active)
    def _():
        half = wgu_scr.shape[0] // 2
        t_lo, t_hi = _unpack_halves(xs_ref[...])
        gu = _dot(t_lo, wgu_scr[0:half, :]) + _dot(t_hi, wgu_scr[half:2 * half, :])
        act = jax.nn.silu(gu[:, 0:f]) * gu[:, f:2 * f]
        ys_ref[...] = _pack_halves(_dot(act.astype(BF16), wd_scr[...]))

    @pl.when(jnp.logical_not(active))
    def _():
        ys_ref[...] = jnp.zeros_like(ys_ref)


def _experts_call(tile_blk, tile_exp, n_active, xs, w_gate, w_up, w_down, layer):
    nrows = xs.shape[0]
    _, n_exp, d_model, f = w_gate.shape
    n_tiles = nrows // EXPERT_TILE
    return pl.pallas_call(
        _experts_kernel,
        grid_spec=pltpu.PrefetchScalarGridSpec(
            num_scalar_prefetch=3,
            grid=(n_tiles,),
            in_specs=[pl.BlockSpec((EXPERT_TILE, d_model // 2), lambda i, blk, ex, na: (blk[i], 0)),
                      pl.BlockSpec((None, None, d_model, f), lambda i, blk, ex, na: (layer, ex[i], 0, 0)),
                      pl.BlockSpec((None, None, d_model, f), lambda i, blk, ex, na: (layer, ex[i], 0, 0)),
                      pl.BlockSpec((None, None, f, d_model), lambda i, blk, ex, na: (layer, ex[i], 0, 0))],
            out_specs=pl.BlockSpec((EXPERT_TILE, d_model // 2), lambda i, blk, ex, na: (i, 0)),
            scratch_shapes=[pltpu.VMEM((d_model, 2 * f), BF16), pltpu.VMEM((f, d_model), BF16)],
        ),
        out_shape=jax.ShapeDtypeStruct((nrows, d_model // 2), jnp.uint32),
        compiler_params=pltpu.CompilerParams(dimension_semantics=("arbitrary",), vmem_limit_bytes=VMEM_LIMIT),
        name="moe_experts",
    )(tile_blk, tile_exp, n_active, xs, w_gate, w_up, w_down)


def _combine_kernel(cmap_ref, nch_ref, x1_ref, route_ref, loff_ref, gfin_ref, ys_ref, out_ref, yloc, sem,
                    *, final_norm, n_chunks):
    step = pl.program_id(0)
    base = step * n_chunks

    @pl.when(step == 0)
    def _():
        yloc[...] = jnp.zeros_like(yloc)

    def chunk(c):
        return _chunk_copy(ys_ref, cmap_ref[base + c], yloc, c * SUBLANES, sem)

    def issue(c, carry):
        chunk(c).start()
        return carry

    def drain(c, carry):
        chunk(c).wait()
        return carry

    lax.fori_loop(0, nch_ref[step], issue, 0)

    route = route_ref[...]
    tt = route.shape[0]
    lanef = lax.broadcasted_iota(jnp.int32, route.shape, 1).astype(F32)
    oh1 = lanef == route[:, 0:1]
    oh2 = lanef == route[:, 1:2]
    earlier = lax.broadcasted_iota(jnp.int32, (tt, tt), 0) > lax.broadcasted_iota(jnp.int32, (tt, tt), 1)
    rank = _dot(earlier.astype(BF16), jnp.logical_or(oh1, oh2).astype(BF16))
    where_to = loff_ref[...] + rank
    lpos1 = jnp.sum(jnp.where(oh1, where_to, 0.0), axis=-1, keepdims=True).astype(jnp.int32)
    lpos2 = jnp.sum(jnp.where(oh2, where_to, 0.0), axis=-1, keepdims=True).astype(jnp.int32)
    lio = lax.broadcasted_iota(jnp.int32, (tt, yloc.shape[0]), 1)
    wperm = jnp.where(lio == lpos1, route[:, 2:3], jnp.where(lio == lpos2, route[:, 3:4], 0.0)).astype(BF16)

    lax.fori_loop(0, nch_ref[step], drain, 0)
    y_lo, y_hi = _unpack_halves(yloc[...])
    x2 = x1_ref[...] + jnp.concatenate([_dot(wperm, y_lo), _dot(wperm, y_hi)], axis=-1)
    if final_norm:
        ms = jnp.mean(x2 * x2, axis=-1, keepdims=True)
        x2 = x2 * lax.rsqrt(ms + EPS) * gfin_ref[...]
    out_ref[...] = x2


def _combine_call(cmap, nch, x1, route, loff_row, g_final, ys, *, final_norm, n_chunks):
    total, d_model = x1.shape
    return pl.pallas_call(
        functools.partial(_combine_kernel, final_norm=final_norm, n_chunks=n_chunks),
        grid_spec=pltpu.PrefetchScalarGridSpec(
            num_scalar_prefetch=2,
            grid=(total // MOE_TILE,),
            in_specs=[pl.BlockSpec((MOE_TILE, d_model), lambda i, *_: (i, 0)),
                      pl.BlockSpec((MOE_TILE, LANES), lambda i, *_: (i, 0)),
                      pl.BlockSpec((None, 1, LANES), lambda i, *_: (i, 0, 0)),
                      pl.BlockSpec((1, d_model), lambda i, *_: (0, 0)),
                      pl.BlockSpec(memory_space=pl.ANY)],
            out_specs=pl.BlockSpec((MOE_TILE, d_model), lambda i, *_: (i, 0)),
            scratch_shapes=[pltpu.VMEM((n_chunks * SUBLANES, d_model // 2), jnp.uint32), pltpu.SemaphoreType.DMA],
        ),
        out_shape=jax.ShapeDtypeStruct((total, d_model), F32),
        input_output_aliases={2: 0},
        compiler_params=pltpu.CompilerParams(dimension_semantics=("arbitrary",), vmem_limit_bytes=VMEM_LIMIT),
        name="moe_combine",
    )(cmap, nch, x1, route, loff_row, g_final, ys)


def _moe(x1, route, route_t, g_ffn, w_gate, w_up, w_down, g_final, *, layer, final_norm):
    total, d_model = x1.shape
    n_exp = w_gate.shape[1]
    nt = total // MOE_TILE
    counts = _plan_call(route)[:, 0, :n_exp].astype(jnp.int32)
    run = (counts + SUBLANES - 1) // SUBLANES * SUBLANES
    lend = jnp.cumsum(run, axis=1)
    loff = lend - run
    gend = jnp.cumsum(run, axis=0)
    seg_rows = gend[-1]
    seg_tiles = (seg_rows + EXPERT_TILE - 1) // EXPERT_TILE
    seg_end = jnp.cumsum(seg_tiles)
    seg_off = (seg_end - seg_tiles) * EXPERT_TILE
    goff = seg_off[None, :] + gend - run
    n_chunks = (2 * MOE_TILE + n_exp * (SUBLANES - 1) + SUBLANES - 1) // SUBLANES
    n_chunks = (n_chunks + 15) // 16 * 16
    chunk_row = jnp.arange(n_chunks, dtype=jnp.int32) * SUBLANES
    in_run = jnp.logical_and(chunk_row[None, :, None] >= loff[:, None, :], chunk_row[None, :, None] < lend[:, None, :])
    cmap = (chunk_row[None, :] + jnp.sum(jnp.where(in_run, (goff - loff)[:, None, :], 0), axis=2)).reshape(-1)
    nch = lend[:, -1] // SUBLANES
    n_tiles = (2 * total + nt * n_exp * (SUBLANES - 1) + n_exp * (EXPERT_TILE - 1)) // EXPERT_TILE + 1
    n_active = seg_end[-1].reshape(1).astype(jnp.int32)
    tile_blk = jnp.minimum(jnp.arange(n_tiles, dtype=jnp.int32), n_active - 1)
    tile_exp = jnp.sum((tile_blk[:, None] >= seg_end[None, :]).astype(jnp.int32), axis=1)
    tile_exp = jnp.minimum(tile_exp, n_exp - 1)
    lofff = loff.astype(F32)
    loff_col = jnp.broadcast_to(lofff[:, :, None], (nt, n_exp, LANES))
    loff_row = jnp.pad(lofff, ((0, 0), (0, LANES - n_exp)))[:, None, :]
    xs = _dispatch_call(cmap, nch, seg_off + seg_rows, seg_tiles * EXPERT_TILE - seg_rows, n_active, x1, route_t,
                        loff_col, g_ffn, n_tiles, n_chunks)
    ys = _experts_call(tile_blk, tile_exp, n_active, xs, w_gate, w_up, w_down, layer)
    return _combine_call(cmap, nch, x1, route, loff_row, g_final, ys, final_norm=final_norm, n_chunks=n_chunks)


def _to_rows_kernel(xp_ref, xs_ref, meta_ref, o_ref, *, n_p_steps, n_s_steps):
    i = pl.program_id(0)
    pb, pl_ = xp_ref.shape[0], xp_ref.shape[1]
    sb, sl = xs_ref.shape[0], xs_ref.shape[1]
    ml = meta_ref.shape[0]
    mb = o_ref.shape[0] // ml

    @pl.when(i < n_p_steps)
    def _():
        for l in range(pl_):
            o_ref[l * pb:(l + 1) * pb, :] = xp_ref[:, l, :]

    @pl.when(jnp.logical_and(i >= n_p_steps, i < n_p_steps + n_s_steps))
    def _():
        for l in range(sl):
            o_ref[l * sb:(l + 1) * sb, :] = xs_ref[:, l, :]

    @pl.when(i >= n_p_steps + n_s_steps)
    def _():
        for l in range(ml):
            o_ref[l * mb:(l + 1) * mb, :] = jnp.broadcast_to(meta_ref[l:l + 1, :], (mb, o_ref.shape[1]))


def _to_rows_call(x_prompt, x_sample, meta, *, step_rows):
    batch, seq, d_model = x_prompt.shape
    dec_batch, dec_seq, _ = x_sample.shape
    p_tile_l, s_bblk = step_rows // batch, step_rows // dec_seq
    n_p_steps, n_s_steps = seq // p_tile_l, dec_batch // s_bblk
    steps = n_p_steps + n_s_steps + 1
    return pl.pallas_call(
        functools.partial(_to_rows_kernel, n_p_steps=n_p_steps, n_s_steps=n_s_steps),
        grid=(steps,),
        in_specs=[pl.BlockSpec((batch, p_tile_l, d_model), lambda i: (0, jnp.minimum(i, n_p_steps - 1), 0)),
                  pl.BlockSpec((s_bblk, dec_seq, d_model),
                               lambda i: (jnp.clip(i - n_p_steps, 0, n_s_steps - 1), 0, 0)),
                  pl.BlockSpec(meta.shape, lambda i: (0, 0))],
        out_specs=pl.BlockSpec((step_rows, d_model), lambda i: (i, 0)),
        out_shape=jax.ShapeDtypeStruct((steps * step_rows, d_model), F32),
        compiler_params=pltpu.CompilerParams(dimension_semantics=("arbitrary",)),
        name="to_rows",
    )(x_prompt, x_sample, meta)


def _from_rows_kernel(y_ref, yp_ref, ys_ref, *, n_p_steps):
    i = pl.program_id(0)
    pb, pl_ = yp_ref.shape[0], yp_ref.shape[1]
    sb, sl = ys_ref.shape[0], ys_ref.shape[1]

    @pl.when(i < n_p_steps)
    def _():
        for l in range(pl_):
            yp_ref[:, l, :] = y_ref[l * pb:(l + 1) * pb, :]

    @pl.when(i >= n_p_steps)
    def _():
        for l in range(sl):
            ys_ref[:, l, :] = y_ref[l * sb:(l + 1) * sb, :]


def _from_rows_call(y_flat, batch, seq, dec_batch, dec_seq, *, step_rows):
    d_model = y_flat.shape[-1]
    p_tile_l, s_bblk = step_rows // batch, step_rows // dec_seq
    n_p_steps, n_s_steps = seq // p_tile_l, dec_batch // s_bblk
    return pl.pallas_call(
        functools.partial(_from_rows_kernel, n_p_steps=n_p_steps),
        grid=(n_p_steps + n_s_steps,),
        in_specs=[pl.BlockSpec((step_rows, d_model), lambda i: (i, 0))],
        out_specs=[pl.BlockSpec((batch, p_tile_l, d_model), lambda i: (0, jnp.minimum(i, n_p_steps - 1), 0)),
                   pl.BlockSpec((s_bblk, dec_seq, d_model),
                                lambda i: (jnp.clip(i - n_p_steps, 0, n_s_steps - 1), 0, 0))],
        out_shape=[jax.ShapeDtypeStruct((batch, seq, d_model), F32),
                   jax.ShapeDtypeStruct((dec_batch, dec_seq, d_model), F32)],
        compiler_params=pltpu.CompilerParams(dimension_semantics=("arbitrary",)),
        name="from_rows",
    )(y_flat)


def kernel(x_prompt, x_sample, state_ssm_re, state_ssm_im, state_pool, meta_tokens, norm_mix_g, w_in, ssm_a_re,
           ssm_a_im, ssm_log_dt, ssm_b_re, ssm_b_im, ssm_c_re, ssm_c_im, ssm_d, w_glu, b_glu, w_pool, pool_scale,
           w_out, norm_ffn_g, w_router_group, b_router_group, w_router_expert, b_router_expert, w_gate, w_up,
           w_down, norm_final_g):
    batch, seq, d_model = x_prompt.shape
    dec_batch, dec_seq, _ = x_sample.shape
    depth, _, groups, nstate = state_ssm_re.shape
    n_meta = meta_tokens.shape[0]
    pool_buf, d_pool = state_pool.shape[2], state_pool.shape[3]
    n_pool_groups = w_pool.shape[1]
    windows = tuple(2 ** (k + 1) for k in range(n_pool_groups))
    assert pool_buf == max(windows) - 1 == POOL_HIST - 1
    n_expert_groups = w_router_group.shape[-1]
    n_experts = w_router_expert.shape[-1]
    cfg = dict(windows=windows, n_expert_groups=n_expert_groups, experts_per_group=n_experts // n_expert_groups)

    step_rows = 512
    p_tile_l, s_bblk, m_batch = step_rows // batch, step_rows // dec_seq, step_rows // n_meta
    n_p, n_s, n_m = batch * seq, dec_batch * dec_seq, m_batch * n_meta
    assert m_batch >= batch and seq % p_tile_l == 0 and dec_batch % s_bblk == 0
    total = n_p + n_s + n_m
    assert total % step_rows == 0 and step_rows % MOE_TILE == 0

    mq, pmat, a8 = _s5_prepare(ssm_a_re, ssm_a_im, ssm_log_dt, ssm_b_re, ssm_b_im, ssm_c_re, ssm_c_im)

    x = _to_rows_call(x_prompt.astype(F32), x_sample.astype(F32), meta_tokens.astype(F32), step_rows=step_rows)
    assert x.shape[0] == total

    row2 = lambda v: v.reshape(1, -1).astype(F32)
    outs = {k: [] for k in ("hp", "bp", "hs", "bs")}
    for l in range(depth):
        w_r = jnp.concatenate([w_router_group[l], w_router_expert[l]], axis=1).astype(F32)
        w_r = jnp.pad(w_r, ((0, 0), (0, LANES - w_r.shape[1])))
        b_r = jnp.concatenate([b_router_group[l], b_router_expert[l]]).astype(F32)
        lw32 = dict(g_mix=row2(norm_mix_g[l]), w_in=w_in[l].astype(F32), mq=mq[l], pmat=pmat[l], a8=a8[l],
                    ssm_d=row2(ssm_d[l]), w_glu=w_glu[l].astype(F32), b_glu=row2(b_glu[l]),
                    w_pool=w_pool[l].astype(F32), pool_scale=row2(pool_scale[l]), w_out=w_out[l].astype(F32),
                    g_ffn=row2(norm_ffn_g[l]), w_r=w_r,
                    b_r=jnp.pad(b_r, (0, LANES - b_r.shape[0])).reshape(1, LANES))
        lw = dict(lw32, **{k: lw32[k].astype(BF16) for k in ("w_in", "mq", "pmat", "w_glu", "w_pool", "w_out")})

        zero_h = jnp.zeros((groups, m_batch, LANES), F32)
        zero_buf = jnp.zeros((POOL_HIST, m_batch, d_pool), F32)
        route = jnp.zeros((total, LANES), F32)
        route_t = jnp.zeros((SUBLANES, total), F32)
        x, route, route_t, h_m, buf_m = _mixer_call(
            x, route, route_t, zero_h, zero_buf, lw, row_block=(n_p + n_s) // step_rows, batch=m_batch,
            bblk=m_batch, length=n_meta, tile_l=n_meta, pos0=0, cfg=cfg)
        n_fast = seq - p_tile_l if l < depth - 1 else seq
        x, route, route_t, h_p, buf_p = _mixer_call(
            x, route, route_t, h_m[:, :batch], buf_m[:, :batch], lw, row_block=0, batch=batch, bblk=batch,
            length=n_fast, tile_l=p_tile_l, pos0=n_meta, cfg=cfg)
        if n_fast < seq:
            x, route, route_t, h_p, buf_p = _mixer_call(
                x, route, route_t, h_p, buf_p, lw32, row_block=n_fast // p_tile_l, batch=batch, bblk=batch,
                length=seq - n_fast, tile_l=p_tile_l, pos0=n_meta + n_fast, cfg=cfg, precise=True)
        h0_s = jnp.concatenate([state_ssm_re[l], state_ssm_im[l]], axis=-1).transpose(1, 0, 2).astype(F32)
        buf0_s = jnp.pad(state_pool[l].astype(F32).transpose(1, 0, 2), ((1, 0), (0, 0), (0, 0)))
        x, route, route_t, h_s, buf_s = _mixer_call(
            x, route, route_t, h0_s, buf0_s, lw, row_block=n_p // step_rows, batch=dec_batch, bblk=s_bblk,
            length=dec_seq, tile_l=dec_seq, pos0=PAST_LEN, cfg=cfg)
        outs["hp"].append(h_p)
        outs["bp"].append(buf_p)
        outs["hs"].append(h_s)
        outs["bs"].append(buf_s)
        x = _moe(x, route, route_t, lw["g_ffn"], w_gate, w_up, w_down, row2(norm_final_g), layer=l,
                 final_norm=(l == depth - 1))

    y_prompt, y_sample = _from_rows_call(x, batch, seq, dec_batch, dec_seq, step_rows=step_rows)
    st = lambda hs: jnp.stack(hs).transpose(0, 2, 1, 3)
    pl_out = lambda bs: jnp.stack(bs)[:, 1:].transpose(0, 2, 1, 3)
    hp, hs = st(outs["hp"]), st(outs["hs"])
    return (y_prompt, y_sample, hp[..., :nstate], hp[..., nstate:], pl_out(outs["bp"]),
            hs[..., :nstate], hs[..., nstate:], pl_out(outs["bs"]))
```

```python
import functools

import numpy as np
import jax
import jax.numpy as jnp
from jax import lax
from jax.experimental import pallas as pl
from jax.experimental.pallas import tpu as pltpu

F32 = jnp.float32
BF16 = jnp.bfloat16

LANES = 128
SUBLANES = 8
CHUNK = 8
POOL_HIST = 16
MOE_TILE = 256
EXPERT_TILE = 256
VMEM_LIMIT = 56 * 1024 * 1024
EPS = 1e-6
NEG = -1e30
PAST_LEN = 16384


def _dot(a, b):
    return jnp.dot(a, b, preferred_element_type=F32)


def _mm(a, w, precise):
    if precise:
        return jnp.dot(a, w, precision=lax.Precision.HIGHEST, preferred_element_type=F32)
    return _dot(a.astype(BF16), w)


def _pack_halves(x):
    k = x.shape[-1] // 2
    return pltpu.pack_elementwise([x[:, :k], x[:, k:]], packed_dtype=BF16)


def _unpack_halves(p):
    lo = pltpu.unpack_elementwise(p, index=0, packed_dtype=BF16, unpacked_dtype=F32)
    hi = pltpu.unpack_elementwise(p, index=1, packed_dtype=BF16, unpacked_dtype=F32)
    return lo.astype(BF16), hi.astype(BF16)


def _dot_nt_f32(a, b):
    return lax.dot_general(a, b, (((1,), (1,)), ((), ())), precision=lax.Precision.HIGHEST,
                           preferred_element_type=F32)


def _s5_prep_kernel(a_re_ref, a_im_ref, ldt_ref, bt_ref, btx_ref, c_ref, cx_ref, mq_ref, p_ref, a8_ref):
    ch = bt_ref.shape[1]
    first_half = lax.broadcasted_iota(jnp.int32, (1, LANES), 1) < LANES // 2
    sgn = jnp.where(first_half, 1.0, -1.0)
    for s in range(CHUNK):
        a_re = a_re_ref[s]
        a_im = a_im_ref[s]
        dt = jnp.exp(ldt_ref[s])
        mag = jnp.exp(dt * a_re)
        abr = mag * jnp.cos(dt * a_im)
        abi = mag * jnp.sin(dt * a_im)
        den = a_re * a_re + a_im * a_im
        nr = abr - 1.0
        f_re = (nr * a_re + abi * a_im) / den
        f_im = (abi * a_re - nr * a_im) / den
        bb = f_re * bt_ref[s] - sgn * f_im * btx_ref[s]
        cc = c_ref[s] * sgn
        ccx = -cx_ref[s]
        bbx = f_re * btx_ref[s] + sgn * f_im * bt_ref[s]
        pr = jnp.ones_like(a_re)
        pi = jnp.zeros_like(a_re)
        e2, w2 = [], []
        for k in range(CHUNK + 1):
            e2.append(pr * cc + pi * ccx)
            w2.append(pr * bb - sgn * pi * bbx)
            if k < CHUNK:
                pr, pi = pr * abr - pi * abi, pr * abi + pi * abr
        a8_ref[s, 0:1, :] = pr
        a8_ref[s, 1:2, :] = -sgn * pi
        a8_ref[s, 2:3, :] = sgn * pi
        zero = jnp.zeros_like(bb)
        jpos = [(p - s) % CHUNK for p in range(CHUNK)]
        for p in range(CHUNK):
            j = jpos[p]
            taps = jnp.concatenate([e2[jp - j] if jp >= j else zero for jp in jpos], axis=0)
            rows = slice(p * ch, (p + 1) * ch)
            mq_ref[s, rows, 0:LANES] = _dot_nt_f32(bb, taps)
            mq_ref[s, rows, LANES:2 * LANES] = w2[CHUNK - 1 - j]
        pt = jnp.concatenate([e2[jp + 1] for jp in jpos], axis=0)
        p_ref[s] = pt.T


def _s5_prepare(a_re, a_im, log_dt, b_re, b_im, c_re, c_im):
    depth, groups, nstate = a_re.shape
    ch = b_re.shape[-1]
    dg = depth * groups
    assert ch * CHUNK == LANES and 2 * nstate == LANES and groups % CHUNK == 0
    dup = lambda x: jnp.concatenate([x, x], axis=-1).reshape(dg, 1, LANES).astype(F32)
    btr = jnp.swapaxes(b_re.reshape(dg, nstate, ch), 1, 2).astype(F32)
    bti = jnp.swapaxes(b_im.reshape(dg, nstate, ch), 1, 2).astype(F32)
    cr = c_re.reshape(dg, ch, nstate).astype(F32)
    ci = c_im.reshape(dg, ch, nstate).astype(F32)
    pair = lambda x, y: jnp.concatenate([x, y], axis=-1)
    blk = lambda *shape: pl.BlockSpec((CHUNK,) + shape, lambda g: (g,) + (0,) * len(shape))
    mq, pmat, a8 = pl.pallas_call(
        _s5_prep_kernel,
        grid=(dg // CHUNK,),
        in_specs=[blk(1, LANES), blk(1, LANES), blk(1, 1)] + [blk(ch, LANES)] * 4,
        out_specs=[blk(LANES, 2 * LANES), blk(LANES, LANES), blk(3, LANES)],
        out_shape=[jax.ShapeDtypeStruct((dg, LANES, 2 * LANES), F32), jax.ShapeDtypeStruct((dg, LANES, LANES), F32),
                   jax.ShapeDtypeStruct((dg, 3, LANES), F32)],
        name="s5_prep",
    )(dup(a_re), dup(a_im), log_dt.reshape(dg, 1, 1).astype(F32), pair(btr, bti), pair(bti, btr),
      pair(cr, ci), pair(ci, cr))
    shp = lambda x: x.reshape((depth, groups) + x.shape[1:])
    return shp(mq), shp(pmat), shp(a8)


N_MIXER_WEIGHTS = 14


def _stream_step(x_ref, w_refs, h_init, buf_init, first, pos_base, x1_ref, route_ref, xloc_ref, cnt_ref,
                 h_ref, buf_ref, u_scr, y_scr, cat_scr, *, bblk, tile_l, cfg, precise):
    (gmix_ref, win_ref, mq_ref, p_ref, a8_ref, d_ref, wglu_ref, bglu_ref, wpool_ref, pscale_ref, wout_ref,
     gffn_ref, wr_ref, br_ref) = w_refs
    windows = cfg["windows"]
    n_expert_groups = cfg["n_expert_groups"]
    experts_per_group = cfg["experts_per_group"]
    rows = tile_l * bblk
    nk = tile_l // CHUNK
    n = nk * bblk
    d_model = x_ref.shape[-1]
    d_ssm = y_scr.shape[-1]
    d_pool = d_model - d_ssm
    hist = POOL_HIST * bblk

    def _init():
        h_ref[...] = h_init()
        cat_scr[0:hist, :] = buf_init().reshape(hist, d_pool)

    if first is True:
        _init()
    else:
        pl.when(first)(_init)

    x = x_ref[...]
    ms = jnp.mean(x * x, axis=-1, keepdims=True)
    h = x * lax.rsqrt(ms + EPS) * gmix_ref[...]
    u = _mm(h, win_ref[...], precise)
    u_scr[...] = u.reshape(nk, CHUNK, bblk, d_model)

    slot = lax.broadcasted_iota(jnp.int32, (n, LANES), 1) // (LANES // CHUNK)
    for v in range(d_ssm // LANES):
        rolled = []
        for j in range(CHUNK):
            uj = u_scr[:, j, :, v * LANES:(v + 1) * LANES].reshape(n, LANES)
            rolled.append(pltpu.roll(uj, (LANES // CHUNK) * j, axis=1) if j else uj)
        ys = []
        for s in range(CHUNK):
            g = v * CHUNK + s
            ug = rolled[0]
            for j in range(1, CHUNK):
                ug = jnp.where(slot == (s + j) % CHUNK, rolled[j], ug)
            yq = _mm(ug, mq_ref[g], precise)
            sg = yq[:, LANES:]
            sg_sw = pltpu.roll(sg, LANES // 2, axis=1)
            a8c = a8_ref[g, 0:1, :]
            a8s = a8_ref[g, 1:2, :]
            a8w = a8_ref[g, 2:3, :]
            hcur = h_ref[g]
            hsw = pltpu.roll(hcur, LANES // 2, axis=1)
            hins = []
            for k in range(nk):
                hins.append(hcur)
                sk = sg[k * bblk:(k + 1) * bblk]
                skw = sg_sw[k * bblk:(k + 1) * bblk]
                hcur, hsw = a8c * hcur + a8s * hsw + sk, a8c * hsw + a8w * hcur + skw
            h_ref[g] = hcur
            hin = jnp.concatenate(hins, axis=0) if nk > 1 else hins[0]
            ys.append(yq[:, :LANES] + _mm(hin, p_ref[g], precise))
        for j in range(CHUNK):
            z = ys[0]
            for s in range(1, CHUNK):
                z = jnp.where(slot == (s + j) % CHUNK, ys[s], z)
            if j:
                z = pltpu.roll(z, LANES - (LANES // CHUNK) * j, axis=1)
            y_scr[:, j, :, v * LANES:(v + 1) * LANES] = z.reshape(nk, bblk, LANES)

    u_ssm = u_scr[:, :, :, 0:d_ssm].reshape(rows, d_ssm)
    y = y_scr[...].reshape(rows, d_ssm) + d_ref[...] * u_ssm
    gl = jax.nn.gelu(y)
    ssm_out = gl * jax.nn.sigmoid(_mm(gl, wglu_ref[...], precise) + bglu_ref[...])

    up = u_scr[:, :, :, d_ssm:].reshape(rows, d_pool)
    cat_scr[hist:hist + rows, :] = up
    row_pos = lax.broadcasted_iota(jnp.int32, (rows, 1), 0) // bblk
    pos = pos_base + row_pos
    pool_parts = []
    pgrp = d_pool // len(windows)
    for kk, w in enumerate(windows):
        lo = kk * pgrp
        acc = cat_scr[hist:hist + rows, lo:lo + pgrp]
        cur = acc
        for sft in range(1, w):
            acc = acc + cat_scr[hist - sft * bblk:hist - sft * bblk + rows, lo:lo + pgrp]
        inv = 1.0 / jnp.minimum(pos + 1, w).astype(F32)
        diff = acc * inv - cur
        pool_parts.append(_mm(diff, wpool_ref[kk], precise))
    pool_out = jnp.concatenate(pool_parts, axis=-1) * pscale_ref[...]
    tail = cat_scr[rows:rows + hist, :]
    cat_scr[0:hist, :] = tail
    buf_ref[...] = tail.reshape(POOL_HIST, bblk, d_pool)

    if precise:
        mix = jnp.concatenate([ssm_out, pool_out], axis=-1)
    else:
        mix = jnp.concatenate([ssm_out.astype(BF16), pool_out.astype(BF16)], axis=-1)
    x1 = x + _mm(mix, wout_ref[...], precise)
    x1_ref[...] = x1
    ms2 = jnp.mean(x1 * x1, axis=-1, keepdims=True)
    t = x1 * lax.rsqrt(ms2 + EPS) * gffn_ref[...]

    t_hi = t.astype(BF16)
    t_lo = (t - t_hi.astype(F32)).astype(BF16)
    w_r = wr_ref[...]
    w_hi = w_r.astype(BF16)
    w_lo = (w_r - w_hi.astype(F32)).astype(BF16)
    logits = _dot(t_hi, w_hi) + _dot(t_lo, w_hi) + _dot(t_hi, w_lo) + br_ref[...]
    lane = lax.broadcasted_iota(jnp.int32, (rows, LANES), 1)
    big = jnp.int32(4 * LANES)
    lg = jnp.where(lane < n_expert_groups, logits, NEG)
    mg = jnp.max(lg, axis=-1, keepdims=True)
    gsel = jnp.min(jnp.where(lg == mg, lane, big), axis=-1, keepdims=True)
    p_sel = 1.0 / jnp.sum(jnp.exp(lg - mg), axis=-1, keepdims=True)
    e_lo = n_expert_groups + gsel * experts_per_group
    le = jnp.where((lane >= e_lo) & (lane < e_lo + experts_per_group), logits, NEG)
    v1 = jnp.max(le, axis=-1, keepdims=True)
    i1 = jnp.min(jnp.where(le == v1, lane, big), axis=-1, keepdims=True)
    le2 = jnp.where(lane == i1, NEG, le)
    v2 = jnp.max(le2, axis=-1, keepdims=True)
    i2 = jnp.min(jnp.where(le2 == v2, lane, big), axis=-1, keepdims=True)
    ex = jnp.exp(v2 - v1)
    q1 = p_sel / (1.0 + ex)
    q2 = q1 * ex
    e1 = (i1 - n_expert_groups).astype(F32)
    e2 = (i2 - n_expert_groups).astype(F32)
    route = jnp.where(lane == 0, e1, jnp.where(lane == 1, e2, jnp.where(lane == 2, q1,
                      jnp.where(lane == 3, q2, 0.0))))

    lf = lax.broadcasted_iota(jnp.int32, (MOE_TILE, LANES), 1).astype(F32)
    n_local = xloc_ref.shape[1]
    before_e = (lax.broadcasted_iota(jnp.int32, (LANES, LANES), 0)
                < lax.broadcasted_iota(jnp.int32, (LANES, LANES), 1)).astype(BF16)
    earlier = (lax.broadcasted_iota(jnp.int32, (MOE_TILE, MOE_TILE), 0)
               > lax.broadcasted_iota(jnp.int32, (MOE_TILE, MOE_TILE), 1)).astype(BF16)
    local_row = lax.broadcasted_iota(jnp.int32, (n_local, MOE_TILE), 0)
    route_parts = []
    for hh in range(rows // MOE_TILE):
        sl = slice(hh * MOE_TILE, (hh + 1) * MOE_TILE)
        rt = route[sl]
        oh1 = lf == rt[:, 0:1]
        oh2 = lf == rt[:, 1:2]
        oh = jnp.logical_or(oh1, oh2).astype(F32)
        cnt = jnp.sum(oh, axis=0, keepdims=True)
        run_chunks = jnp.floor((cnt + (SUBLANES - 1)) * (1.0 / SUBLANES))
        loff = SUBLANES * _dot(jnp.broadcast_to(run_chunks, (SUBLANES, LANES)).astype(BF16), before_e)[0:1]
        where_to = loff + _dot(earlier, oh.astype(BF16))
        lpos1 = jnp.sum(jnp.where(oh1, where_to, 0.0), axis=-1, keepdims=True)
        lpos2 = jnp.sum(jnp.where(oh2, where_to, 0.0), axis=-1, keepdims=True)
        lpos_t = jnp.where(lf == 0.0, lpos1, jnp.where(lf == 1.0, lpos2, 0.0)).T
        perm = jnp.logical_or(local_row == lpos_t[0:1].astype(jnp.int32),
                              local_row == lpos_t[1:2].astype(jnp.int32)).astype(BF16)
        xloc_ref[hh] = _pack_halves(_dot(perm, t_hi[sl]))
        cnt_ref[hh] = cnt
        route_parts.append(jnp.where(lf == 4.0, lpos1, jnp.where(lf == 5.0, lpos2, rt)))
    route_ref[...] = jnp.concatenate(route_parts, axis=0)


def _mixer_kernel(*refs, cfg, geo, snapshot):
    x_ref, h0s_ref, buf0s_ref = refs[:3]
    w_refs = refs[3:3 + N_MIXER_WEIGHTS]
    rest = refs[3 + N_MIXER_WEIGHTS:]
    x1_ref, route_ref, xloc_ref, cnt_ref, hp_ref, bufp_ref, hs_ref, bufs_ref = rest[:8]
    rest = rest[8:]
    if snapshot:
        hsnap_ref, bsnap_ref = rest[:2]
        rest = rest[2:]
    hm, bufm, u_m, y_m, cat_m, u_p, y_p, cat_p, u_s, y_s, cat_s = rest
    outs = (x1_ref, route_ref, xloc_ref, cnt_ref)
    s = pl.program_id(0)
    n_p = geo["n_p_steps"]
    batch = geo["batch"]

    @pl.when(s == 0)
    def _meta():
        _stream_step(x_ref, w_refs, lambda: jnp.zeros(hm.shape, F32), lambda: jnp.zeros(bufm.shape, F32), True, 0,
                     *outs, hm, bufm, u_m, y_m, cat_m, bblk=geo["m_batch"], tile_l=geo["n_meta"], cfg=cfg,
                     precise=False)

    @pl.when(jnp.logical_and(s >= 1, s <= n_p))
    def _prompt():
        if snapshot:
            @pl.when(s == n_p)
            def _():
                hsnap_ref[...] = hp_ref[...]
                bsnap_ref[...] = bufp_ref[...]
        _stream_step(x_ref, w_refs, lambda: hm[:, 0:batch, :], lambda: bufm[:, 0:batch, :], s == 1,
                     geo["n_meta"] + (s - 1) * geo["p_tile_l"], *outs, hp_ref, bufp_ref, u_p, y_p, cat_p,
                     bblk=batch, tile_l=geo["p_tile_l"], cfg=cfg, precise=False)

    @pl.when(s > n_p)
    def _sample():
        _stream_step(x_ref, w_refs, lambda: h0s_ref[...], lambda: buf0s_ref[...], True, PAST_LEN,
                     *outs, hs_ref, bufs_ref, u_s, y_s, cat_s, bblk=geo["s_bblk"], tile_l=geo["dec_seq"], cfg=cfg,
                     precise=False)


def _mixer_weights(lw):
    return [lw["g_mix"], lw["w_in"], lw["mq"], lw["pmat"], lw["a8"], lw["ssm_d"], lw["w_glu"], lw["b_glu"],
            lw["w_pool"], lw["pool_scale"], lw["w_out"], lw["g_ffn"], lw["w_r"], lw["b_r"]]


def _stream_scratch(tile_l, bblk, d_model, d_ssm):
    return [pltpu.VMEM((tile_l // CHUNK, CHUNK, bblk, d_model), F32),
            pltpu.VMEM((tile_l // CHUNK, CHUNK, bblk, d_ssm), F32),
            pltpu.VMEM(((POOL_HIST + tile_l) * bblk, d_model - d_ssm), F32)]


def _mixer_call(x_flat, h0s, buf0s, lw, *, geo, cfg, snapshot):
    total, d_model = x_flat.shape
    groups, dec_batch, _ = h0s.shape
    d_pool = buf0s.shape[-1]
    d_ssm = d_model - d_pool
    g = geo
    n_p, n_s = g["n_p_steps"], g["n_s_steps"]
    rows = g["step_rows"]
    per_step = rows // MOE_TILE
    n_local = g["n_local"]
    blk = lambda s: jnp.where(s == 0, n_p + n_s, s - 1)
    sblk = lambda s: jnp.clip(s - (n_p + 1), 0, n_s - 1)
    weights = _mixer_weights(lw)

    def full(a):
        nd = a.ndim
        return pl.BlockSpec(a.shape, lambda s, _n=nd: (0,) * _n)

    h_s_spec = pl.BlockSpec((groups, g["s_bblk"], LANES), lambda s: (0, sblk(s), 0))
    buf_s_spec = pl.BlockSpec((POOL_HIST, g["s_bblk"], d_pool), lambda s: (0, sblk(s), 0))
    in_specs = [pl.BlockSpec((rows, d_model), lambda s: (blk(s), 0)), h_s_spec, buf_s_spec] + [full(w) for w in weights]
    h_p_shape, buf_p_shape = (groups, g["batch"], LANES), (POOL_HIST, g["batch"], d_pool)
    out_specs = [pl.BlockSpec((rows, d_model), lambda s: (blk(s), 0)),
                 pl.BlockSpec((rows, LANES), lambda s: (blk(s), 0)),
                 pl.BlockSpec((per_step, n_local, d_model // 2), lambda s: (blk(s), 0, 0)),
                 pl.BlockSpec((per_step, 1, LANES), lambda s: (blk(s), 0, 0)),
                 pl.BlockSpec(h_p_shape, lambda s: (0, 0, 0)), pl.BlockSpec(buf_p_shape, lambda s: (0, 0, 0)),
                 h_s_spec, buf_s_spec]
    out_shape = [jax.ShapeDtypeStruct((total, d_model), F32), jax.ShapeDtypeStruct((total, LANES), F32),
                 jax.ShapeDtypeStruct((total // MOE_TILE, n_local, d_model // 2), jnp.uint32),
                 jax.ShapeDtypeStruct((total // MOE_TILE, 1, LANES), F32),
                 jax.ShapeDtypeStruct(h_p_shape, F32), jax.ShapeDtypeStruct(buf_p_shape, F32),
                 jax.ShapeDtypeStruct((groups, dec_batch, LANES), F32),
                 jax.ShapeDtypeStruct((POOL_HIST, dec_batch, d_pool), F32)]
    if snapshot:
        out_specs += [pl.BlockSpec(h_p_shape, lambda s: (0, 0, 0)), pl.BlockSpec(buf_p_shape, lambda s: (0, 0, 0))]
        out_shape += [jax.ShapeDtypeStruct(h_p_shape, F32), jax.ShapeDtypeStruct(buf_p_shape, F32)]
    scratch = ([pltpu.VMEM((groups, g["m_batch"], LANES), F32), pltpu.VMEM((POOL_HIST, g["m_batch"], d_pool), F32)]
               + _stream_scratch(g["n_meta"], g["m_batch"], d_model, d_ssm)
               + _stream_scratch(g["p_tile_l"], g["batch"], d_model, d_ssm)
               + _stream_scratch(g["dec_seq"], g["s_bblk"], d_model, d_ssm))
    return pl.pallas_call(
        functools.partial(_mixer_kernel, cfg=cfg, geo=geo, snapshot=snapshot),
        grid=(1 + n_p + n_s,),
        in_specs=in_specs,
        out_specs=out_specs,
        out_shape=out_shape,
        scratch_shapes=scratch,
        compiler_params=pltpu.CompilerParams(dimension_semantics=("arbitrary",), vmem_limit_bytes=VMEM_LIMIT),
        name="mixer",
    )(x_flat, h0s, buf0s, *weights)


def _mixer_tail_kernel(*refs, cfg, geo):
    x_ref, h0_ref, buf0_ref = refs[:3]
    w_refs = refs[3:3 + N_MIXER_WEIGHTS]
    rest = refs[3 + N_MIXER_WEIGHTS + 4:]
    x1_ref, route_ref, xloc_ref, cnt_ref, h_ref, buf_ref, u_scr, y_scr, cat_scr = rest
    pos = geo["n_meta"] + (geo["n_p_steps"] - 1) * geo["p_tile_l"]
    _stream_step(x_ref, w_refs, lambda: h0_ref[...], lambda: buf0_ref[...], True, pos, x1_ref, route_ref, xloc_ref,
                 cnt_ref, h_ref, buf_ref, u_scr, y_scr, cat_scr, bblk=geo["batch"], tile_l=geo["p_tile_l"], cfg=cfg,
                 precise=True)


def _mixer_tail_call(x_flat, h0, buf0, lw32, x1, route, xloc, cnt, *, geo, cfg):
    total, d_model = x_flat.shape
    groups = h0.shape[0]
    d_pool = buf0.shape[-1]
    g = geo
    rows = g["step_rows"]
    per_step = rows // MOE_TILE
    last = g["n_p_steps"] - 1
    weights = _mixer_weights(lw32)

    def full(a):
        nd = a.ndim
        return pl.BlockSpec(a.shape, lambda s, _n=nd: (0,) * _n)

    n_in = 3 + len(weights)
    return pl.pallas_call(
        functools.partial(_mixer_tail_kernel, cfg=cfg, geo=geo),
        grid=(1,),
        in_specs=([pl.BlockSpec((rows, d_model), lambda s: (last, 0)), full(h0), full(buf0)]
                  + [full(w) for w in weights] + [pl.BlockSpec(memory_space=pl.ANY)] * 4),
        out_specs=[pl.BlockSpec((rows, d_model), lambda s: (last, 0)),
                   pl.BlockSpec((rows, LANES), lambda s: (last, 0)),
                   pl.BlockSpec((per_step, g["n_local"], d_model // 2), lambda s: (last, 0, 0)),
                   pl.BlockSpec((per_step, 1, LANES), lambda s: (last, 0, 0)),
                   full(h0), full(buf0)],
        out_shape=[jax.ShapeDtypeStruct(a.shape, a.dtype) for a in (x1, route, xloc, cnt, h0, buf0)],
        scratch_shapes=_stream_scratch(g["p_tile_l"], g["batch"], d_model, d_model - d_pool),
        input_output_aliases={n_in + k: k for k in range(4)},
        compiler_params=pltpu.CompilerParams(dimension_semantics=("arbitrary",), vmem_limit_bytes=VMEM_LIMIT),
        name="mixer_tail",
    )(x_flat, h0, buf0, *weights, x1, route, xloc, cnt)


def _chunk_copy(src_ref, src_row, dst_ref, dst_row, sem):
    src = src_ref.at[pl.ds(pl.multiple_of(src_row, SUBLANES), SUBLANES), :]
    dst = dst_ref.at[pl.ds(pl.multiple_of(dst_row, SUBLANES), SUBLANES), :]
    return pltpu.make_async_copy(src, dst, sem)


def _experts_gather_kernel(exp_ref, nact_ref, src_ref, nval_ref, xl_ref, wg_ref, wu_ref, wd_ref, ys_ref,
                           xbuf, wgu_scr, wd_scr, sem):
    j = pl.program_id(0)
    f = wg_ref.shape[-1]
    chunks_per_tile = EXPERT_TILE // SUBLANES
    nact = nact_ref[0]

    def for_tile_chunks(tile, slot, op):
        def body(k, carry):
            src = xl_ref.at[pl.ds(pl.multiple_of(src_ref[tile * chunks_per_tile + k], SUBLANES), SUBLANES), :]
            dst = xbuf.at[slot, pl.ds(pl.multiple_of(k * SUBLANES, SUBLANES), SUBLANES), :]
            op(pltpu.make_async_copy(src, dst, sem.at[slot]))
            return carry

        lax.fori_loop(0, nval_ref[tile], body, 0)

    @pl.when(j == 0)
    def _():
        xbuf[...] = jnp.zeros_like(xbuf)
        for_tile_chunks(0, 0, lambda cp: cp.start())

    @pl.when(j + 1 < nact)
    def _():
        for_tile_chunks(j + 1, (j + 1) % 2, lambda cp: cp.start())

    active = j < nact
    changed = jnp.logical_or(j == 0, exp_ref[j] != exp_ref[jnp.maximum(j - 1, 0)])

    @pl.when(jnp.logical_and(active, changed))
    def _():
        wgu_scr[:, 0:f] = wg_ref[...].astype(BF16)
        wgu_scr[:, f:2 * f] = wu_ref[...].astype(BF16)
        wd_scr[...] = wd_ref[...].astype(BF16)

    @pl.when(active)
    def _():
        for_tile_chunks(j, j % 2, lambda cp: cp.wait())
        half = wgu_scr.shape[0] // 2
        t_lo, t_hi = _unpack_halves(xbuf[j % 2])
        gu = _dot(t_lo, wgu_scr[0:half, :]) + _dot(t_hi, wgu_scr[half:2 * half, :])
        act = jax.nn.silu(gu[:, 0:f]) * gu[:, f:2 * f]
        ys_ref[...] = _pack_halves(_dot(act.astype(BF16), wd_scr[...]))

    @pl.when(jnp.logical_not(active))
    def _():
        ys_ref[...] = jnp.zeros_like(ys_ref)


def _experts_gather_call(tile_exp, n_active, src, nval, xloc, w_gate, w_up, w_down, layer, n_tiles):
    _, n_exp, d_model, f = w_gate.shape
    xl_flat = xloc.reshape(-1, xloc.shape[-1])
    wspec = lambda a, b: pl.BlockSpec((None, None, a, b), lambda i, ex, *_: (layer, ex[i], 0, 0))
    return pl.pallas_call(
        _experts_gather_kernel,
        grid_spec=pltpu.PrefetchScalarGridSpec(
            num_scalar_prefetch=4,
            grid=(n_tiles,),
            in_specs=[pl.BlockSpec(memory_space=pl.ANY), wspec(d_model, f), wspec(d_model, f), wspec(f, d_model)],
            out_specs=pl.BlockSpec((EXPERT_TILE, d_model // 2), lambda i, *_: (i, 0)),
            scratch_shapes=[pltpu.VMEM((2, EXPERT_TILE, d_model // 2), jnp.uint32),
                            pltpu.VMEM((d_model, 2 * f), BF16), pltpu.VMEM((f, d_model), BF16),
                            pltpu.SemaphoreType.DMA((2,))],
        ),
        out_shape=jax.ShapeDtypeStruct((n_tiles * EXPERT_TILE, d_model // 2), jnp.uint32),
        compiler_params=pltpu.CompilerParams(dimension_semantics=("arbitrary",), vmem_limit_bytes=VMEM_LIMIT),
        name="moe_experts",
    )(tile_exp, n_active, src, nval, xl_flat, w_gate, w_up, w_down)


def _combine2_kernel(cmap_ref, nch_ref, x1_ref, route_ref, gfin_ref, ys_ref, out_ref, yloc, sem,
                     *, final_norm, n_chunks):
    step = pl.program_id(0)
    base = step * n_chunks

    @pl.when(step == 0)
    def _():
        yloc[...] = jnp.zeros_like(yloc)

    def chunk(c):
        return _chunk_copy(ys_ref, cmap_ref[base + c], yloc, c * SUBLANES, sem)

    def issue(c, carry):
        chunk(c).start()
        return carry

    def drain(c, carry):
        chunk(c).wait()
        return carry

    lax.fori_loop(0, nch_ref[step], issue, 0)
    route = route_ref[...]
    lio = lax.broadcasted_iota(jnp.int32, (route.shape[0], yloc.shape[0]), 1)
    lpos1 = route[:, 4:5].astype(jnp.int32)
    lpos2 = route[:, 5:6].astype(jnp.int32)
    wperm = jnp.where(lio == lpos1, route[:, 2:3], jnp.where(lio == lpos2, route[:, 3:4], 0.0)).astype(BF16)
    lax.fori_loop(0, nch_ref[step], drain, 0)
    y_lo, y_hi = _unpack_halves(yloc[...])
    x2 = x1_ref[...] + jnp.concatenate([_dot(wperm, y_lo), _dot(wperm, y_hi)], axis=-1)
    if final_norm:
        ms = jnp.mean(x2 * x2, axis=-1, keepdims=True)
        x2 = x2 * lax.rsqrt(ms + EPS) * gfin_ref[...]
    out_ref[...] = x2


def _combine2_call(cmap, nch, x1, route, g_final, ys, *, final_norm, n_chunks):
    total, d_model = x1.shape
    return pl.pallas_call(
        functools.partial(_combine2_kernel, final_norm=final_norm, n_chunks=n_chunks),
        grid_spec=pltpu.PrefetchScalarGridSpec(
            num_scalar_prefetch=2,
            grid=(total // MOE_TILE,),
            in_specs=[pl.BlockSpec((MOE_TILE, d_model), lambda i, *_: (i, 0)),
                      pl.BlockSpec((MOE_TILE, LANES), lambda i, *_: (i, 0)),
                      pl.BlockSpec((1, d_model), lambda i, *_: (0, 0)),
                      pl.BlockSpec(memory_space=pl.ANY)],
            out_specs=pl.BlockSpec((MOE_TILE, d_model), lambda i, *_: (i, 0)),
            scratch_shapes=[pltpu.VMEM((n_chunks * SUBLANES, d_model // 2), jnp.uint32), pltpu.SemaphoreType.DMA],
        ),
        out_shape=jax.ShapeDtypeStruct((total, d_model), F32),
        input_output_aliases={2: 0},
        compiler_params=pltpu.CompilerParams(dimension_semantics=("arbitrary",), vmem_limit_bytes=VMEM_LIMIT),
        name="moe_combine",
    )(cmap, nch, x1, route, g_final, ys)


def _moe2(x1, route, xloc, cnt, w_gate, w_up, w_down, g_final, *, layer, final_norm):
    total, d_model = x1.shape
    n_exp = w_gate.shape[1]
    nt, n_local, _ = xloc.shape
    cpt = EXPERT_TILE // SUBLANES
    n_chunks = n_local // SUBLANES
    counts = cnt[:, 0, :n_exp].astype(jnp.int32)
    run = (counts + SUBLANES - 1) // SUBLANES
    lend = jnp.cumsum(run, axis=1)
    loff = lend - run
    gend = jnp.cumsum(run, axis=0)
    gcum = gend - run
    seg = gend[-1]
    seg_tiles = (seg + cpt - 1) // cpt
    seg_end = jnp.cumsum(seg_tiles)
    seg_off = seg_end - seg_tiles
    n_tiles = (2 * total + nt * n_exp * (SUBLANES - 1) + n_exp * (EXPERT_TILE - 1)) // EXPERT_TILE + 1
    n_active = seg_end[-1].reshape(1).astype(jnp.int32)
    tile_idx = jnp.minimum(jnp.arange(n_tiles, dtype=jnp.int32), n_active - 1)
    tile_exp = jnp.minimum(jnp.sum((tile_idx[:, None] >= seg_end[None, :]).astype(jnp.int32), axis=1), n_exp - 1)
    is_e = tile_exp[:, None] == jnp.arange(n_exp, dtype=jnp.int32)[None, :]
    pick = lambda table: jnp.sum(jnp.where(is_e[:, None, :], table[None, :, :], 0), axis=2)
    pick1 = lambda vec: jnp.sum(jnp.where(is_e, vec[None, :], 0), axis=1)
    g = ((jnp.arange(n_tiles, dtype=jnp.int32) - pick1(seg_off)) * cpt)[:, None] + jnp.arange(cpt, dtype=jnp.int32)
    gend_t, gcum_t, loff_t = pick(gend), pick(gcum), pick(loff)
    owner = jnp.logical_and(g[:, :, None] >= gcum_t[:, None, :], g[:, :, None] < gend_t[:, None, :])
    tile_base = jnp.arange(nt, dtype=jnp.int32) * n_chunks
    src = jnp.sum(jnp.where(owner, (tile_base[None, :] + loff_t - gcum_t)[:, None, :] + g[:, :, None], 0), axis=2)
    src = (src * SUBLANES).reshape(-1)
    nval = jnp.clip(pick1(seg) - g[:, 0], 0, cpt)
    chunk_id = jnp.arange(n_chunks, dtype=jnp.int32)
    in_run = jnp.logical_and(chunk_id[None, :, None] >= loff[:, None, :], chunk_id[None, :, None] < lend[:, None, :])
    gchunk = (seg_off * cpt)[None, :] + gcum - loff
    cmap = ((chunk_id[None, :] + jnp.sum(jnp.where(in_run, gchunk[:, None, :], 0), axis=2)) * SUBLANES).reshape(-1)
    nch = lend[:, -1]
    ys = _experts_gather_call(tile_exp, n_active, src, nval, xloc, w_gate, w_up, w_down, layer, n_tiles)
    return _combine2_call(cmap, nch, x1, route, g_final, ys, final_norm=final_norm, n_chunks=n_chunks)


def _to_rows_kernel(xp_ref, xs_ref, meta_ref, o_ref, *, n_p_steps, n_s_steps):
    i = pl.program_id(0)
    pb, pl_ = xp_ref.shape[0], xp_ref.shape[1]
    sb, sl = xs_ref.shape[0], xs_ref.shape[1]
    ml = meta_ref.shape[0]
    mb = o_ref.shape[0] // ml

    @pl.when(i < n_p_steps)
    def _():
        for l in range(pl_):
            o_ref[l * pb:(l + 1) * pb, :] = xp_ref[:, l, :]

    @pl.when(jnp.logical_and(i >= n_p_steps, i < n_p_steps + n_s_steps))
    def _():
        for l in range(sl):
            o_ref[l * sb:(l + 1) * sb, :] = xs_ref[:, l, :]

    @pl.when(i >= n_p_steps + n_s_steps)
    def _():
        for l in range(ml):
            o_ref[l * mb:(l + 1) * mb, :] = jnp.broadcast_to(meta_ref[l:l + 1, :], (mb, o_ref.shape[1]))


def _to_rows_call(x_prompt, x_sample, meta, *, step_rows):
    batch, seq, d_model = x_prompt.shape
    dec_batch, dec_seq, _ = x_sample.shape
    p_tile_l, s_bblk = step_rows // batch, step_rows // dec_seq
    n_p_steps, n_s_steps = seq // p_tile_l, dec_batch // s_bblk
    steps = n_p_steps + n_s_steps + 1
    return pl.pallas_call(
        functools.partial(_to_rows_kernel, n_p_steps=n_p_steps, n_s_steps=n_s_steps),
        grid=(steps,),
        in_specs=[pl.BlockSpec((batch, p_tile_l, d_model), lambda i: (0, jnp.minimum(i, n_p_steps - 1), 0)),
                  pl.BlockSpec((s_bblk, dec_seq, d_model),
                               lambda i: (jnp.clip(i - n_p_steps, 0, n_s_steps - 1), 0, 0)),
                  pl.BlockSpec(meta.shape, lambda i: (0, 0))],
        out_specs=pl.BlockSpec((step_rows, d_model), lambda i: (i, 0)),
        out_shape=jax.ShapeDtypeStruct((steps * step_rows, d_model), F32),
        compiler_params=pltpu.CompilerParams(dimension_semantics=("arbitrary",)),
        name="to_rows",
    )(x_prompt, x_sample, meta)


def _from_rows_kernel(y_ref, yp_ref, ys_ref, *, n_p_steps):
    i = pl.program_id(0)
    pb, pl_ = yp_ref.shape[0], yp_ref.shape[1]
    sb, sl = ys_ref.shape[0], ys_ref.shape[1]

    @pl.when(i < n_p_steps)
    def _():
        for l in range(pl_):
            yp_ref[:, l, :] = y_ref[l * pb:(l + 1) * pb, :]

    @pl.when(i >= n_p_steps)
    def _():
        for l in range(sl):
            ys_ref[:, l, :] = y_ref[l * sb:(l + 1) * sb, :]


def _from_rows_call(y_flat, batch, seq, dec_batch, dec_seq, *, step_rows):
    d_model = y_flat.shape[-1]
    p_tile_l, s_bblk = step_rows // batch, step_rows // dec_seq
    n_p_steps, n_s_steps = seq // p_tile_l, dec_batch // s_bblk
    return pl.pallas_call(
        functools.partial(_from_rows_kernel, n_p_steps=n_p_steps),
        grid=(n_p_steps + n_s_steps,),
        in_specs=[pl.BlockSpec((step_rows, d_model), lambda i: (i, 0))],
        out_specs=[pl.BlockSpec((batch, p_tile_l, d_model), lambda i: (0, jnp.minimum(i, n_p_steps - 1), 0)),
                   pl.BlockSpec((s_bblk, dec_seq, d_model),
                                lambda i: (jnp.clip(i - n_p_steps, 0, n_s_steps - 1), 0, 0))],
        out_shape=[jax.ShapeDtypeStruct((batch, seq, d_model), F32),
                   jax.ShapeDtypeStruct((dec_batch, dec_seq, d_model), F32)],
        compiler_params=pltpu.CompilerParams(dimension_semantics=("arbitrary",)),
        name="from_rows",
    )(y_flat)


def kernel(x_prompt, x_sample, state_ssm_re, state_ssm_im, state_pool, meta_tokens, norm_mix_g, w_in, ssm_a_re,
           ssm_a_im, ssm_log_dt, ssm_b_re, ssm_b_im, ssm_c_re, ssm_c_im, ssm_d, w_glu, b_glu, w_pool, pool_scale,
           w_out, norm_ffn_g, w_router_group, b_router_group, w_router_expert, b_router_expert, w_gate, w_up,
           w_down, norm_final_g):
    batch, seq, d_model = x_prompt.shape
    dec_batch, dec_seq, _ = x_sample.shape
    depth, _, groups, nstate = state_ssm_re.shape
    n_meta = meta_tokens.shape[0]
    pool_buf, d_pool = state_pool.shape[2], state_pool.shape[3]
    n_pool_groups = w_pool.shape[1]
    windows = tuple(2 ** (k + 1) for k in range(n_pool_groups))
    assert pool_buf == max(windows) - 1 == POOL_HIST - 1
    n_expert_groups = w_router_group.shape[-1]
    n_experts = w_router_expert.shape[-1]
    cfg = dict(windows=windows, n_expert_groups=n_expert_groups, experts_per_group=n_experts // n_expert_groups)

    step_rows = 512
    p_tile_l, s_bblk, m_batch = step_rows // batch, step_rows // dec_seq, step_rows // n_meta
    n_p, n_s, n_m = batch * seq, dec_batch * dec_seq, m_batch * n_meta
    assert m_batch >= batch and seq % p_tile_l == 0 and dec_batch % s_bblk == 0
    total = n_p + n_s + n_m
    assert total % step_rows == 0 and step_rows % MOE_TILE == 0
    n_local = 2 * MOE_TILE + n_experts * (SUBLANES - 1)
    n_local = (n_local + LANES - 1) // LANES * LANES
    geo = dict(batch=batch, seq=seq, dec_batch=dec_batch, dec_seq=dec_seq, n_meta=n_meta, step_rows=step_rows,
               p_tile_l=p_tile_l, s_bblk=s_bblk, m_batch=m_batch, n_p_steps=seq // p_tile_l,
               n_s_steps=dec_batch // s_bblk, n_local=n_local)

    mq, pmat, a8 = _s5_prepare(ssm_a_re, ssm_a_im, ssm_log_dt, ssm_b_re, ssm_b_im, ssm_c_re, ssm_c_im)

    x = _to_rows_call(x_prompt.astype(F32), x_sample.astype(F32), meta_tokens.astype(F32), step_rows=step_rows)
    assert x.shape[0] == total

    row2 = lambda v: v.reshape(1, -1).astype(F32)
    outs = {k: [] for k in ("hp", "bp", "hs", "bs")}
    for l in range(depth):
        w_r = jnp.concatenate([w_router_group[l], w_router_expert[l]], axis=1).astype(F32)
        w_r = jnp.pad(w_r, ((0, 0), (0, LANES - w_r.shape[1])))
        b_r = jnp.concatenate([b_router_group[l], b_router_expert[l]]).astype(F32)
        lw32 = dict(g_mix=row2(norm_mix_g[l]), w_in=w_in[l].astype(F32), mq=mq[l], pmat=pmat[l], a8=a8[l],
                    ssm_d=row2(ssm_d[l]), w_glu=w_glu[l].astype(F32), b_glu=row2(b_glu[l]),
                    w_pool=w_pool[l].astype(F32), pool_scale=row2(pool_scale[l]), w_out=w_out[l].astype(F32),
                    g_ffn=row2(norm_ffn_g[l]), w_r=w_r,
                    b_r=jnp.pad(b_r, (0, LANES - b_r.shape[0])).reshape(1, LANES))
        lw = dict(lw32, **{k: lw32[k].astype(BF16) for k in ("w_in", "mq", "pmat", "w_glu", "w_pool", "w_out")})

        h0_s = jnp.concatenate([state_ssm_re[l], state_ssm_im[l]], axis=-1).transpose(1, 0, 2).astype(F32)
        buf0_s = jnp.pad(state_pool[l].astype(F32).transpose(1, 0, 2), ((1, 0), (0, 0), (0, 0)))
        redo_last = l < depth - 1
        res = _mixer_call(x, h0_s, buf0_s, lw, geo=geo, cfg=cfg, snapshot=redo_last)
        x1, route, xloc, cnt, h_p, buf_p, h_s, buf_s = res[:8]
        if redo_last:
            x1, route, xloc, cnt, h_p, buf_p = _mixer_tail_call(x, res[8], res[9], lw32, x1, route, xloc, cnt,
                                                                 geo=geo, cfg=cfg)
        outs["hp"].append(h_p)
        outs["bp"].append(buf_p)
        outs["hs"].append(h_s)
        outs["bs"].append(buf_s)
        x = _moe2(x1, route, xloc, cnt, w_gate, w_up, w_down, row2(norm_final_g), layer=l,
                  final_norm=(l == depth - 1))

    y_prompt, y_sample = _from_rows_call(x, batch, seq, dec_batch, dec_seq, step_rows=step_rows)
    st = lambda hs: jnp.stack(hs).transpose(0, 2, 1, 3)
    pl_out = lambda bs: jnp.stack(bs)[:, 1:].transpose(0, 2, 1, 3)
    hp, hs = st(outs["hp"]), st(outs["hs"])
    return (y_prompt, y_sample, hp[..., :nstate], hp[..., nstate:], pl_out(outs["bp"]),
            hs[..., :nstate], hs[..., nstate:], pl_out(outs["bs"]))
```

```python
import functools

import numpy as np
import jax
import jax.numpy as jnp
from jax import lax
from jax.experimental import pallas as pl
from jax.experimental.pallas import tpu as pltpu

F32 = jnp.float32
BF16 = jnp.bfloat16

LANES = 128
SUBLANES = 8
CHUNK = 8
POOL_HIST = 16
MOE_TILE = 256
EXPERT_TILE = 256
VMEM_LIMIT = 56 * 1024 * 1024
EPS = 1e-6
NEG = -1e30
PAST_LEN = 16384


def _dot(a, b):
    return jnp.dot(a, b, preferred_element_type=F32)


def _mm(a, w, precise):
    if precise:
        return jnp.dot(a, w, precision=lax.Precision.HIGHEST, preferred_element_type=F32)
    return _dot(a.astype(BF16), w)


def _pack_halves(x):
    k = x.shape[-1] // 2
    return pltpu.pack_elementwise([x[:, :k], x[:, k:]], packed_dtype=BF16)


def _unpack_halves(p):
    lo = pltpu.unpack_elementwise(p, index=0, packed_dtype=BF16, unpacked_dtype=F32)
    hi = pltpu.unpack_elementwise(p, index=1, packed_dtype=BF16, unpacked_dtype=F32)
    return lo.astype(BF16), hi.astype(BF16)


def _dot_nt_f32(a, b):
    return lax.dot_general(a, b, (((1,), (1,)), ((), ())), precision=lax.Precision.HIGHEST,
                           preferred_element_type=F32)


def _s5_prep_kernel(a_re_ref, a_im_ref, ldt_ref, bt_ref, btx_ref, c_ref, cx_ref, mq_ref, p_ref, a8_ref):
    ch = bt_ref.shape[1]
    first_half = lax.broadcasted_iota(jnp.int32, (1, LANES), 1) < LANES // 2
    sgn = jnp.where(first_half, 1.0, -1.0)
    for s in range(CHUNK):
        a_re = a_re_ref[s]
        a_im = a_im_ref[s]
        dt = jnp.exp(ldt_ref[s])
        mag = jnp.exp(dt * a_re)
        abr = mag * jnp.cos(dt * a_im)
        abi = mag * jnp.sin(dt * a_im)
        den = a_re * a_re + a_im * a_im
        nr = abr - 1.0
        f_re = (nr * a_re + abi * a_im) / den
        f_im = (abi * a_re - nr * a_im) / den
        bb = f_re * bt_ref[s] - sgn * f_im * btx_ref[s]
        cc = c_ref[s] * sgn
        ccx = -cx_ref[s]
        bbx = f_re * btx_ref[s] + sgn * f_im * bt_ref[s]
        pr = jnp.ones_like(a_re)
        pi = jnp.zeros_like(a_re)
        e2, w2 = [], []
        for k in range(CHUNK + 1):
            e2.append(pr * cc + pi * ccx)
            w2.append(pr * bb - sgn * pi * bbx)
            if k < CHUNK:
                pr, pi = pr * abr - pi * abi, pr * abi + pi * abr
        a8_ref[s, 0:1, :] = pr
        a8_ref[s, 1:2, :] = -sgn * pi
        a8_ref[s, 2:3, :] = sgn * pi
        zero = jnp.zeros_like(bb)
        jpos = [(p - s) % CHUNK for p in range(CHUNK)]
        for p in range(CHUNK):
            j = jpos[p]
            taps = jnp.concatenate([e2[jp - j] if jp >= j else zero for jp in jpos], axis=0)
            rows = slice(p * ch, (p + 1) * ch)
            mq_ref[s, rows, 0:LANES] = _dot_nt_f32(bb, taps)
            mq_ref[s, rows, LANES:2 * LANES] = w2[CHUNK - 1 - j]
        pt = jnp.concatenate([e2[jp + 1] for jp in jpos], axis=0)
        p_ref[s] = pt.T


def _s5_prepare(a_re, a_im, log_dt, b_re, b_im, c_re, c_im):
    depth, groups, nstate = a_re.shape
    ch = b_re.shape[-1]
    dg = depth * groups
    assert ch * CHUNK == LANES and 2 * nstate == LANES and groups % CHUNK == 0
    dup = lambda x: jnp.concatenate([x, x], axis=-1).reshape(dg, 1, LANES).astype(F32)
    btr = jnp.swapaxes(b_re.reshape(dg, nstate, ch), 1, 2).astype(F32)
    bti = jnp.swapaxes(b_im.reshape(dg, nstate, ch), 1, 2).astype(F32)
    cr = c_re.reshape(dg, ch, nstate).astype(F32)
    ci = c_im.reshape(dg, ch, nstate).astype(F32)
    pair = lambda x, y: jnp.concatenate([x, y], axis=-1)
    blk = lambda *shape: pl.BlockSpec((CHUNK,) + shape, lambda g: (g,) + (0,) * len(shape))
    mq, pmat, a8 = pl.pallas_call(
        _s5_prep_kernel,
        grid=(dg // CHUNK,),
        in_specs=[blk(1, LANES), blk(1, LANES), blk(1, 1)] + [blk(ch, LANES)] * 4,
        out_specs=[blk(LANES, 2 * LANES), blk(LANES, LANES), blk(3, LANES)],
        out_shape=[jax.ShapeDtypeStruct((dg, LANES, 2 * LANES), F32), jax.ShapeDtypeStruct((dg, LANES, LANES), F32),
                   jax.ShapeDtypeStruct((dg, 3, LANES), F32)],
        name="s5_prep",
    )(dup(a_re), dup(a_im), log_dt.reshape(dg, 1, 1).astype(F32), pair(btr, bti), pair(bti, btr),
      pair(cr, ci), pair(ci, cr))
    shp = lambda x: x.reshape((depth, groups) + x.shape[1:])
    return shp(mq), shp(pmat), shp(a8)


N_MIXER_WEIGHTS = 14


def _moe_combine(x1_ref, yl_ref, route_ref):
    rows = x1_ref.shape[0]
    n_local = yl_ref.shape[1]
    lio = lax.broadcasted_iota(jnp.int32, (MOE_TILE, n_local), 1)
    parts = []
    for hh in range(rows // MOE_TILE):
        r = route_ref[hh * MOE_TILE:(hh + 1) * MOE_TILE, :]
        wperm = jnp.where(lio == r[:, 4:5].astype(jnp.int32), r[:, 2:3],
                          jnp.where(lio == r[:, 5:6].astype(jnp.int32), r[:, 3:4], 0.0)).astype(BF16)
        y_lo, y_hi = _unpack_halves(yl_ref[hh])
        parts.append(jnp.concatenate([_dot(wperm, y_lo), _dot(wperm, y_hi)], axis=-1))
    return x1_ref[...] + jnp.concatenate(parts, axis=0)


def _stream_step(get_x, w_refs, h_init, buf_init, first, pos_base, x1_ref, route_ref, xloc_ref, cnt_ref,
                 h_ref, buf_ref, u_scr, y_scr, cat_scr, *, bblk, tile_l, cfg, precise):
    (gmix_ref, win_ref, mq_ref, p_ref, a8_ref, d_ref, wglu_ref, bglu_ref, wpool_ref, pscale_ref, wout_ref,
     gffn_ref, wr_ref, br_ref) = w_refs
    windows = cfg["windows"]
    n_expert_groups = cfg["n_expert_groups"]
    experts_per_group = cfg["experts_per_group"]
    rows = tile_l * bblk
    nk = tile_l // CHUNK
    n = nk * bblk
    d_model = x1_ref.shape[-1]
    d_ssm = y_scr.shape[-1]
    d_pool = d_model - d_ssm
    hist = POOL_HIST * bblk

    def _init():
        h_ref[...] = h_init()
        cat_scr[0:hist, :] = buf_init().reshape(hist, d_pool)

    if first is True:
        _init()
    else:
        pl.when(first)(_init)

    x = get_x()
    ms = jnp.mean(x * x, axis=-1, keepdims=True)
    h = x * lax.rsqrt(ms + EPS) * gmix_ref[...]
    u = _mm(h, win_ref[...], precise)
    u_scr[...] = u.reshape(nk, CHUNK, bblk, d_model)

    slot = lax.broadcasted_iota(jnp.int32, (n, LANES), 1) // (LANES // CHUNK)
    for v in range(d_ssm // LANES):
        rolled = []
        for j in range(CHUNK):
            uj = u_scr[:, j, :, v * LANES:(v + 1) * LANES].reshape(n, LANES)
            rolled.append(pltpu.roll(uj, (LANES // CHUNK) * j, axis=1) if j else uj)
        ys = []
        for s in range(CHUNK):
            g = v * CHUNK + s
            ug = rolled[0]
            for j in range(1, CHUNK):
                ug = jnp.where(slot == (s + j) % CHUNK, rolled[j], ug)
            yq = _mm(ug, mq_ref[g], precise)
            sg = yq[:, LANES:]
            sg_sw = pltpu.roll(sg, LANES // 2, axis=1)
            a8c = a8_ref[g, 0:1, :]
            a8s = a8_ref[g, 1:2, :]
            a8w = a8_ref[g, 2:3, :]
            hcur = h_ref[g]
            hsw = pltpu.roll(hcur, LANES // 2, axis=1)
            hins = []
            for k in range(nk):
                hins.append(hcur)
                sk = sg[k * bblk:(k + 1) * bblk]
                skw = sg_sw[k * bblk:(k + 1) * bblk]
                hcur, hsw = a8c * hcur + a8s * hsw + sk, a8c * hsw + a8w * hcur + skw
            h_ref[g] = hcur
            hin = jnp.concatenate(hins, axis=0) if nk > 1 else hins[0]
            ys.append(yq[:, :LANES] + _mm(hin, p_ref[g], precise))
        for j in range(CHUNK):
            z = ys[0]
            for s in range(1, CHUNK):
                z = jnp.where(slot == (s + j) % CHUNK, ys[s], z)
            if j:
                z = pltpu.roll(z, LANES - (LANES // CHUNK) * j, axis=1)
            y_scr[:, j, :, v * LANES:(v + 1) * LANES] = z.reshape(nk, bblk, LANES)

    u_ssm = u_scr[:, :, :, 0:d_ssm].reshape(rows, d_ssm)
    y = y_scr[...].reshape(rows, d_ssm) + d_ref[...] * u_ssm
    gl = jax.nn.gelu(y)
    ssm_out = gl * jax.nn.sigmoid(_mm(gl, wglu_ref[...], precise) + bglu_ref[...])

    up = u_scr[:, :, :, d_ssm:].reshape(rows, d_pool)
    cat_scr[hist:hist + rows, :] = up
    row_pos = lax.broadcasted_iota(jnp.int32, (rows, 1), 0) // bblk
    pos = pos_base + row_pos
    pool_parts = []
    pgrp = d_pool // len(windows)
    for kk, w in enumerate(windows):
        lo = kk * pgrp
        acc = cat_scr[hist:hist + rows, lo:lo + pgrp]
        cur = acc
        for sft in range(1, w):
            acc = acc + cat_scr[hist - sft * bblk:hist - sft * bblk + rows, lo:lo + pgrp]
        inv = 1.0 / jnp.minimum(pos + 1, w).astype(F32)
        diff = acc * inv - cur
        pool_parts.append(_mm(diff, wpool_ref[kk], precise))
    pool_out = jnp.concatenate(pool_parts, axis=-1) * pscale_ref[...]
    tail = cat_scr[rows:rows + hist, :]
    cat_scr[0:hist, :] = tail
    buf_ref[...] = tail.reshape(POOL_HIST, bblk, d_pool)

    if precise:
        mix = jnp.concatenate([ssm_out, pool_out], axis=-1)
    else:
        mix = jnp.concatenate([ssm_out.astype(BF16), pool_out.astype(BF16)], axis=-1)
    x1 = x + _mm(mix, wout_ref[...], precise)
    x1_ref[...] = x1
    ms2 = jnp.mean(x1 * x1, axis=-1, keepdims=True)
    t = x1 * lax.rsqrt(ms2 + EPS) * gffn_ref[...]

    t_hi = t.astype(BF16)
    t_lo = (t - t_hi.astype(F32)).astype(BF16)
    w_r = wr_ref[...]
    w_hi = w_r.astype(BF16)
    w_lo = (w_r - w_hi.astype(F32)).astype(BF16)
    logits = _dot(t_hi, w_hi) + _dot(t_lo, w_hi) + _dot(t_hi, w_lo) + br_ref[...]
    lane = lax.broadcasted_iota(jnp.int32, (rows, LANES), 1)
    big = jnp.int32(4 * LANES)
    lg = jnp.where(lane < n_expert_groups, logits, NEG)
    mg = jnp.max(lg, axis=-1, keepdims=True)
    gsel = jnp.min(jnp.where(lg == mg, lane, big), axis=-1, keepdims=True)
    p_sel = 1.0 / jnp.sum(jnp.exp(lg - mg), axis=-1, keepdims=True)
    e_lo = n_expert_groups + gsel * experts_per_group
    le = jnp.where((lane >= e_lo) & (lane < e_lo + experts_per_group), logits, NEG)
    v1 = jnp.max(le, axis=-1, keepdims=True)
    i1 = jnp.min(jnp.where(le == v1, lane, big), axis=-1, keepdims=True)
    le2 = jnp.where(lane == i1, NEG, le)
    v2 = jnp.max(le2, axis=-1, keepdims=True)
    i2 = jnp.min(jnp.where(le2 == v2, lane, big), axis=-1, keepdims=True)
    ex = jnp.exp(v2 - v1)
    q1 = p_sel / (1.0 + ex)
    q2 = q1 * ex
    e1 = (i1 - n_expert_groups).astype(F32)
    e2 = (i2 - n_expert_groups).astype(F32)
    route = jnp.where(lane == 0, e1, jnp.where(lane == 1, e2, jnp.where(lane == 2, q1,
                      jnp.where(lane == 3, q2, 0.0))))

    lf = lax.broadcasted_iota(jnp.int32, (MOE_TILE, LANES), 1).astype(F32)
    n_local = xloc_ref.shape[1]
    before_e = (lax.broadcasted_iota(jnp.int32, (LANES, LANES), 0)
                < lax.broadcasted_iota(jnp.int32, (LANES, LANES), 1)).astype(BF16)
    earlier = (lax.broadcasted_iota(jnp.int32, (MOE_TILE, MOE_TILE), 0)
               > lax.broadcasted_iota(jnp.int32, (MOE_TILE, MOE_TILE), 1)).astype(BF16)
    local_row = lax.broadcasted_iota(jnp.int32, (n_local, MOE_TILE), 0)
    route_parts = []
    for hh in range(rows // MOE_TILE):
        sl = slice(hh * MOE_TILE, (hh + 1) * MOE_TILE)
        rt = route[sl]
        oh1 = lf == rt[:, 0:1]
        oh2 = lf == rt[:, 1:2]
        oh = jnp.logical_or(oh1, oh2).astype(F32)
        cnt = jnp.sum(oh, axis=0, keepdims=True)
        run_chunks = jnp.floor((cnt + (SUBLANES - 1)) * (1.0 / SUBLANES))
        loff = SUBLANES * _dot(jnp.broadcast_to(run_chunks, (SUBLANES, LANES)).astype(BF16), before_e)[0:1]
        where_to = loff + _dot(earlier, oh.astype(BF16))
        lpos1 = jnp.sum(jnp.where(oh1, where_to, 0.0), axis=-1, keepdims=True)
        lpos2 = jnp.sum(jnp.where(oh2, where_to, 0.0), axis=-1, keepdims=True)
        lpos_t = jnp.where(lf == 0.0, lpos1, jnp.where(lf == 1.0, lpos2, 0.0)).T
        perm = jnp.logical_or(local_row == lpos_t[0:1].astype(jnp.int32),
                              local_row == lpos_t[1:2].astype(jnp.int32)).astype(BF16)
        xloc_ref[hh] = _pack_halves(_dot(perm, t_hi[sl]))
        cnt_ref[hh] = cnt
        route_parts.append(jnp.where(lf == 4.0, lpos1, jnp.where(lf == 5.0, lpos2, rt)))
    route_ref[...] = jnp.concatenate(route_parts, axis=0)


def _split_inputs(refs, pending):
    if pending:
        x_ref, yl_ref, rprev_ref = refs[:3]
        return (lambda: _moe_combine(x_ref, yl_ref, rprev_ref)), refs[3:]
    x_ref = refs[0]
    return (lambda: x_ref[...]), refs[1:]


def _mixer_kernel(*refs, cfg, geo, snapshot, pending):
    x_ref, refs = _split_inputs(refs, pending)
    h0s_ref, buf0s_ref = refs[:2]
    w_refs = refs[2:2 + N_MIXER_WEIGHTS]
    rest = refs[2 + N_MIXER_WEIGHTS:]
    x1_ref, route_ref, xloc_ref, cnt_ref, hp_ref, bufp_ref, hs_ref, bufs_ref = rest[:8]
    rest = rest[8:]
    if snapshot:
        hsnap_ref, bsnap_ref = rest[:2]
        rest = rest[2:]
    hm, bufm, u_m, y_m, cat_m, u_p, y_p, cat_p, u_s, y_s, cat_s = rest
    outs = (x1_ref, route_ref, xloc_ref, cnt_ref)
    s = pl.program_id(0)
    n_p = geo["n_p_steps"]
    batch = geo["batch"]

    @pl.when(s == 0)
    def _meta():
        _stream_step(x_ref, w_refs, lambda: jnp.zeros(hm.shape, F32), lambda: jnp.zeros(bufm.shape, F32), True, 0,
                     *outs, hm, bufm, u_m, y_m, cat_m, bblk=geo["m_batch"], tile_l=geo["n_meta"], cfg=cfg,
                     precise=False)

    @pl.when(jnp.logical_and(s >= 1, s <= n_p))
    def _prompt():
        if snapshot:
            @pl.when(s == n_p)
            def _():
                hsnap_ref[...] = hp_ref[...]
                bsnap_ref[...] = bufp_ref[...]
        _stream_step(x_ref, w_refs, lambda: hm[:, 0:batch, :], lambda: bufm[:, 0:batch, :], s == 1,
                     geo["n_meta"] + (s - 1) * geo["p_tile_l"], *outs, hp_ref, bufp_ref, u_p, y_p, cat_p,
                     bblk=batch, tile_l=geo["p_tile_l"], cfg=cfg, precise=False)

    @pl.when(s > n_p)
    def _sample():
        _stream_step(x_ref, w_refs, lambda: h0s_ref[...], lambda: buf0s_ref[...], True, PAST_LEN,
                     *outs, hs_ref, bufs_ref, u_s, y_s, cat_s, bblk=geo["s_bblk"], tile_l=geo["dec_seq"], cfg=cfg,
                     precise=False)


def _mixer_weights(lw):
    return [lw["g_mix"], lw["w_in"], lw["mq"], lw["pmat"], lw["a8"], lw["ssm_d"], lw["w_glu"], lw["b_glu"],
            lw["w_pool"], lw["pool_scale"], lw["w_out"], lw["g_ffn"], lw["w_r"], lw["b_r"]]


def _stream_scratch(tile_l, bblk, d_model, d_ssm):
    return [pltpu.VMEM((tile_l // CHUNK, CHUNK, bblk, d_model), F32),
            pltpu.VMEM((tile_l // CHUNK, CHUNK, bblk, d_ssm), F32),
            pltpu.VMEM(((POOL_HIST + tile_l) * bblk, d_model - d_ssm), F32)]


def _pending_specs(pending, rows, n_local, d_model, blk):
    if pending is None:
        return [], []
    per_step = rows // MOE_TILE
    return ([pl.BlockSpec((per_step, n_local, d_model // 2), lambda s: (blk(s), 0, 0)),
             pl.BlockSpec((rows, LANES), lambda s: (blk(s), 0))], list(pending))


def _mixer_call(x_flat, pending, h0s, buf0s, lw, *, geo, cfg, snapshot):
    total, d_model = x_flat.shape
    groups, dec_batch, _ = h0s.shape
    d_pool = buf0s.shape[-1]
    d_ssm = d_model - d_pool
    g = geo
    n_p, n_s = g["n_p_steps"], g["n_s_steps"]
    rows = g["step_rows"]
    per_step = rows // MOE_TILE
    n_local = g["n_local"]
    blk = lambda s: jnp.where(s == 0, n_p + n_s, s - 1)
    sblk = lambda s: jnp.clip(s - (n_p + 1), 0, n_s - 1)
    weights = _mixer_weights(lw)

    def full(a):
        nd = a.ndim
        return pl.BlockSpec(a.shape, lambda s, _n=nd: (0,) * _n)

    h_s_spec = pl.BlockSpec((groups, g["s_bblk"], LANES), lambda s: (0, sblk(s), 0))
    buf_s_spec = pl.BlockSpec((POOL_HIST, g["s_bblk"], d_pool), lambda s: (0, sblk(s), 0))
    pend_specs, pend_args = _pending_specs(pending, rows, n_local, d_model, blk)
    in_specs = ([pl.BlockSpec((rows, d_model), lambda s: (blk(s), 0))] + pend_specs + [h_s_spec, buf_s_spec]
                + [full(w) for w in weights])
    h_p_shape, buf_p_shape = (groups, g["batch"], LANES), (POOL_HIST, g["batch"], d_pool)
    out_specs = [pl.BlockSpec((rows, d_model), lambda s: (blk(s), 0)),
                 pl.BlockSpec((rows, LANES), lambda s: (blk(s), 0)),
                 pl.BlockSpec((per_step, n_local, d_model // 2), lambda s: (blk(s), 0, 0)),
                 pl.BlockSpec((per_step, 1, LANES), lambda s: (blk(s), 0, 0)),
                 pl.BlockSpec(h_p_shape, lambda s: (0, 0, 0)), pl.BlockSpec(buf_p_shape, lambda s: (0, 0, 0)),
                 h_s_spec, buf_s_spec]
    out_shape = [jax.ShapeDtypeStruct((total, d_model), F32), jax.ShapeDtypeStruct((total, LANES), F32),
                 jax.ShapeDtypeStruct((total // MOE_TILE, n_local, d_model // 2), jnp.uint32),
                 jax.ShapeDtypeStruct((total // MOE_TILE, 1, LANES), F32),
                 jax.ShapeDtypeStruct(h_p_shape, F32), jax.ShapeDtypeStruct(buf_p_shape, F32),
                 jax.ShapeDtypeStruct((groups, dec_batch, LANES), F32),
                 jax.ShapeDtypeStruct((POOL_HIST, dec_batch, d_pool), F32)]
    if snapshot:
        out_specs += [pl.BlockSpec(h_p_shape, lambda s: (0, 0, 0)), pl.BlockSpec(buf_p_shape, lambda s: (0, 0, 0))]
        out_shape += [jax.ShapeDtypeStruct(h_p_shape, F32), jax.ShapeDtypeStruct(buf_p_shape, F32)]
    scratch = ([pltpu.VMEM((groups, g["m_batch"], LANES), F32), pltpu.VMEM((POOL_HIST, g["m_batch"], d_pool), F32)]
               + _stream_scratch(g["n_meta"], g["m_batch"], d_model, d_ssm)
               + _stream_scratch(g["p_tile_l"], g["batch"], d_model, d_ssm)
               + _stream_scratch(g["dec_seq"], g["s_bblk"], d_model, d_ssm))
    return pl.pallas_call(
        functools.partial(_mixer_kernel, cfg=cfg, geo=geo, snapshot=snapshot, pending=pending is not None),
        grid=(1 + n_p + n_s,),
        in_specs=in_specs,
        out_specs=out_specs,
        out_shape=out_shape,
        scratch_shapes=scratch,
        compiler_params=pltpu.CompilerParams(dimension_semantics=("arbitrary",), vmem_limit_bytes=VMEM_LIMIT),
        name="mixer",
    )(x_flat, *pend_args, h0s, buf0s, *weights)


def _mixer_tail_kernel(*refs, cfg, geo, pending):
    get_x, refs = _split_inputs(refs, pending)
    h0_ref, buf0_ref = refs[:2]
    w_refs = refs[2:2 + N_MIXER_WEIGHTS]
    rest = refs[2 + N_MIXER_WEIGHTS + 4:]
    x1_ref, route_ref, xloc_ref, cnt_ref, h_ref, buf_ref, u_scr, y_scr, cat_scr = rest
    pos = geo["n_meta"] + (geo["n_p_steps"] - 1) * geo["p_tile_l"]
    _stream_step(get_x, w_refs, lambda: h0_ref[...], lambda: buf0_ref[...], True, pos, x1_ref, route_ref, xloc_ref,
                 cnt_ref, h_ref, buf_ref, u_scr, y_scr, cat_scr, bblk=geo["batch"], tile_l=geo["p_tile_l"], cfg=cfg,
                 precise=True)


def _mixer_tail_call(x_flat, pending, h0, buf0, lw32, x1, route, xloc, cnt, *, geo, cfg):
    total, d_model = x_flat.shape
    groups = h0.shape[0]
    d_pool = buf0.shape[-1]
    g = geo
    rows = g["step_rows"]
    per_step = rows // MOE_TILE
    last = g["n_p_steps"] - 1
    weights = _mixer_weights(lw32)

    def full(a):
        nd = a.ndim
        return pl.BlockSpec(a.shape, lambda s, _n=nd: (0,) * _n)

    pend_specs, pend_args = _pending_specs(pending, rows, g["n_local"], d_model, lambda s: last)
    n_in = 3 + len(pend_args) + len(weights)
    return pl.pallas_call(
        functools.partial(_mixer_tail_kernel, cfg=cfg, geo=geo, pending=pending is not None),
        grid=(1,),
        in_specs=([pl.BlockSpec((rows, d_model), lambda s: (last, 0))] + pend_specs + [full(h0), full(buf0)]
                  + [full(w) for w in weights] + [pl.BlockSpec(memory_space=pl.ANY)] * 4),
        out_specs=[pl.BlockSpec((rows, d_model), lambda s: (last, 0)),
                   pl.BlockSpec((rows, LANES), lambda s: (last, 0)),
                   pl.BlockSpec((per_step, g["n_local"], d_model // 2), lambda s: (last, 0, 0)),
                   pl.BlockSpec((per_step, 1, LANES), lambda s: (last, 0, 0)),
                   full(h0), full(buf0)],
        out_shape=[jax.ShapeDtypeStruct(a.shape, a.dtype) for a in (x1, route, xloc, cnt, h0, buf0)],
        scratch_shapes=_stream_scratch(g["p_tile_l"], g["batch"], d_model, d_model - d_pool),
        input_output_aliases={n_in + k: k for k in range(4)},
        compiler_params=pltpu.CompilerParams(dimension_semantics=("arbitrary",), vmem_limit_bytes=VMEM_LIMIT),
        name="mixer_tail",
    )(x_flat, *pend_args, h0, buf0, *weights, x1, route, xloc, cnt)


def _experts_kernel(exp_ref, nact_ref, src_ref, nval_ref, xl_in_ref, wg_ref, wu_ref, wd_ref, rows_ref,
                    xbuf, ybuf, wgu_scr, wd_scr, sem_in, sem_out):
    del xl_in_ref
    j = pl.program_id(0)
    f = wg_ref.shape[-1]
    chunks_per_tile = EXPERT_TILE // SUBLANES
    nact = nact_ref[0]

    def for_tile_chunks(tile, op, scatter):
        slot = tile % 2

        def body(k, carry):
            hbm = rows_ref.at[pl.ds(pl.multiple_of(src_ref[tile * chunks_per_tile + k], SUBLANES), SUBLANES), :]
            local = pl.ds(pl.multiple_of(k * SUBLANES, SUBLANES), SUBLANES)
            if scatter:
                op(pltpu.make_async_copy(ybuf.at[slot, local, :], hbm, sem_out.at[slot]))
            else:
                op(pltpu.make_async_copy(hbm, xbuf.at[slot, local, :], sem_in.at[slot]))
            return carry

        lax.fori_loop(0, nval_ref[tile], body, 0)

    start = lambda cp: cp.start()
    wait = lambda cp: cp.wait()

    @pl.when(j == 0)
    def _():
        xbuf[...] = jnp.zeros_like(xbuf)
        for_tile_chunks(0, start, False)

    @pl.when(j + 1 < nact)
    def _():
        for_tile_chunks(j + 1, start, False)

    @pl.when(jnp.logical_and(j >= 2, j - 2 < nact))
    def _():
        for_tile_chunks(j - 2, wait, True)

    active = j < nact
    changed = jnp.logical_or(j == 0, exp_ref[j] != exp_ref[jnp.maximum(j - 1, 0)])

    @pl.when(jnp.logical_and(active, changed))
    def _():
        wgu_scr[:, 0:f] = wg_ref[...].astype(BF16)
        wgu_scr[:, f:2 * f] = wu_ref[...].astype(BF16)
        wd_scr[...] = wd_ref[...].astype(BF16)

    @pl.when(active)
    def _():
        for_tile_chunks(j, wait, False)
        half = wgu_scr.shape[0] // 2
        t_lo, t_hi = _unpack_halves(xbuf[j % 2])
        gu = _dot(t_lo, wgu_scr[0:half, :]) + _dot(t_hi, wgu_scr[half:2 * half, :])
        act = jax.nn.silu(gu[:, 0:f]) * gu[:, f:2 * f]
        ybuf[j % 2] = _pack_halves(_dot(act.astype(BF16), wd_scr[...]))
        for_tile_chunks(j, start, True)


def _experts_call(tile_exp, n_active, src, nval, xloc, w_gate, w_up, w_down, layer, n_tiles):
    _, n_exp, d_model, f = w_gate.shape
    xl_flat = xloc.reshape(-1, xloc.shape[-1])
    wspec = lambda a, b: pl.BlockSpec((None, None, a, b), lambda i, ex, *_: (layer, ex[i], 0, 0))
    buf = pltpu.VMEM((2, EXPERT_TILE, d_model // 2), jnp.uint32)
    out = pl.pallas_call(
        _experts_kernel,
        grid_spec=pltpu.PrefetchScalarGridSpec(
            num_scalar_prefetch=4,
            grid=(n_tiles + 2,),
            in_specs=[pl.BlockSpec(memory_space=pl.ANY), wspec(d_model, f), wspec(d_model, f), wspec(f, d_model)],
            out_specs=pl.BlockSpec(memory_space=pl.ANY),
            scratch_shapes=[buf, buf, pltpu.VMEM((d_model, 2 * f), BF16), pltpu.VMEM((f, d_model), BF16),
                            pltpu.SemaphoreType.DMA((2,)), pltpu.SemaphoreType.DMA((2,))],
        ),
        out_shape=jax.ShapeDtypeStruct(xl_flat.shape, xl_flat.dtype),
        input_output_aliases={4: 0},
        compiler_params=pltpu.CompilerParams(dimension_semantics=("arbitrary",), vmem_limit_bytes=VMEM_LIMIT),
        name="moe_experts",
    )(tile_exp, n_active, src, nval, xl_flat, w_gate, w_up, w_down)
    return out.reshape(xloc.shape)


def _moe_experts(xloc, cnt, w_gate, w_up, w_down, *, layer):
    n_exp = w_gate.shape[1]
    nt, n_local, _ = xloc.shape
    total = nt * MOE_TILE
    cpt = EXPERT_TILE // SUBLANES
    n_chunks = n_local // SUBLANES
    counts = cnt[:, 0, :n_exp].astype(jnp.int32)
    run = (counts + SUBLANES - 1) // SUBLANES
    lend = jnp.cumsum(run, axis=1)
    loff = lend - run
    gend = jnp.cumsum(run, axis=0)
    gcum = gend - run
    seg = gend[-1]
    seg_tiles = (seg + cpt - 1) // cpt
    seg_end = jnp.cumsum(seg_tiles)
    seg_off = seg_end - seg_tiles
    n_tiles = (2 * total + nt * n_exp * (SUBLANES - 1) + n_exp * (EXPERT_TILE - 1)) // EXPERT_TILE + 1
    n_steps = n_tiles + 2
    n_active = seg_end[-1].reshape(1).astype(jnp.int32)
    tile_idx = jnp.minimum(jnp.arange(n_steps, dtype=jnp.int32), n_active - 1)
    tile_exp = jnp.minimum(jnp.sum((tile_idx[:, None] >= seg_end[None, :]).astype(jnp.int32), axis=1), n_exp - 1)
    is_e = tile_exp[:, None] == jnp.arange(n_exp, dtype=jnp.int32)[None, :]
    pick = lambda table: jnp.sum(jnp.where(is_e[:, None, :], table[None, :, :], 0), axis=2)
    pick1 = lambda vec: jnp.sum(jnp.where(is_e, vec[None, :], 0), axis=1)
    g = ((jnp.arange(n_steps, dtype=jnp.int32) - pick1(seg_off)) * cpt)[:, None] + jnp.arange(cpt, dtype=jnp.int32)
    gend_t, gcum_t, loff_t = pick(gend), pick(gcum), pick(loff)
    owner = jnp.logical_and(g[:, :, None] >= gcum_t[:, None, :], g[:, :, None] < gend_t[:, None, :])
    tile_base = jnp.arange(nt, dtype=jnp.int32) * n_chunks
    src = jnp.sum(jnp.where(owner, (tile_base[None, :] + loff_t - gcum_t)[:, None, :] + g[:, :, None], 0), axis=2)
    src = (src * SUBLANES).reshape(-1)
    nval = jnp.clip(pick1(seg) - g[:, 0], 0, cpt)
    return _experts_call(tile_exp, n_active, src, nval, xloc, w_gate, w_up, w_down, layer, n_tiles)


def _finish_kernel(x1_ref, yl_ref, route_ref, gfin_ref, yp_ref, ys_ref, *, n_p_steps):
    i = pl.program_id(0)
    x = _moe_combine(x1_ref, yl_ref, route_ref)
    ms = jnp.mean(x * x, axis=-1, keepdims=True)
    y = x * lax.rsqrt(ms + EPS) * gfin_ref[...]
    pb, pl_ = yp_ref.shape[0], yp_ref.shape[1]
    sb, sl = ys_ref.shape[0], ys_ref.shape[1]

    @pl.when(i < n_p_steps)
    def _():
        for l in range(pl_):
            yp_ref[:, l, :] = y[l * pb:(l + 1) * pb, :]

    @pl.when(i >= n_p_steps)
    def _():
        for l in range(sl):
            ys_ref[:, l, :] = y[l * sb:(l + 1) * sb, :]


def _finish_call(x1, yl, route, g_final, *, geo):
    d_model = x1.shape[-1]
    g = geo
    rows = g["step_rows"]
    n_p, n_s = g["n_p_steps"], g["n_s_steps"]
    return pl.pallas_call(
        functools.partial(_finish_kernel, n_p_steps=n_p),
        grid=(n_p + n_s,),
        in_specs=[pl.BlockSpec((rows, d_model), lambda i: (i, 0)),
                  pl.BlockSpec((rows // MOE_TILE, g["n_local"], d_model // 2), lambda i: (i, 0, 0)),
                  pl.BlockSpec((rows, LANES), lambda i: (i, 0)),
                  pl.BlockSpec((1, d_model), lambda i: (0, 0))],
        out_specs=[pl.BlockSpec((g["batch"], g["p_tile_l"], d_model), lambda i: (0, jnp.minimum(i, n_p - 1), 0)),
                   pl.BlockSpec((g["s_bblk"], g["dec_seq"], d_model), lambda i: (jnp.clip(i - n_p, 0, n_s - 1), 0, 0))],
        out_shape=[jax.ShapeDtypeStruct((g["batch"], g["seq"], d_model), F32),
                   jax.ShapeDtypeStruct((g["dec_batch"], g["dec_seq"], d_model), F32)],
        compiler_params=pltpu.CompilerParams(dimension_semantics=("arbitrary",), vmem_limit_bytes=VMEM_LIMIT),
        name="finish",
    )(x1, yl, route, g_final)


def _to_rows_kernel(xp_ref, xs_ref, meta_ref, o_ref, *, n_p_steps, n_s_steps):
    i = pl.program_id(0)
    pb, pl_ = xp_ref.shape[0], xp_ref.shape[1]
    sb, sl = xs_ref.shape[0], xs_ref.shape[1]
    ml = meta_ref.shape[0]
    mb = o_ref.shape[0] // ml

    @pl.when(i < n_p_steps)
    def _():
        for l in range(pl_):
            o_ref[l * pb:(l + 1) * pb, :] = xp_ref[:, l, :]

    @pl.when(jnp.logical_and(i >= n_p_steps, i < n_p_steps + n_s_steps))
    def _():
        for l in range(sl):
            o_ref[l * sb:(l + 1) * sb, :] = xs_ref[:, l, :]

    @pl.when(i >= n_p_steps + n_s_steps)
    def _():
        for l in range(ml):
            o_ref[l * mb:(l + 1) * mb, :] = jnp.broadcast_to(meta_ref[l:l + 1, :], (mb, o_ref.shape[1]))


def _to_rows_call(x_prompt, x_sample, meta, *, step_rows):
    batch, seq, d_model = x_prompt.shape
    dec_batch, dec_seq, _ = x_sample.shape
    p_tile_l, s_bblk = step_rows // batch, step_rows // dec_seq
    n_p_steps, n_s_steps = seq // p_tile_l, dec_batch // s_bblk
    steps = n_p_steps + n_s_steps + 1
    return pl.pallas_call(
        functools.partial(_to_rows_kernel, n_p_steps=n_p_steps, n_s_steps=n_s_steps),
        grid=(steps,),
        in_specs=[pl.BlockSpec((batch, p_tile_l, d_model), lambda i: (0, jnp.minimum(i, n_p_steps - 1), 0)),
                  pl.BlockSpec((s_bblk, dec_seq, d_model),
                               lambda i: (jnp.clip(i - n_p_steps, 0, n_s_steps - 1), 0, 0)),
                  pl.BlockSpec(meta.shape, lambda i: (0, 0))],
        out_specs=pl.BlockSpec((step_rows, d_model), lambda i: (i, 0)),
        out_shape=jax.ShapeDtypeStruct((steps * step_rows, d_model), F32),
        compiler_params=pltpu.CompilerParams(dimension_semantics=("arbitrary",)),
        name="to_rows",
    )(x_prompt, x_sample, meta)


def kernel(x_prompt, x_sample, state_ssm_re, state_ssm_im, state_pool, meta_tokens, norm_mix_g, w_in, ssm_a_re,
           ssm_a_im, ssm_log_dt, ssm_b_re, ssm_b_im, ssm_c_re, ssm_c_im, ssm_d, w_glu, b_glu, w_pool, pool_scale,
           w_out, norm_ffn_g, w_router_group, b_router_group, w_router_expert, b_router_expert, w_gate, w_up,
           w_down, norm_final_g):
    batch, seq, d_model = x_prompt.shape
    dec_batch, dec_seq, _ = x_sample.shape
    depth, _, groups, nstate = state_ssm_re.shape
    n_meta = meta_tokens.shape[0]
    pool_buf, d_pool = state_pool.shape[2], state_pool.shape[3]
    n_pool_groups = w_pool.shape[1]
    windows = tuple(2 ** (k + 1) for k in range(n_pool_groups))
    assert pool_buf == max(windows) - 1 == POOL_HIST - 1
    n_expert_groups = w_router_group.shape[-1]
    n_experts = w_router_expert.shape[-1]
    cfg = dict(windows=windows, n_expert_groups=n_expert_groups, experts_per_group=n_experts // n_expert_groups)

    step_rows = 512
    p_tile_l, s_bblk, m_batch = step_rows // batch, step_rows // dec_seq, step_rows // n_meta
    n_p, n_s, n_m = batch * seq, dec_batch * dec_seq, m_batch * n_meta
    assert m_batch >= batch and seq % p_tile_l == 0 and dec_batch % s_bblk == 0
    total = n_p + n_s + n_m
    assert total % step_rows == 0 and step_rows % MOE_TILE == 0
    n_local = 2 * MOE_TILE + n_experts * (SUBLANES - 1)
    n_local = (n_local + LANES - 1) // LANES * LANES
    geo = dict(batch=batch, seq=seq, dec_batch=dec_batch, dec_seq=dec_seq, n_meta=n_meta, step_rows=step_rows,
               p_tile_l=p_tile_l, s_bblk=s_bblk, m_batch=m_batch, n_p_steps=seq // p_tile_l,
               n_s_steps=dec_batch // s_bblk, n_local=n_local)

    mq, pmat, a8 = _s5_prepare(ssm_a_re, ssm_a_im, ssm_log_dt, ssm_b_re, ssm_b_im, ssm_c_re, ssm_c_im)

    x = _to_rows_call(x_prompt.astype(F32), x_sample.astype(F32), meta_tokens.astype(F32), step_rows=step_rows)
    assert x.shape[0] == total

    row2 = lambda v: v.reshape(1, -1).astype(F32)
    outs = {k: [] for k in ("hp", "bp", "hs", "bs")}
    pending = None
    for l in range(depth):
        w_r = jnp.concatenate([w_router_group[l], w_router_expert[l]], axis=1).astype(F32)
        w_r = jnp.pad(w_r, ((0, 0), (0, LANES - w_r.shape[1])))
        b_r = jnp.concatenate([b_router_group[l], b_router_expert[l]]).astype(F32)
        lw32 = dict(g_mix=row2(norm_mix_g[l]), w_in=w_in[l].astype(F32), mq=mq[l], pmat=pmat[l], a8=a8[l],
                    ssm_d=row2(ssm_d[l]), w_glu=w_glu[l].astype(F32), b_glu=row2(b_glu[l]),
                    w_pool=w_pool[l].astype(F32), pool_scale=row2(pool_scale[l]), w_out=w_out[l].astype(F32),
                    g_ffn=row2(norm_ffn_g[l]), w_r=w_r,
                    b_r=jnp.pad(b_r, (0, LANES - b_r.shape[0])).reshape(1, LANES))
        lw = dict(lw32, **{k: lw32[k].astype(BF16) for k in ("w_in", "mq", "pmat", "w_glu", "w_pool", "w_out")})

        h0_s = jnp.concatenate([state_ssm_re[l], state_ssm_im[l]], axis=-1).transpose(1, 0, 2).astype(F32)
        buf0_s = jnp.pad(state_pool[l].astype(F32).transpose(1, 0, 2), ((1, 0), (0, 0), (0, 0)))
        redo_last = l < depth - 1
        res = _mixer_call(x, pending, h0_s, buf0_s, lw, geo=geo, cfg=cfg, snapshot=redo_last)
        x1, route, xloc, cnt, h_p, buf_p, h_s, buf_s = res[:8]
        if redo_last:
            x1, route, xloc, cnt, h_p, buf_p = _mixer_tail_call(x, pending, res[8], res[9], lw32, x1, route, xloc,
                                                                 cnt, geo=geo, cfg=cfg)
        outs["hp"].append(h_p)
        outs["bp"].append(buf_p)
        outs["hs"].append(h_s)
        outs["bs"].append(buf_s)
        x, pending = x1, (_moe_experts(xloc, cnt, w_gate, w_up, w_down, layer=l), route)

    y_prompt, y_sample = _finish_call(x, pending[0], pending[1], row2(norm_final_g), geo=geo)
    st = lambda hs: jnp.stack(hs).transpose(0, 2, 1, 3)
    pl_out = lambda bs: jnp.stack(bs)[:, 1:].transpose(0, 2, 1, 3)
    hp, hs = st(outs["hp"]), st(outs["hs"])
    return (y_prompt, y_sample, hp[..., :nstate], hp[..., nstate:], pl_out(outs["bp"]),
            hs[..., :nstate], hs[..., nstate:], pl_out(outs["bs"]))
```

```python
import functools

import numpy as np
import jax
import jax.numpy as jnp
from jax import lax
from jax.experimental import pallas as pl
from jax.experimental.pallas import tpu as pltpu

F32 = jnp.float32
BF16 = jnp.bfloat16

LANES = 128
SUBLANES = 8
CHUNK = 8
POOL_HIST = 16
MOE_TILE = 256
EXPERT_TILE = 256
VMEM_LIMIT = 56 * 1024 * 1024
EPS = 1e-6
NEG = -1e30
PAST_LEN = 16384


def _dot(a, b):
    return jnp.dot(a, b, preferred_element_type=F32)


def _mm(a, w, precise):
    if precise:
        return jnp.dot(a, w, precision=lax.Precision.HIGHEST, preferred_element_type=F32)
    return _dot(a.astype(BF16), w)


def _pack_halves(x):
    k = x.shape[-1] // 2
    return pltpu.pack_elementwise([x[:, :k], x[:, k:]], packed_dtype=BF16)


def _unpack_halves(p):
    lo = pltpu.unpack_elementwise(p, index=0, packed_dtype=BF16, unpacked_dtype=F32)
    hi = pltpu.unpack_elementwise(p, index=1, packed_dtype=BF16, unpacked_dtype=F32)
    return lo.astype(BF16), hi.astype(BF16)


def _dot_nt_f32(a, b):
    return lax.dot_general(a, b, (((1,), (1,)), ((), ())), precision=lax.Precision.HIGHEST,
                           preferred_element_type=F32)


def _s5_prep_kernel(a_re_ref, a_im_ref, ldt_ref, bt_ref, btx_ref, c_ref, cx_ref, mq_ref, p_ref, a8_ref):
    ch = bt_ref.shape[1]
    first_half = lax.broadcasted_iota(jnp.int32, (1, LANES), 1) < LANES // 2
    sgn = jnp.where(first_half, 1.0, -1.0)
    for s in range(CHUNK):
        a_re = a_re_ref[s]
        a_im = a_im_ref[s]
        dt = jnp.exp(ldt_ref[s])
        mag = jnp.exp(dt * a_re)
        abr = mag * jnp.cos(dt * a_im)
        abi = mag * jnp.sin(dt * a_im)
        den = a_re * a_re + a_im * a_im
        nr = abr - 1.0
        f_re = (nr * a_re + abi * a_im) / den
        f_im = (abi * a_re - nr * a_im) / den
        bb = f_re * bt_ref[s] - sgn * f_im * btx_ref[s]
        cc = c_ref[s] * sgn
        ccx = -cx_ref[s]
        bbx = f_re * btx_ref[s] + sgn * f_im * bt_ref[s]
        pr = jnp.ones_like(a_re)
        pi = jnp.zeros_like(a_re)
        e2, w2 = [], []
        for k in range(CHUNK + 1):
            e2.append(pr * cc + pi * ccx)
            w2.append(pr * bb - sgn * pi * bbx)
            if k < CHUNK:
                pr, pi = pr * abr - pi * abi, pr * abi + pi * abr
        a8_ref[s, 0:1, :] = pr
        a8_ref[s, 1:2, :] = -sgn * pi
        a8_ref[s, 2:3, :] = sgn * pi
        zero = jnp.zeros_like(bb)
        jpos = [(p - s) % CHUNK for p in range(CHUNK)]
        for p in range(CHUNK):
            j = jpos[p]
            taps = jnp.concatenate([e2[jp - j] if jp >= j else zero for jp in jpos], axis=0)
            rows = slice(p * ch, (p + 1) * ch)
            mq_ref[s, rows, 0:LANES] = _dot_nt_f32(bb, taps)
            mq_ref[s, rows, LANES:2 * LANES] = w2[CHUNK - 1 - j]
        pt = jnp.concatenate([e2[jp + 1] for jp in jpos], axis=0)
        p_ref[s] = pt.T


def _s5_prepare(a_re, a_im, log_dt, b_re, b_im, c_re, c_im):
    depth, groups, nstate = a_re.shape
    ch = b_re.shape[-1]
    dg = depth * groups
    assert ch * CHUNK == LANES and 2 * nstate == LANES and groups % CHUNK == 0
    dup = lambda x: jnp.concatenate([x, x], axis=-1).reshape(dg, 1, LANES).astype(F32)
    btr = jnp.swapaxes(b_re.reshape(dg, nstate, ch), 1, 2).astype(F32)
    bti = jnp.swapaxes(b_im.reshape(dg, nstate, ch), 1, 2).astype(F32)
    cr = c_re.reshape(dg, ch, nstate).astype(F32)
    ci = c_im.reshape(dg, ch, nstate).astype(F32)
    pair = lambda x, y: jnp.concatenate([x, y], axis=-1)
    blk = lambda *shape: pl.BlockSpec((CHUNK,) + shape, lambda g: (g,) + (0,) * len(shape))
    mq, pmat, a8 = pl.pallas_call(
        _s5_prep_kernel,
        grid=(dg // CHUNK,),
        in_specs=[blk(1, LANES), blk(1, LANES), blk(1, 1)] + [blk(ch, LANES)] * 4,
        out_specs=[blk(LANES, 2 * LANES), blk(LANES, LANES), blk(3, LANES)],
        out_shape=[jax.ShapeDtypeStruct((dg, LANES, 2 * LANES), F32), jax.ShapeDtypeStruct((dg, LANES, LANES), F32),
                   jax.ShapeDtypeStruct((dg, 3, LANES), F32)],
        name="s5_prep",
    )(dup(a_re), dup(a_im), log_dt.reshape(dg, 1, 1).astype(F32), pair(btr, bti), pair(bti, btr),
      pair(cr, ci), pair(ci, cr))
    shp = lambda x: x.reshape((depth, groups) + x.shape[1:])
    return shp(mq), shp(pmat), shp(a8)


N_MIXER_WEIGHTS = 14


def _moe_combine(x1_ref, yl_ref, route_ref):
    rows = x1_ref.shape[0]
    n_local = yl_ref.shape[1]
    lio = lax.broadcasted_iota(jnp.int32, (MOE_TILE, n_local), 1)
    parts = []
    for hh in range(rows // MOE_TILE):
        r = route_ref[hh * MOE_TILE:(hh + 1) * MOE_TILE, :]
        wperm = jnp.where(lio == r[:, 4:5].astype(jnp.int32), r[:, 2:3],
                          jnp.where(lio == r[:, 5:6].astype(jnp.int32), r[:, 3:4], 0.0)).astype(BF16)
        y_lo, y_hi = _unpack_halves(yl_ref[hh])
        parts.append(jnp.concatenate([_dot(wperm, y_lo), _dot(wperm, y_hi)], axis=-1))
    return x1_ref[...] + jnp.concatenate(parts, axis=0)


def _stream_step(get_x, w_refs, h_init, buf_init, first, pos_base, x1_ref, route_ref, xloc_ref, cnt_ref,
                 h_ref, buf_ref, u_scr, y_scr, cat_scr, *, bblk, tile_l, cfg, precise):
    (gmix_ref, win_ref, mq_ref, p_ref, a8_ref, d_ref, wglu_ref, bglu_ref, wpool_ref, pscale_ref, wout_ref,
     gffn_ref, wr_ref, br_ref) = w_refs
    windows = cfg["windows"]
    n_expert_groups = cfg["n_expert_groups"]
    experts_per_group = cfg["experts_per_group"]
    rows = tile_l * bblk
    nk = tile_l // CHUNK
    n = nk * bblk
    d_model = x1_ref.shape[-1]
    d_ssm = y_scr.shape[-1]
    d_pool = d_model - d_ssm
    hist = POOL_HIST * bblk

    def _init():
        h_ref[...] = h_init()
        cat_scr[0:hist, :] = buf_init().reshape(hist, d_pool)

    if first is True:
        _init()
    else:
        pl.when(first)(_init)

    x = get_x()
    ms = jnp.mean(x * x, axis=-1, keepdims=True)
    h = x * lax.rsqrt(ms + EPS) * gmix_ref[...]
    u = _mm(h, win_ref[...], precise)
    u_scr[...] = u.reshape(nk, CHUNK, bblk, d_model)

    slot = lax.broadcasted_iota(jnp.int32, (n, LANES), 1) // (LANES // CHUNK)
    for v in range(d_ssm // LANES):
        rolled = []
        for j in range(CHUNK):
            uj = u_scr[:, j, :, v * LANES:(v + 1) * LANES].reshape(n, LANES)
            rolled.append(pltpu.roll(uj, (LANES // CHUNK) * j, axis=1) if j else uj)
        ys = []
        for s in range(CHUNK):
            g = v * CHUNK + s
            ug = rolled[0]
            for j in range(1, CHUNK):
                ug = jnp.where(slot == (s + j) % CHUNK, rolled[j], ug)
            yq = _mm(ug, mq_ref[g], precise)
            sg = yq[:, LANES:]
            sg_sw = pltpu.roll(sg, LANES // 2, axis=1)
            a8c = a8_ref[g, 0:1, :]
            a8s = a8_ref[g, 1:2, :]
            a8w = a8_ref[g, 2:3, :]
            hcur = h_ref[g]
            hsw = pltpu.roll(hcur, LANES // 2, axis=1)
            hins = []
            for k in range(nk):
                hins.append(hcur)
                sk = sg[k * bblk:(k + 1) * bblk]
                skw = sg_sw[k * bblk:(k + 1) * bblk]
                hcur, hsw = a8c * hcur + a8s * hsw + sk, a8c * hsw + a8w * hcur + skw
            h_ref[g] = hcur
            hin = jnp.concatenate(hins, axis=0) if nk > 1 else hins[0]
            ys.append(yq[:, :LANES] + _mm(hin, p_ref[g], precise))
        for j in range(CHUNK):
            z = ys[0]
            for s in range(1, CHUNK):
                z = jnp.where(slot == (s + j) % CHUNK, ys[s], z)
            if j:
                z = pltpu.roll(z, LANES - (LANES // CHUNK) * j, axis=1)
            y_scr[:, j, :, v * LANES:(v + 1) * LANES] = z.reshape(nk, bblk, LANES)

    u_ssm = u_scr[:, :, :, 0:d_ssm].reshape(rows, d_ssm)
    y = y_scr[...].reshape(rows, d_ssm) + d_ref[...] * u_ssm
    gl = jax.nn.gelu(y)
    ssm_out = gl * jax.nn.sigmoid(_mm(gl, wglu_ref[...], precise) + bglu_ref[...])

    up = u_scr[:, :, :, d_ssm:].reshape(rows, d_pool)
    cat_scr[hist:hist + rows, :] = up
    row_pos = lax.broadcasted_iota(jnp.int32, (rows, 1), 0) // bblk
    pos = pos_base + row_pos
    pool_parts = []
    pgrp = d_pool // len(windows)
    for kk, w in enumerate(windows):
        lo = kk * pgrp
        acc = cat_scr[hist:hist + rows, lo:lo + pgrp]
        cur = acc
        for sft in range(1, w):
            acc = acc + cat_scr[hist - sft * bblk:hist - sft * bblk + rows, lo:lo + pgrp]
        inv = 1.0 / jnp.minimum(pos + 1, w).astype(F32)
        diff = acc * inv - cur
        pool_parts.append(_mm(diff, wpool_ref[kk], precise))
    pool_out = jnp.concatenate(pool_parts, axis=-1) * pscale_ref[...]
    tail = cat_scr[rows:rows + hist, :]
    cat_scr[0:hist, :] = tail
    buf_ref[...] = tail.reshape(POOL_HIST, bblk, d_pool)

    if precise:
        mix = jnp.concatenate([ssm_out, pool_out], axis=-1)
    else:
        mix = jnp.concatenate([ssm_out.astype(BF16), pool_out.astype(BF16)], axis=-1)
    x1 = x + _mm(mix, wout_ref[...], precise)
    x1_ref[...] = x1
    ms2 = jnp.mean(x1 * x1, axis=-1, keepdims=True)
    t = x1 * lax.rsqrt(ms2 + EPS) * gffn_ref[...]

    t_hi = t.astype(BF16)
    t_lo = (t - t_hi.astype(F32)).astype(BF16)
    w_r = wr_ref[...]
    w_hi = w_r.astype(BF16)
    w_lo = (w_r - w_hi.astype(F32)).astype(BF16)
    hi_both = _dot(t_hi, jnp.concatenate([w_hi, w_lo], axis=-1))
    logits = hi_both[:, :LANES] + hi_both[:, LANES:] + _dot(t_lo, w_hi) + br_ref[...]
    lane = lax.broadcasted_iota(jnp.int32, (rows, LANES), 1).astype(F32)
    big = float(4 * LANES)
    lg = jnp.where(lane < n_expert_groups, logits, NEG)
    mg = jnp.max(lg, axis=-1, keepdims=True)
    gsel = jnp.min(jnp.where(lg == mg, lane, big), axis=-1, keepdims=True)
    p_sel = 1.0 / jnp.sum(jnp.exp(lg - mg), axis=-1, keepdims=True)
    e_lo = n_expert_groups + gsel * experts_per_group
    le = jnp.where((lane >= e_lo) & (lane < e_lo + experts_per_group), logits, NEG)
    v1 = jnp.max(le, axis=-1, keepdims=True)
    i1 = jnp.min(jnp.where(le == v1, lane, big), axis=-1, keepdims=True)
    le2 = jnp.where(lane == i1, NEG, le)
    v2 = jnp.max(le2, axis=-1, keepdims=True)
    i2 = jnp.min(jnp.where(le2 == v2, lane, big), axis=-1, keepdims=True)
    ex = jnp.exp(v2 - v1)
    q1 = p_sel / (1.0 + ex)
    q2 = q1 * ex
    e1 = i1 - n_expert_groups
    e2 = i2 - n_expert_groups
    route = jnp.where(lane == 0.0, e1, jnp.where(lane == 1.0, e2, jnp.where(lane == 2.0, q1,
                      jnp.where(lane == 3.0, q2, 0.0))))

    lf = lax.broadcasted_iota(jnp.int32, (MOE_TILE, LANES), 1).astype(F32)
    n_local = xloc_ref.shape[1]
    before_e = (lax.broadcasted_iota(jnp.int32, (LANES, LANES), 0)
                < lax.broadcasted_iota(jnp.int32, (LANES, LANES), 1)).astype(BF16)
    earlier = (lax.broadcasted_iota(jnp.int32, (MOE_TILE, MOE_TILE), 0)
               > lax.broadcasted_iota(jnp.int32, (MOE_TILE, MOE_TILE), 1)).astype(BF16)
    local_row = lax.broadcasted_iota(jnp.int32, (n_local, MOE_TILE), 0)
    route_parts = []
    for hh in range(rows // MOE_TILE):
        sl = slice(hh * MOE_TILE, (hh + 1) * MOE_TILE)
        rt = route[sl]
        oh1 = lf == rt[:, 0:1]
        oh2 = lf == rt[:, 1:2]
        oh = jnp.logical_or(oh1, oh2).astype(F32)
        cnt = jnp.sum(oh, axis=0, keepdims=True)
        run_chunks = jnp.floor((cnt + (SUBLANES - 1)) * (1.0 / SUBLANES))
        loff = SUBLANES * _dot(jnp.broadcast_to(run_chunks, (SUBLANES, LANES)).astype(BF16), before_e)[0:1]
        where_to = loff + _dot(earlier, oh.astype(BF16))
        lpos1 = jnp.sum(jnp.where(oh1, where_to, 0.0), axis=-1, keepdims=True)
        lpos2 = jnp.sum(jnp.where(oh2, where_to, 0.0), axis=-1, keepdims=True)
        lpos_t = jnp.where(lf == 0.0, lpos1, jnp.where(lf == 1.0, lpos2, 0.0)).T
        perm = jnp.logical_or(local_row == lpos_t[0:1].astype(jnp.int32),
                              local_row == lpos_t[1:2].astype(jnp.int32)).astype(BF16)
        xloc_ref[hh] = _pack_halves(_dot(perm, t_hi[sl]))
        cnt_ref[hh] = cnt
        route_parts.append(jnp.where(lf == 4.0, lpos1, jnp.where(lf == 5.0, lpos2, rt)))
    route_ref[...] = jnp.concatenate(route_parts, axis=0)


def _split_inputs(refs, source, geo):
    if source == "raw":
        xp_ref, xs_ref, meta_ref = refs[:3]
        by_position = lambda ref: (lambda: jnp.concatenate([ref[:, l, :] for l in range(ref.shape[1])], axis=0))
        meta = lambda: jnp.concatenate([jnp.broadcast_to(meta_ref[l:l + 1, :], (geo["m_batch"], meta_ref.shape[1]))
                                        for l in range(meta_ref.shape[0])], axis=0)
        return dict(meta=meta, prompt=by_position(xp_ref), sample=by_position(xs_ref)), refs[3:]
    if source == "pending":
        x_ref, yl_ref, rprev_ref = refs[:3]
        get, refs = (lambda: _moe_combine(x_ref, yl_ref, rprev_ref)), refs[3:]
    else:
        x_ref = refs[0]
        get, refs = (lambda: x_ref[...]), refs[1:]
    return dict(meta=get, prompt=get, sample=get), refs


def _mixer_kernel(*refs, cfg, geo, snapshot, source):
    get_x, refs = _split_inputs(refs, source, geo)
    h0s_ref, buf0s_ref = refs[:2]
    w_refs = refs[2:2 + N_MIXER_WEIGHTS]
    rest = refs[2 + N_MIXER_WEIGHTS:]
    x1_ref, route_ref, xloc_ref, cnt_ref, hp_ref, bufp_ref, hs_ref, bufs_ref = rest[:8]
    rest = rest[8:]
    if snapshot:
        hsnap_ref, bsnap_ref = rest[:2]
        rest = rest[2:]
    hm, bufm, u_m, y_m, cat_m, u_p, y_p, cat_p, u_s, y_s, cat_s = rest
    outs = (x1_ref, route_ref, xloc_ref, cnt_ref)
    s = pl.program_id(0)
    n_p = geo["n_p_steps"]
    batch = geo["batch"]

    @pl.when(s == 0)
    def _meta():
        _stream_step(get_x["meta"], w_refs, lambda: jnp.zeros(hm.shape, F32), lambda: jnp.zeros(bufm.shape, F32), True, 0,
                     *outs, hm, bufm, u_m, y_m, cat_m, bblk=geo["m_batch"], tile_l=geo["n_meta"], cfg=cfg,
                     precise=False)

    @pl.when(jnp.logical_and(s >= 1, s <= n_p))
    def _prompt():
        if snapshot:
            @pl.when(s == n_p)
            def _():
                hsnap_ref[...] = hp_ref[...]
                bsnap_ref[...] = bufp_ref[...]
        _stream_step(get_x["prompt"], w_refs, lambda: hm[:, 0:batch, :], lambda: bufm[:, 0:batch, :], s == 1,
                     geo["n_meta"] + (s - 1) * geo["p_tile_l"], *outs, hp_ref, bufp_ref, u_p, y_p, cat_p,
                     bblk=batch, tile_l=geo["p_tile_l"], cfg=cfg, precise=False)

    @pl.when(s > n_p)
    def _sample():
        _stream_step(get_x["sample"], w_refs, lambda: h0s_ref[...], lambda: buf0s_ref[...], True, PAST_LEN,
                     *outs, hs_ref, bufs_ref, u_s, y_s, cat_s, bblk=geo["s_bblk"], tile_l=geo["dec_seq"], cfg=cfg,
                     precise=False)


def _mixer_weights(lw):
    return [lw["g_mix"], lw["w_in"], lw["mq"], lw["pmat"], lw["a8"], lw["ssm_d"], lw["w_glu"], lw["b_glu"],
            lw["w_pool"], lw["pool_scale"], lw["w_out"], lw["g_ffn"], lw["w_r"], lw["b_r"]]


def _stream_scratch(tile_l, bblk, d_model, d_ssm):
    return [pltpu.VMEM((tile_l // CHUNK, CHUNK, bblk, d_model), F32),
            pltpu.VMEM((tile_l // CHUNK, CHUNK, bblk, d_ssm), F32),
            pltpu.VMEM(((POOL_HIST + tile_l) * bblk, d_model - d_ssm), F32)]


def _input_specs(x_in, pending, geo, d_model, blk, pblk, sblk):
    g = geo
    rows = g["step_rows"]
    if isinstance(x_in, tuple):
        specs = [pl.BlockSpec((g["batch"], g["p_tile_l"], d_model), lambda s: (0, pblk(s), 0)),
                 pl.BlockSpec((g["s_bblk"], g["dec_seq"], d_model), lambda s: (sblk(s), 0, 0)),
                 pl.BlockSpec(x_in[2].shape, lambda s: (0, 0))]
        return specs, list(x_in), "raw"
    specs = [pl.BlockSpec((rows, d_model), lambda s: (blk(s), 0))]
    if pending is None:
        return specs, [x_in], "flat"
    specs += [pl.BlockSpec((rows // MOE_TILE, g["n_local"], d_model // 2), lambda s: (blk(s), 0, 0)),
              pl.BlockSpec((rows, LANES), lambda s: (blk(s), 0))]
    return specs, [x_in, *pending], "pending"


def _mixer_call(x_in, pending, h0s, buf0s, lw, *, geo, cfg, snapshot):
    d_model = lw["w_in"].shape[0]
    total = (geo["n_p_steps"] + geo["n_s_steps"] + 1) * geo["step_rows"]
    groups, dec_batch, _ = h0s.shape
    d_pool = buf0s.shape[-1]
    d_ssm = d_model - d_pool
    g = geo
    n_p, n_s = g["n_p_steps"], g["n_s_steps"]
    rows = g["step_rows"]
    per_step = rows // MOE_TILE
    n_local = g["n_local"]
    blk = lambda s: jnp.where(s == 0, n_p + n_s, s - 1)
    sblk = lambda s: jnp.clip(s - (n_p + 1), 0, n_s - 1)
    weights = _mixer_weights(lw)

    def full(a):
        nd = a.ndim
        return pl.BlockSpec(a.shape, lambda s, _n=nd: (0,) * _n)

    h_s_spec = pl.BlockSpec((groups, g["s_bblk"], LANES), lambda s: (0, sblk(s), 0))
    buf_s_spec = pl.BlockSpec((POOL_HIST, g["s_bblk"], d_pool), lambda s: (0, sblk(s), 0))
    x_specs, x_args, source = _input_specs(x_in, pending, geo, d_model, blk,
                                           lambda s: jnp.clip(s - 1, 0, n_p - 1), sblk)
    in_specs = x_specs + [h_s_spec, buf_s_spec] + [full(w) for w in weights]
    h_p_shape, buf_p_shape = (groups, g["batch"], LANES), (POOL_HIST, g["batch"], d_pool)
    out_specs = [pl.BlockSpec((rows, d_model), lambda s: (blk(s), 0)),
                 pl.BlockSpec((rows, LANES), lambda s: (blk(s), 0)),
                 pl.BlockSpec((per_step, n_local, d_model // 2), lambda s: (blk(s), 0, 0)),
                 pl.BlockSpec((per_step, 1, LANES), lambda s: (blk(s), 0, 0)),
                 pl.BlockSpec(h_p_shape, lambda s: (0, 0, 0)), pl.BlockSpec(buf_p_shape, lambda s: (0, 0, 0)),
                 h_s_spec, buf_s_spec]
    out_shape = [jax.ShapeDtypeStruct((total, d_model), F32), jax.ShapeDtypeStruct((total, LANES), F32),
                 jax.ShapeDtypeStruct((total // MOE_TILE, n_local, d_model // 2), jnp.uint32),
                 jax.ShapeDtypeStruct((total // MOE_TILE, 1, LANES), F32),
                 jax.ShapeDtypeStruct(h_p_shape, F32), jax.ShapeDtypeStruct(buf_p_shape, F32),
                 jax.ShapeDtypeStruct((groups, dec_batch, LANES), F32),
                 jax.ShapeDtypeStruct((POOL_HIST, dec_batch, d_pool), F32)]
    if snapshot:
        out_specs += [pl.BlockSpec(h_p_shape, lambda s: (0, 0, 0)), pl.BlockSpec(buf_p_shape, lambda s: (0, 0, 0))]
        out_shape += [jax.ShapeDtypeStruct(h_p_shape, F32), jax.ShapeDtypeStruct(buf_p_shape, F32)]
    scratch = ([pltpu.VMEM((groups, g["m_batch"], LANES), F32), pltpu.VMEM((POOL_HIST, g["m_batch"], d_pool), F32)]
               + _stream_scratch(g["n_meta"], g["m_batch"], d_model, d_ssm)
               + _stream_scratch(g["p_tile_l"], g["batch"], d_model, d_ssm)
               + _stream_scratch(g["dec_seq"], g["s_bblk"], d_model, d_ssm))
    return pl.pallas_call(
        functools.partial(_mixer_kernel, cfg=cfg, geo=geo, snapshot=snapshot, source=source),
        grid=(1 + n_p + n_s,),
        in_specs=in_specs,
        out_specs=out_specs,
        out_shape=out_shape,
        scratch_shapes=scratch,
        compiler_params=pltpu.CompilerParams(dimension_semantics=("arbitrary",), vmem_limit_bytes=VMEM_LIMIT),
        name="mixer",
    )(*x_args, h0s, buf0s, *weights)


def _mixer_tail_kernel(*refs, cfg, geo, source):
    get_x, refs = _split_inputs(refs, source, geo)
    get_x = get_x["prompt"]
    h0_ref, buf0_ref = refs[:2]
    w_refs = refs[2:2 + N_MIXER_WEIGHTS]
    rest = refs[2 + N_MIXER_WEIGHTS + 4:]
    x1_ref, route_ref, xloc_ref, cnt_ref, h_ref, buf_ref, u_scr, y_scr, cat_scr = rest
    pos = geo["n_meta"] + (geo["n_p_steps"] - 1) * geo["p_tile_l"]
    _stream_step(get_x, w_refs, lambda: h0_ref[...], lambda: buf0_ref[...], True, pos, x1_ref, route_ref, xloc_ref,
                 cnt_ref, h_ref, buf_ref, u_scr, y_scr, cat_scr, bblk=geo["batch"], tile_l=geo["p_tile_l"], cfg=cfg,
                 precise=True)


def _mixer_tail_call(x_in, pending, h0, buf0, lw32, x1, route, xloc, cnt, *, geo, cfg):
    d_model = x1.shape[-1]
    d_pool = buf0.shape[-1]
    g = geo
    rows = g["step_rows"]
    per_step = rows // MOE_TILE
    last = g["n_p_steps"] - 1
    weights = _mixer_weights(lw32)

    def full(a):
        nd = a.ndim
        return pl.BlockSpec(a.shape, lambda s, _n=nd: (0,) * _n)

    x_specs, x_args, source = _input_specs(x_in, pending, geo, d_model, lambda s: last, lambda s: last, lambda s: 0)
    n_in = len(x_args) + 2 + len(weights)
    return pl.pallas_call(
        functools.partial(_mixer_tail_kernel, cfg=cfg, geo=geo, source=source),
        grid=(1,),
        in_specs=(x_specs + [full(h0), full(buf0)] + [full(w) for w in weights]
                  + [pl.BlockSpec(memory_space=pl.ANY)] * 4),
        out_specs=[pl.BlockSpec((rows, d_model), lambda s: (last, 0)),
                   pl.BlockSpec((rows, LANES), lambda s: (last, 0)),
                   pl.BlockSpec((per_step, g["n_local"], d_model // 2), lambda s: (last, 0, 0)),
                   pl.BlockSpec((per_step, 1, LANES), lambda s: (last, 0, 0)),
                   full(h0), full(buf0)],
        out_shape=[jax.ShapeDtypeStruct(a.shape, a.dtype) for a in (x1, route, xloc, cnt, h0, buf0)],
        scratch_shapes=_stream_scratch(g["p_tile_l"], g["batch"], d_model, d_model - d_pool),
        input_output_aliases={n_in + k: k for k in range(4)},
        compiler_params=pltpu.CompilerParams(dimension_semantics=("arbitrary",), vmem_limit_bytes=VMEM_LIMIT),
        name="mixer_tail",
    )(*x_args, h0, buf0, *weights, x1, route, xloc, cnt)


def _experts_kernel(exp_ref, nact_ref, src_ref, nval_ref, xl_in_ref, wg_ref, wu_ref, wd_ref, rows_ref,
                    xbuf, ybuf, wgu_scr, wd_scr, sem_in, sem_out):
    del xl_in_ref
    j = pl.program_id(0)
    f = wg_ref.shape[-1]
    chunks_per_tile = EXPERT_TILE // SUBLANES
    nact = nact_ref[0]

    def for_tile_chunks(tile, op, scatter):
        slot = tile % 2

        def body(k, carry):
            hbm = rows_ref.at[pl.ds(pl.multiple_of(src_ref[tile * chunks_per_tile + k], SUBLANES), SUBLANES), :]
            local = pl.ds(pl.multiple_of(k * SUBLANES, SUBLANES), SUBLANES)
            if scatter:
                op(pltpu.make_async_copy(ybuf.at[slot, local, :], hbm, sem_out.at[slot]))
            else:
                op(pltpu.make_async_copy(hbm, xbuf.at[slot, local, :], sem_in.at[slot]))
            return carry

        lax.fori_loop(0, nval_ref[tile], body, 0)

    def wait_tile_chunks(tile, scatter):
        slot = tile % 2
        full = nval_ref[tile] == chunks_per_tile

        @pl.when(full)
        def _():
            whole = rows_ref.at[pl.ds(0, EXPERT_TILE), :]
            if scatter:
                pltpu.make_async_copy(ybuf.at[slot], whole, sem_out.at[slot]).wait()
            else:
                pltpu.make_async_copy(whole, xbuf.at[slot], sem_in.at[slot]).wait()

        @pl.when(jnp.logical_not(full))
        def _():
            for_tile_chunks(tile, wait, scatter)

    start = lambda cp: cp.start()
    wait = lambda cp: cp.wait()

    @pl.when(j == 0)
    def _():
        xbuf[...] = jnp.zeros_like(xbuf)
        for_tile_chunks(0, start, False)

    @pl.when(j + 1 < nact)
    def _():
        for_tile_chunks(j + 1, start, False)

    @pl.when(jnp.logical_and(j >= 2, j - 2 < nact))
    def _():
        wait_tile_chunks(j - 2, True)

    active = j < nact
    changed = jnp.logical_or(j == 0, exp_ref[j] != exp_ref[jnp.maximum(j - 1, 0)])

    @pl.when(jnp.logical_and(active, changed))
    def _():
        wgu_scr[:, 0:f] = wg_ref[...].astype(BF16)
        wgu_scr[:, f:2 * f] = wu_ref[...].astype(BF16)
        wd_scr[...] = wd_ref[...].astype(BF16)

    @pl.when(active)
    def _():
        wait_tile_chunks(j, False)
        half = wgu_scr.shape[0] // 2
        t_lo, t_hi = _unpack_halves(xbuf[j % 2])
        gu = _dot(t_lo, wgu_scr[0:half, :]) + _dot(t_hi, wgu_scr[half:2 * half, :])
        act = jax.nn.silu(gu[:, 0:f]) * gu[:, f:2 * f]
        ybuf[j % 2] = _pack_halves(_dot(act.astype(BF16), wd_scr[...]))
        for_tile_chunks(j, start, True)


def _experts_call(tile_exp, n_active, src, nval, xloc, w_gate, w_up, w_down, layer, n_tiles):
    _, n_exp, d_model, f = w_gate.shape
    xl_flat = xloc.reshape(-1, xloc.shape[-1])
    wspec = lambda a, b: pl.BlockSpec((None, None, a, b), lambda i, ex, *_: (layer, ex[i], 0, 0))
    buf = pltpu.VMEM((2, EXPERT_TILE, d_model // 2), jnp.uint32)
    out = pl.pallas_call(
        _experts_kernel,
        grid_spec=pltpu.PrefetchScalarGridSpec(
            num_scalar_prefetch=4,
            grid=(n_tiles + 2,),
            in_specs=[pl.BlockSpec(memory_space=pl.ANY), wspec(d_model, f), wspec(d_model, f), wspec(f, d_model)],
            out_specs=pl.BlockSpec(memory_space=pl.ANY),
            scratch_shapes=[buf, buf, pltpu.VMEM((d_model, 2 * f), BF16), pltpu.VMEM((f, d_model), BF16),
                            pltpu.SemaphoreType.DMA((2,)), pltpu.SemaphoreType.DMA((2,))],
        ),
        out_shape=jax.ShapeDtypeStruct(xl_flat.shape, xl_flat.dtype),
        input_output_aliases={4: 0},
        compiler_params=pltpu.CompilerParams(dimension_semantics=("arbitrary",), vmem_limit_bytes=VMEM_LIMIT),
        name="moe_experts",
    )(tile_exp, n_active, src, nval, xl_flat, w_gate, w_up, w_down)
    return out.reshape(xloc.shape)


def _moe_experts(xloc, cnt, w_gate, w_up, w_down, *, layer):
    n_exp = w_gate.shape[1]
    nt, n_local, _ = xloc.shape
    total = nt * MOE_TILE
    cpt = EXPERT_TILE // SUBLANES
    n_chunks = n_local // SUBLANES
    counts = cnt[:, 0, :n_exp].astype(jnp.int32)
    run = (counts + SUBLANES - 1) // SUBLANES
    lend = jnp.cumsum(run, axis=1)
    loff = lend - run
    gend = jnp.cumsum(run, axis=0)
    gcum = gend - run
    seg = gend[-1]
    seg_tiles = (seg + cpt - 1) // cpt
    seg_end = jnp.cumsum(seg_tiles)
    seg_off = seg_end - seg_tiles
    n_tiles = (2 * total + nt * n_exp * (SUBLANES - 1) + n_exp * (EXPERT_TILE - 1)) // EXPERT_TILE + 1
    n_steps = n_tiles + 2
    n_active = seg_end[-1].reshape(1).astype(jnp.int32)
    tile_idx = jnp.minimum(jnp.arange(n_steps, dtype=jnp.int32), n_active - 1)
    tile_exp = jnp.minimum(jnp.sum((tile_idx[:, None] >= seg_end[None, :]).astype(jnp.int32), axis=1), n_exp - 1)
    is_e = tile_exp[:, None] == jnp.arange(n_exp, dtype=jnp.int32)[None, :]
    pick = lambda table: jnp.sum(jnp.where(is_e[:, None, :], table[None, :, :], 0), axis=2)
    pick1 = lambda vec: jnp.sum(jnp.where(is_e, vec[None, :], 0), axis=1)
    g = ((jnp.arange(n_steps, dtype=jnp.int32) - pick1(seg_off)) * cpt)[:, None] + jnp.arange(cpt, dtype=jnp.int32)
    gend_t, gcum_t, loff_t = pick(gend), pick(gcum), pick(loff)
    owner = jnp.logical_and(g[:, :, None] >= gcum_t[:, None, :], g[:, :, None] < gend_t[:, None, :])
    tile_base = jnp.arange(nt, dtype=jnp.int32) * n_chunks
    src = jnp.sum(jnp.where(owner, (tile_base[None, :] + loff_t - gcum_t)[:, None, :] + g[:, :, None], 0), axis=2)
    src = (src * SUBLANES).reshape(-1)
    nval = jnp.clip(pick1(seg) - g[:, 0], 0, cpt)
    return _experts_call(tile_exp, n_active, src, nval, xloc, w_gate, w_up, w_down, layer, n_tiles)


def _finish_kernel(x1_ref, yl_ref, route_ref, gfin_ref, yp_ref, ys_ref, *, n_p_steps):
    i = pl.program_id(0)
    x = _moe_combine(x1_ref, yl_ref, route_ref)
    ms = jnp.mean(x * x, axis=-1, keepdims=True)
    y = x * lax.rsqrt(ms + EPS) * gfin_ref[...]
    pb, pl_ = yp_ref.shape[0], yp_ref.shape[1]
    sb, sl = ys_ref.shape[0], ys_ref.shape[1]

    @pl.when(i < n_p_steps)
    def _():
        for l in range(pl_):
            yp_ref[:, l, :] = y[l * pb:(l + 1) * pb, :]

    @pl.when(i >= n_p_steps)
    def _():
        for l in range(sl):
            ys_ref[:, l, :] = y[l * sb:(l + 1) * sb, :]


def _finish_call(x1, yl, route, g_final, *, geo):
    d_model = x1.shape[-1]
    g = geo
    rows = g["step_rows"]
    n_p, n_s = g["n_p_steps"], g["n_s_steps"]
    return pl.pallas_call(
        functools.partial(_finish_kernel, n_p_steps=n_p),
        grid=(n_p + n_s,),
        in_specs=[pl.BlockSpec((rows, d_model), lambda i: (i, 0)),
                  pl.BlockSpec((rows // MOE_TILE, g["n_local"], d_model // 2), lambda i: (i, 0, 0)),
                  pl.BlockSpec((rows, LANES), lambda i: (i, 0)),
                  pl.BlockSpec((1, d_model), lambda i: (0, 0))],
        out_specs=[pl.BlockSpec((g["batch"], g["p_tile_l"], d_model), lambda i: (0, jnp.minimum(i, n_p - 1), 0)),
                   pl.BlockSpec((g["s_bblk"], g["dec_seq"], d_model), lambda i: (jnp.clip(i - n_p, 0, n_s - 1), 0, 0))],
        out_shape=[jax.ShapeDtypeStruct((g["batch"], g["seq"], d_model), F32),
                   jax.ShapeDtypeStruct((g["dec_batch"], g["dec_seq"], d_model), F32)],
        compiler_params=pltpu.CompilerParams(dimension_semantics=("arbitrary",), vmem_limit_bytes=VMEM_LIMIT),
        name="finish",
    )(x1, yl, route, g_final)


def kernel(x_prompt, x_sample, state_ssm_re, state_ssm_im, state_pool, meta_tokens, norm_mix_g, w_in, ssm_a_re,
           ssm_a_im, ssm_log_dt, ssm_b_re, ssm_b_im, ssm_c_re, ssm_c_im, ssm_d, w_glu, b_glu, w_pool, pool_scale,
           w_out, norm_ffn_g, w_router_group, b_router_group, w_router_expert, b_router_expert, w_gate, w_up,
           w_down, norm_final_g):
    batch, seq, d_model = x_prompt.shape
    dec_batch, dec_seq, _ = x_sample.shape
    depth, _, groups, nstate = state_ssm_re.shape
    n_meta = meta_tokens.shape[0]
    pool_buf, d_pool = state_pool.shape[2], state_pool.shape[3]
    n_pool_groups = w_pool.shape[1]
    windows = tuple(2 ** (k + 1) for k in range(n_pool_groups))
    assert pool_buf == max(windows) - 1 == POOL_HIST - 1
    n_expert_groups = w_router_group.shape[-1]
    n_experts = w_router_expert.shape[-1]
    cfg = dict(windows=windows, n_expert_groups=n_expert_groups, experts_per_group=n_experts // n_expert_groups)

    step_rows = 512
    p_tile_l, s_bblk, m_batch = step_rows // batch, step_rows // dec_seq, step_rows // n_meta
    n_p, n_s, n_m = batch * seq, dec_batch * dec_seq, m_batch * n_meta
    assert m_batch >= batch and seq % p_tile_l == 0 and dec_batch % s_bblk == 0
    total = n_p + n_s + n_m
    assert total % step_rows == 0 and step_rows % MOE_TILE == 0
    n_local = 2 * MOE_TILE + n_experts * (SUBLANES - 1)
    n_local = (n_local + LANES - 1) // LANES * LANES
    geo = dict(batch=batch, seq=seq, dec_batch=dec_batch, dec_seq=dec_seq, n_meta=n_meta, step_rows=step_rows,
               p_tile_l=p_tile_l, s_bblk=s_bblk, m_batch=m_batch, n_p_steps=seq // p_tile_l,
               n_s_steps=dec_batch // s_bblk, n_local=n_local)

    mq, pmat, a8 = _s5_prepare(ssm_a_re, ssm_a_im, ssm_log_dt, ssm_b_re, ssm_b_im, ssm_c_re, ssm_c_im)

    x = (x_prompt.astype(F32), x_sample.astype(F32), meta_tokens.astype(F32))

    row2 = lambda v: v.reshape(1, -1).astype(F32)
    outs = {k: [] for k in ("hp", "bp", "hs", "bs")}
    pending = None
    for l in range(depth):
        w_r = jnp.concatenate([w_router_group[l], w_router_expert[l]], axis=1).astype(F32)
        w_r = jnp.pad(w_r, ((0, 0), (0, LANES - w_r.shape[1])))
        b_r = jnp.concatenate([b_router_group[l], b_router_expert[l]]).astype(F32)
        lw32 = dict(g_mix=row2(norm_mix_g[l]), w_in=w_in[l].astype(F32), mq=mq[l], pmat=pmat[l], a8=a8[l],
                    ssm_d=row2(ssm_d[l]), w_glu=w_glu[l].astype(F32), b_glu=row2(b_glu[l]),
                    w_pool=w_pool[l].astype(F32), pool_scale=row2(pool_scale[l]), w_out=w_out[l].astype(F32),
                    g_ffn=row2(norm_ffn_g[l]), w_r=w_r,
                    b_r=jnp.pad(b_r, (0, LANES - b_r.shape[0])).reshape(1, LANES))
        lw = dict(lw32, **{k: lw32[k].astype(BF16) for k in ("w_in", "mq", "pmat", "w_glu", "w_pool", "w_out")})

        h0_s = jnp.concatenate([state_ssm_re[l], state_ssm_im[l]], axis=-1).transpose(1, 0, 2).astype(F32)
        buf0_s = jnp.pad(state_pool[l].astype(F32).transpose(1, 0, 2), ((1, 0), (0, 0), (0, 0)))
        redo_last = l < depth - 1
        res = _mixer_call(x, pending, h0_s, buf0_s, lw, geo=geo, cfg=cfg, snapshot=redo_last)
        x1, route, xloc, cnt, h_p, buf_p, h_s, buf_s = res[:8]
        if redo_last:
            x1, route, xloc, cnt, h_p, buf_p = _mixer_tail_call(x, pending, res[8], res[9], lw32, x1, route, xloc,
                                                                 cnt, geo=geo, cfg=cfg)
        outs["hp"].append(h_p)
        outs["bp"].append(buf_p)
        outs["hs"].append(h_s)
        outs["bs"].append(buf_s)
        x, pending = x1, (_moe_experts(xloc, cnt, w_gate, w_up, w_down, layer=l), route)

    y_prompt, y_sample = _finish_call(x, pending[0], pending[1], row2(norm_final_g), geo=geo)
    st = lambda hs: jnp.stack(hs).transpose(0, 2, 1, 3)
    pl_out = lambda bs: jnp.stack(bs)[:, 1:].transpose(0, 2, 1, 3)
    hp, hs = st(outs["hp"]), st(outs["hs"])
    return (y_prompt, y_sample, hp[..., :nstate], hp[..., nstate:], pl_out(outs["bp"]),
            hs[..., :nstate], hs[..., nstate:], pl_out(outs["bs"]))
```

```python
import functools

import numpy as np
import jax
import jax.numpy as jnp
from jax import lax
from jax.experimental import pallas as pl
from jax.experimental.pallas import tpu as pltpu

F32 = jnp.float32
BF16 = jnp.bfloat16

LANES = 128
SUBLANES = 8
CHUNK = 8
POOL_HIST = 16
MOE_TILE = 256
EXPERT_TILE = 512
VMEM_LIMIT = 56 * 1024 * 1024
EPS = 1e-6
NEG = -1e30
PAST_LEN = 16384


def _dot(a, b):
    return jnp.dot(a, b, preferred_element_type=F32)


def _mm(a, w, precise):
    if precise:
        return jnp.dot(a, w, precision=lax.Precision.HIGHEST, preferred_element_type=F32)
    return _dot(a.astype(BF16), w)


def _pack_halves(x):
    k = x.shape[-1] // 2
    return pltpu.pack_elementwise([x[:, :k], x[:, k:]], packed_dtype=BF16)


def _unpack_halves(p):
    lo = pltpu.unpack_elementwise(p, index=0, packed_dtype=BF16, unpacked_dtype=F32)
    hi = pltpu.unpack_elementwise(p, index=1, packed_dtype=BF16, unpacked_dtype=F32)
    return lo.astype(BF16), hi.astype(BF16)


def _dot_nt_f32(a, b):
    return lax.dot_general(a, b, (((1,), (1,)), ((), ())), precision=lax.Precision.HIGHEST,
                           preferred_element_type=F32)


def _s5_prep_kernel(a_re_ref, a_im_ref, ldt_ref, bt_ref, btx_ref, c_ref, cx_ref, mq_ref, p_ref, a8_ref):
    ch = bt_ref.shape[1]
    first_half = lax.broadcasted_iota(jnp.int32, (1, LANES), 1) < LANES // 2
    sgn = jnp.where(first_half, 1.0, -1.0)
    for s in range(CHUNK):
        a_re = a_re_ref[s]
        a_im = a_im_ref[s]
        dt = jnp.exp(ldt_ref[s])
        mag = jnp.exp(dt * a_re)
        abr = mag * jnp.cos(dt * a_im)
        abi = mag * jnp.sin(dt * a_im)
        den = a_re * a_re + a_im * a_im
        nr = abr - 1.0
        f_re = (nr * a_re + abi * a_im) / den
        f_im = (abi * a_re - nr * a_im) / den
        bb = f_re * bt_ref[s] - sgn * f_im * btx_ref[s]
        cc = c_ref[s] * sgn
        ccx = -cx_ref[s]
        bbx = f_re * btx_ref[s] + sgn * f_im * bt_ref[s]
        pr = jnp.ones_like(a_re)
        pi = jnp.zeros_like(a_re)
        e2, w2 = [], []
        for k in range(CHUNK + 1):
            e2.append(pr * cc + pi * ccx)
            w2.append(pr * bb - sgn * pi * bbx)
            if k < CHUNK:
                pr, pi = pr * abr - pi * abi, pr * abi + pi * abr
        a8_ref[s, 0:1, :] = pr
        a8_ref[s, 1:2, :] = -sgn * pi
        a8_ref[s, 2:3, :] = sgn * pi
        zero = jnp.zeros_like(bb)
        jpos = [(p - s) % CHUNK for p in range(CHUNK)]
        for p in range(CHUNK):
            j = jpos[p]
            taps = jnp.concatenate([e2[jp - j] if jp >= j else zero for jp in jpos], axis=0)
            rows = slice(p * ch, (p + 1) * ch)
            mq_ref[s, rows, 0:LANES] = _dot_nt_f32(bb, taps)
            mq_ref[s, rows, LANES:2 * LANES] = w2[CHUNK - 1 - j]
        pt = jnp.concatenate([e2[jp + 1] for jp in jpos], axis=0)
        p_ref[s] = pt.T


def _s5_prepare(a_re, a_im, log_dt, b_re, b_im, c_re, c_im):
    depth, groups, nstate = a_re.shape
    ch = b_re.shape[-1]
    dg = depth * groups
    assert ch * CHUNK == LANES and 2 * nstate == LANES and groups % CHUNK == 0
    dup = lambda x: jnp.concatenate([x, x], axis=-1).reshape(dg, 1, LANES).astype(F32)
    btr = jnp.swapaxes(b_re.reshape(dg, nstate, ch), 1, 2).astype(F32)
    bti = jnp.swapaxes(b_im.reshape(dg, nstate, ch), 1, 2).astype(F32)
    cr = c_re.reshape(dg, ch, nstate).astype(F32)
    ci = c_im.reshape(dg, ch, nstate).astype(F32)
    pair = lambda x, y: jnp.concatenate([x, y], axis=-1)
    blk = lambda *shape: pl.BlockSpec((CHUNK,) + shape, lambda g: (g,) + (0,) * len(shape))
    mq, pmat, a8 = pl.pallas_call(
        _s5_prep_kernel,
        grid=(dg // CHUNK,),
        in_specs=[blk(1, LANES), blk(1, LANES), blk(1, 1)] + [blk(ch, LANES)] * 4,
        out_specs=[blk(LANES, 2 * LANES), blk(LANES, LANES), blk(3, LANES)],
        out_shape=[jax.ShapeDtypeStruct((dg, LANES, 2 * LANES), F32), jax.ShapeDtypeStruct((dg, LANES, LANES), F32),
                   jax.ShapeDtypeStruct((dg, 3, LANES), F32)],
        name="s5_prep",
    )(dup(a_re), dup(a_im), log_dt.reshape(dg, 1, 1).astype(F32), pair(btr, bti), pair(bti, btr),
      pair(cr, ci), pair(ci, cr))
    shp = lambda x: x.reshape((depth, groups) + x.shape[1:])
    return shp(mq), shp(pmat), shp(a8)


N_MIXER_WEIGHTS = 14


def _moe_combine(x1_ref, yl_ref, route_ref):
    rows = x1_ref.shape[0]
    n_local = yl_ref.shape[1]
    lio = lax.broadcasted_iota(jnp.int32, (MOE_TILE, n_local), 1)
    parts = []
    for hh in range(rows // MOE_TILE):
        r = route_ref[hh * MOE_TILE:(hh + 1) * MOE_TILE, :]
        wperm = jnp.where(lio == r[:, 4:5].astype(jnp.int32), r[:, 2:3],
                          jnp.where(lio == r[:, 5:6].astype(jnp.int32), r[:, 3:4], 0.0)).astype(BF16)
        y_lo, y_hi = _unpack_halves(yl_ref[hh])
        parts.append(jnp.concatenate([_dot(wperm, y_lo), _dot(wperm, y_hi)], axis=-1))
    return x1_ref[...] + jnp.concatenate(parts, axis=0)


def _stream_step(get_x, w_refs, h_init, buf_init, first, pos_base, x1_ref, route_ref, xloc_ref, cnt_ref,
                 h_ref, buf_ref, u_scr, y_scr, cat_scr, *, bblk, tile_l, cfg, precise):
    (gmix_ref, win_ref, mq_ref, p_ref, a8_ref, d_ref, wglu_ref, bglu_ref, wpool_ref, pscale_ref, wout_ref,
     gffn_ref, wr_ref, br_ref) = w_refs
    windows = cfg["windows"]
    n_expert_groups = cfg["n_expert_groups"]
    experts_per_group = cfg["experts_per_group"]
    rows = tile_l * bblk
    nk = tile_l // CHUNK
    n = nk * bblk
    d_model = x1_ref.shape[-1]
    d_ssm = y_scr.shape[-1]
    d_pool = d_model - d_ssm
    hist = POOL_HIST * bblk

    def _init():
        h_ref[...] = h_init()
        cat_scr[0:hist, :] = buf_init().reshape(hist, d_pool)

    if first is True:
        _init()
    else:
        pl.when(first)(_init)

    x = get_x()
    ms = jnp.mean(x * x, axis=-1, keepdims=True)
    h = x * lax.rsqrt(ms + EPS) * gmix_ref[...]
    u = _mm(h, win_ref[...], precise)
    u_scr[...] = u.reshape(nk, CHUNK, bblk, d_model)

    slot = lax.broadcasted_iota(jnp.int32, (n, LANES), 1) // (LANES // CHUNK)
    for v in range(d_ssm // LANES):
        rolled = []
        for j in range(CHUNK):
            uj = u_scr[:, j, :, v * LANES:(v + 1) * LANES].reshape(n, LANES)
            rolled.append(pltpu.roll(uj, (LANES // CHUNK) * j, axis=1) if j else uj)
        ys = []
        for s in range(CHUNK):
            g = v * CHUNK + s
            ug = rolled[0]
            for j in range(1, CHUNK):
                ug = jnp.where(slot == (s + j) % CHUNK, rolled[j], ug)
            yq = _mm(ug, mq_ref[g], precise)
            sg = yq[:, LANES:]
            sg_sw = pltpu.roll(sg, LANES // 2, axis=1)
            a8c = a8_ref[g, 0:1, :]
            a8s = a8_ref[g, 1:2, :]
            a8w = a8_ref[g, 2:3, :]
            hcur = h_ref[g]
            hsw = pltpu.roll(hcur, LANES // 2, axis=1)
            hins = []
            for k in range(nk):
                hins.append(hcur)
                sk = sg[k * bblk:(k + 1) * bblk]
                skw = sg_sw[k * bblk:(k + 1) * bblk]
                hcur, hsw = a8c * hcur + a8s * hsw + sk, a8c * hsw + a8w * hcur + skw
            h_ref[g] = hcur
            hin = jnp.concatenate(hins, axis=0) if nk > 1 else hins[0]
            ys.append(yq[:, :LANES] + _mm(hin, p_ref[g], precise))
        for j in range(CHUNK):
            z = ys[0]
            for s in range(1, CHUNK):
                z = jnp.where(slot == (s + j) % CHUNK, ys[s], z)
            if j:
                z = pltpu.roll(z, LANES - (LANES // CHUNK) * j, axis=1)
            y_scr[:, j, :, v * LANES:(v + 1) * LANES] = z.reshape(nk, bblk, LANES)

    u_ssm = u_scr[:, :, :, 0:d_ssm].reshape(rows, d_ssm)
    y = y_scr[...].reshape(rows, d_ssm) + d_ref[...] * u_ssm
    gl = jax.nn.gelu(y)
    ssm_out = gl * jax.nn.sigmoid(_mm(gl, wglu_ref[...], precise) + bglu_ref[...])

    up = u_scr[:, :, :, d_ssm:].reshape(rows, d_pool)
    cat_scr[hist:hist + rows, :] = up
    row_pos = lax.broadcasted_iota(jnp.int32, (rows, 1), 0) // bblk
    pos = pos_base + row_pos
    pool_parts = []
    pgrp = d_pool // len(windows)
    for kk, w in enumerate(windows):
        lo = kk * pgrp
        acc = cat_scr[hist:hist + rows, lo:lo + pgrp]
        cur = acc
        for sft in range(1, w):
            acc = acc + cat_scr[hist - sft * bblk:hist - sft * bblk + rows, lo:lo + pgrp]
        inv = 1.0 / jnp.minimum(pos + 1, w).astype(F32)
        diff = acc * inv - cur
        pool_parts.append(_mm(diff, wpool_ref[kk], precise))
    pool_out = jnp.concatenate(pool_parts, axis=-1) * pscale_ref[...]
    tail = cat_scr[rows:rows + hist, :]
    cat_scr[0:hist, :] = tail
    buf_ref[...] = tail.reshape(POOL_HIST, bblk, d_pool)

    if precise:
        mix = jnp.concatenate([ssm_out, pool_out], axis=-1)
    else:
        mix = jnp.concatenate([ssm_out.astype(BF16), pool_out.astype(BF16)], axis=-1)
    x1 = x + _mm(mix, wout_ref[...], precise)
    x1_ref[...] = x1
    ms2 = jnp.mean(x1 * x1, axis=-1, keepdims=True)
    t = x1 * lax.rsqrt(ms2 + EPS) * gffn_ref[...]

    t_hi = t.astype(BF16)
    t_lo = (t - t_hi.astype(F32)).astype(BF16)
    w_r = wr_ref[...]
    w_hi = w_r.astype(BF16)
    w_lo = (w_r - w_hi.astype(F32)).astype(BF16)
    hi_both = _dot(t_hi, jnp.concatenate([w_hi, w_lo], axis=-1))
    logits = hi_both[:, :LANES] + hi_both[:, LANES:] + _dot(t_lo, w_hi) + br_ref[...]
    lane = lax.broadcasted_iota(jnp.int32, (rows, LANES), 1).astype(F32)
    big = float(4 * LANES)
    lg = jnp.where(lane < n_expert_groups, logits, NEG)
    mg = jnp.max(lg, axis=-1, keepdims=True)
    gsel = jnp.min(jnp.where(lg == mg, lane, big), axis=-1, keepdims=True)
    p_sel = 1.0 / jnp.sum(jnp.exp(lg - mg), axis=-1, keepdims=True)
    e_lo = n_expert_groups + gsel * experts_per_group
    le = jnp.where((lane >= e_lo) & (lane < e_lo + experts_per_group), logits, NEG)
    v1 = jnp.max(le, axis=-1, keepdims=True)
    i1 = jnp.min(jnp.where(le == v1, lane, big), axis=-1, keepdims=True)
    le2 = jnp.where(lane == i1, NEG, le)
    v2 = jnp.max(le2, axis=-1, keepdims=True)
    i2 = jnp.min(jnp.where(le2 == v2, lane, big), axis=-1, keepdims=True)
    ex = jnp.exp(v2 - v1)
    q1 = p_sel / (1.0 + ex)
    q2 = q1 * ex
    e1 = i1 - n_expert_groups
    e2 = i2 - n_expert_groups
    route = jnp.where(lane == 0.0, e1, jnp.where(lane == 1.0, e2, jnp.where(lane == 2.0, q1,
                      jnp.where(lane == 3.0, q2, 0.0))))

    lf = lax.broadcasted_iota(jnp.int32, (MOE_TILE, LANES), 1).astype(F32)
    n_local = xloc_ref.shape[1]
    before_e = (lax.broadcasted_iota(jnp.int32, (LANES, LANES), 0)
                < lax.broadcasted_iota(jnp.int32, (LANES, LANES), 1)).astype(BF16)
    earlier = (lax.broadcasted_iota(jnp.int32, (MOE_TILE, MOE_TILE), 0)
               > lax.broadcasted_iota(jnp.int32, (MOE_TILE, MOE_TILE), 1)).astype(BF16)
    local_row = lax.broadcasted_iota(jnp.int32, (n_local, MOE_TILE), 0)
    route_parts = []
    for hh in range(rows // MOE_TILE):
        sl = slice(hh * MOE_TILE, (hh + 1) * MOE_TILE)
        rt = route[sl]
        oh1 = lf == rt[:, 0:1]
        oh2 = lf == rt[:, 1:2]
        oh = jnp.logical_or(oh1, oh2).astype(F32)
        cnt = jnp.sum(oh, axis=0, keepdims=True)
        run_chunks = jnp.floor((cnt + (SUBLANES - 1)) * (1.0 / SUBLANES))
        loff = SUBLANES * _dot(jnp.broadcast_to(run_chunks, (SUBLANES, LANES)).astype(BF16), before_e)[0:1]
        where_to = loff + _dot(earlier, oh.astype(BF16))
        lpos1 = jnp.sum(jnp.where(oh1, where_to, 0.0), axis=-1, keepdims=True)
        lpos2 = jnp.sum(jnp.where(oh2, where_to, 0.0), axis=-1, keepdims=True)
        lpos_t = jnp.where(lf == 0.0, lpos1, jnp.where(lf == 1.0, lpos2, 0.0)).T
        perm = jnp.logical_or(local_row == lpos_t[0:1].astype(jnp.int32),
                              local_row == lpos_t[1:2].astype(jnp.int32)).astype(BF16)
        xloc_ref[hh] = _pack_halves(_dot(perm, t_hi[sl]))
        cnt_ref[hh] = cnt
        route_parts.append(jnp.where(lf == 4.0, lpos1, jnp.where(lf == 5.0, lpos2, rt)))
    route_ref[...] = jnp.concatenate(route_parts, axis=0)


def _split_inputs(refs, source, geo):
    if source == "raw":
        xp_ref, xs_ref, meta_ref = refs[:3]
        by_position = lambda ref: (lambda: jnp.concatenate([ref[:, l, :] for l in range(ref.shape[1])], axis=0))
        meta = lambda: jnp.concatenate([jnp.broadcast_to(meta_ref[l:l + 1, :], (geo["m_batch"], meta_ref.shape[1]))
                                        for l in range(meta_ref.shape[0])], axis=0)
        return dict(meta=meta, prompt=by_position(xp_ref), sample=by_position(xs_ref)), refs[3:]
    if source == "pending":
        x_ref, yl_ref, rprev_ref = refs[:3]
        get, refs = (lambda: _moe_combine(x_ref, yl_ref, rprev_ref)), refs[3:]
    else:
        x_ref = refs[0]
        get, refs = (lambda: x_ref[...]), refs[1:]
    return dict(meta=get, prompt=get, sample=get), refs


def _mixer_kernel(*refs, cfg, geo, snapshot, source):
    get_x, refs = _split_inputs(refs, source, geo)
    h0s_ref, buf0s_ref = refs[:2]
    w_refs = refs[2:2 + N_MIXER_WEIGHTS]
    rest = refs[2 + N_MIXER_WEIGHTS:]
    x1_ref, route_ref, xloc_ref, cnt_ref, hp_ref, bufp_ref, hs_ref, bufs_ref = rest[:8]
    rest = rest[8:]
    if snapshot:
        hsnap_ref, bsnap_ref = rest[:2]
        rest = rest[2:]
    hm, bufm, u_m, y_m, cat_m, u_p, y_p, cat_p, u_s, y_s, cat_s = rest
    outs = (x1_ref, route_ref, xloc_ref, cnt_ref)
    s = pl.program_id(0)
    n_p = geo["n_p_steps"]
    batch = geo["batch"]

    @pl.when(s == 0)
    def _meta():
        _stream_step(get_x["meta"], w_refs, lambda: jnp.zeros(hm.shape, F32), lambda: jnp.zeros(bufm.shape, F32), True, 0,
                     *outs, hm, bufm, u_m, y_m, cat_m, bblk=geo["m_batch"], tile_l=geo["n_meta"], cfg=cfg,
                     precise=False)

    @pl.when(jnp.logical_and(s >= 1, s <= n_p))
    def _prompt():
        if snapshot:
            @pl.when(s == n_p)
            def _():
                hsnap_ref[...] = hp_ref[...]
                bsnap_ref[...] = bufp_ref[...]
        _stream_step(get_x["prompt"], w_refs, lambda: hm[:, 0:batch, :], lambda: bufm[:, 0:batch, :], s == 1,
                     geo["n_meta"] + (s - 1) * geo["p_tile_l"], *outs, hp_ref, bufp_ref, u_p, y_p, cat_p,
                     bblk=batch, tile_l=geo["p_tile_l"], cfg=cfg, precise=False)

    @pl.when(s > n_p)
    def _sample():
        _stream_step(get_x["sample"], w_refs, lambda: h0s_ref[...], lambda: buf0s_ref[...], True, PAST_LEN,
                     *outs, hs_ref, bufs_ref, u_s, y_s, cat_s, bblk=geo["s_bblk"], tile_l=geo["dec_seq"], cfg=cfg,
                     precise=False)


def _mixer_weights(lw):
    return [lw["g_mix"], lw["w_in"], lw["mq"], lw["pmat"], lw["a8"], lw["ssm_d"], lw["w_glu"], lw["b_glu"],
            lw["w_pool"], lw["pool_scale"], lw["w_out"], lw["g_ffn"], lw["w_r"], lw["b_r"]]


def _stream_scratch(tile_l, bblk, d_model, d_ssm):
    return [pltpu.VMEM((tile_l // CHUNK, CHUNK, bblk, d_model), F32),
            pltpu.VMEM((tile_l // CHUNK, CHUNK, bblk, d_ssm), F32),
            pltpu.VMEM(((POOL_HIST + tile_l) * bblk, d_model - d_ssm), F32)]


def _input_specs(x_in, pending, geo, d_model, blk, pblk, sblk):
    g = geo
    rows = g["step_rows"]
    if isinstance(x_in, tuple):
        specs = [pl.BlockSpec((g["batch"], g["p_tile_l"], d_model), lambda s: (0, pblk(s), 0)),
                 pl.BlockSpec((g["s_bblk"], g["dec_seq"], d_model), lambda s: (sblk(s), 0, 0)),
                 pl.BlockSpec(x_in[2].shape, lambda s: (0, 0))]
        return specs, list(x_in), "raw"
    specs = [pl.BlockSpec((rows, d_model), lambda s: (blk(s), 0))]
    if pending is None:
        return specs, [x_in], "flat"
    specs += [pl.BlockSpec((rows // MOE_TILE, g["n_local"], d_model // 2), lambda s: (blk(s), 0, 0)),
              pl.BlockSpec((rows, LANES), lambda s: (blk(s), 0))]
    return specs, [x_in, *pending], "pending"


def _mixer_call(x_in, pending, h0s, buf0s, lw, *, geo, cfg, snapshot):
    d_model = lw["w_in"].shape[0]
    total = (geo["n_p_steps"] + geo["n_s_steps"] + 1) * geo["step_rows"]
    groups, dec_batch, _ = h0s.shape
    d_pool = buf0s.shape[-1]
    d_ssm = d_model - d_pool
    g = geo
    n_p, n_s = g["n_p_steps"], g["n_s_steps"]
    rows = g["step_rows"]
    per_step = rows // MOE_TILE
    n_local = g["n_local"]
    blk = lambda s: jnp.where(s == 0, n_p + n_s, s - 1)
    sblk = lambda s: jnp.clip(s - (n_p + 1), 0, n_s - 1)
    weights = _mixer_weights(lw)

    def full(a):
        nd = a.ndim
        return pl.BlockSpec(a.shape, lambda s, _n=nd: (0,) * _n)

    h_s_spec = pl.BlockSpec((groups, g["s_bblk"], LANES), lambda s: (0, sblk(s), 0))
    buf_s_spec = pl.BlockSpec((POOL_HIST, g["s_bblk"], d_pool), lambda s: (0, sblk(s), 0))
    x_specs, x_args, source = _input_specs(x_in, pending, geo, d_model, blk,
                                           lambda s: jnp.clip(s - 1, 0, n_p - 1), sblk)
    in_specs = x_specs + [h_s_spec, buf_s_spec] + [full(w) for w in weights]
    h_p_shape, buf_p_shape = (groups, g["batch"], LANES), (POOL_HIST, g["batch"], d_pool)
    out_specs = [pl.BlockSpec((rows, d_model), lambda s: (blk(s), 0)),
                 pl.BlockSpec((rows, LANES), lambda s: (blk(s), 0)),
                 pl.BlockSpec((per_step, n_local, d_model // 2), lambda s: (blk(s), 0, 0)),
                 pl.BlockSpec((per_step, 1, LANES), lambda s: (blk(s), 0, 0)),
                 pl.BlockSpec(h_p_shape, lambda s: (0, 0, 0)), pl.BlockSpec(buf_p_shape, lambda s: (0, 0, 0)),
                 h_s_spec, buf_s_spec]
    out_shape = [jax.ShapeDtypeStruct((total, d_model), F32), jax.ShapeDtypeStruct((total, LANES), F32),
                 jax.ShapeDtypeStruct((total // MOE_TILE, n_local, d_model // 2), jnp.uint32),
                 jax.ShapeDtypeStruct((total // MOE_TILE, 1, LANES), F32),
                 jax.ShapeDtypeStruct(h_p_shape, F32), jax.ShapeDtypeStruct(buf_p_shape, F32),
                 jax.ShapeDtypeStruct((groups, dec_batch, LANES), F32),
                 jax.ShapeDtypeStruct((POOL_HIST, dec_batch, d_pool), F32)]
    if snapshot:
        out_specs += [pl.BlockSpec(h_p_shape, lambda s: (0, 0, 0)), pl.BlockSpec(buf_p_shape, lambda s: (0, 0, 0))]
        out_shape += [jax.ShapeDtypeStruct(h_p_shape, F32), jax.ShapeDtypeStruct(buf_p_shape, F32)]
    scratch = ([pltpu.VMEM((groups, g["m_batch"], LANES), F32), pltpu.VMEM((POOL_HIST, g["m_batch"], d_pool), F32)]
               + _stream_scratch(g["n_meta"], g["m_batch"], d_model, d_ssm)
               + _stream_scratch(g["p_tile_l"], g["batch"], d_model, d_ssm)
               + _stream_scratch(g["dec_seq"], g["s_bblk"], d_model, d_ssm))
    return pl.pallas_call(
        functools.partial(_mixer_kernel, cfg=cfg, geo=geo, snapshot=snapshot, source=source),
        grid=(1 + n_p + n_s,),
        in_specs=in_specs,
        out_specs=out_specs,
        out_shape=out_shape,
        scratch_shapes=scratch,
        compiler_params=pltpu.CompilerParams(dimension_semantics=("arbitrary",), vmem_limit_bytes=VMEM_LIMIT),
        name="mixer",
    )(*x_args, h0s, buf0s, *weights)


def _mixer_tail_kernel(*refs, cfg, geo, source):
    get_x, refs = _split_inputs(refs, source, geo)
    get_x = get_x["prompt"]
    h0_ref, buf0_ref = refs[:2]
    w_refs = refs[2:2 + N_MIXER_WEIGHTS]
    rest = refs[2 + N_MIXER_WEIGHTS + 4:]
    x1_ref, route_ref, xloc_ref, cnt_ref, h_ref, buf_ref, u_scr, y_scr, cat_scr = rest
    pos = geo["n_meta"] + (geo["n_p_steps"] - 1) * geo["p_tile_l"]
    _stream_step(get_x, w_refs, lambda: h0_ref[...], lambda: buf0_ref[...], True, pos, x1_ref, route_ref, xloc_ref,
                 cnt_ref, h_ref, buf_ref, u_scr, y_scr, cat_scr, bblk=geo["batch"], tile_l=geo["p_tile_l"], cfg=cfg,
                 precise=True)


def _mixer_tail_call(x_in, pending, h0, buf0, lw32, x1, route, xloc, cnt, *, geo, cfg):
    d_model = x1.shape[-1]
    d_pool = buf0.shape[-1]
    g = geo
    rows = g["step_rows"]
    per_step = rows // MOE_TILE
    last = g["n_p_steps"] - 1
    weights = _mixer_weights(lw32)

    def full(a):
        nd = a.ndim
        return pl.BlockSpec(a.shape, lambda s, _n=nd: (0,) * _n)

    x_specs, x_args, source = _input_specs(x_in, pending, geo, d_model, lambda s: last, lambda s: last, lambda s: 0)
    n_in = len(x_args) + 2 + len(weights)
    return pl.pallas_call(
        functools.partial(_mixer_tail_kernel, cfg=cfg, geo=geo, source=source),
        grid=(1,),
        in_specs=(x_specs + [full(h0), full(buf0)] + [full(w) for w in weights]
                  + [pl.BlockSpec(memory_space=pl.ANY)] * 4),
        out_specs=[pl.BlockSpec((rows, d_model), lambda s: (last, 0)),
                   pl.BlockSpec((rows, LANES), lambda s: (last, 0)),
                   pl.BlockSpec((per_step, g["n_local"], d_model // 2), lambda s: (last, 0, 0)),
                   pl.BlockSpec((per_step, 1, LANES), lambda s: (last, 0, 0)),
                   full(h0), full(buf0)],
        out_shape=[jax.ShapeDtypeStruct(a.shape, a.dtype) for a in (x1, route, xloc, cnt, h0, buf0)],
        scratch_shapes=_stream_scratch(g["p_tile_l"], g["batch"], d_model, d_model - d_pool),
        input_output_aliases={n_in + k: k for k in range(4)},
        compiler_params=pltpu.CompilerParams(dimension_semantics=("arbitrary",), vmem_limit_bytes=VMEM_LIMIT),
        name="mixer_tail",
    )(*x_args, h0, buf0, *weights, x1, route, xloc, cnt)


def _experts_kernel(exp_ref, nact_ref, src_ref, nval_ref, xl_in_ref, wg_ref, wu_ref, wd_ref, rows_ref,
                    xbuf, ybuf, wgu_scr, wd_scr, sem_in, sem_out):
    del xl_in_ref
    j = pl.program_id(0)
    f = wg_ref.shape[-1]
    chunks_per_tile = EXPERT_TILE // SUBLANES
    nact = nact_ref[0]

    def for_tile_chunks(tile, op, scatter):
        slot = tile % 2

        def body(k, carry):
            hbm = rows_ref.at[pl.ds(pl.multiple_of(src_ref[tile * chunks_per_tile + k], SUBLANES), SUBLANES), :]
            local = pl.ds(pl.multiple_of(k * SUBLANES, SUBLANES), SUBLANES)
            if scatter:
                op(pltpu.make_async_copy(ybuf.at[slot, local, :], hbm, sem_out.at[slot]))
            else:
                op(pltpu.make_async_copy(hbm, xbuf.at[slot, local, :], sem_in.at[slot]))
            return carry

        lax.fori_loop(0, nval_ref[tile], body, 0)

    def wait_tile_chunks(tile, scatter):
        slot = tile % 2
        full = nval_ref[tile] == chunks_per_tile

        @pl.when(full)
        def _():
            whole = rows_ref.at[pl.ds(0, EXPERT_TILE), :]
            if scatter:
                pltpu.make_async_copy(ybuf.at[slot], whole, sem_out.at[slot]).wait()
            else:
                pltpu.make_async_copy(whole, xbuf.at[slot], sem_in.at[slot]).wait()

        @pl.when(jnp.logical_not(full))
        def _():
            for_tile_chunks(tile, wait, scatter)

    start = lambda cp: cp.start()
    wait = lambda cp: cp.wait()

    @pl.when(j == 0)
    def _():
        xbuf[...] = jnp.zeros_like(xbuf)
        for_tile_chunks(0, start, False)

    @pl.when(j + 1 < nact)
    def _():
        for_tile_chunks(j + 1, start, False)

    @pl.when(jnp.logical_and(j >= 2, j - 2 < nact))
    def _():
        wait_tile_chunks(j - 2, True)

    active = j < nact
    changed = jnp.logical_or(j == 0, exp_ref[j] != exp_ref[jnp.maximum(j - 1, 0)])

    @pl.when(jnp.logical_and(active, changed))
    def _():
        wgu_scr[:, 0:f] = wg_ref[...].astype(BF16)
        wgu_scr[:, f:2 * f] = wu_ref[...].astype(BF16)
        wd_scr[...] = wd_ref[...].astype(BF16)

    @pl.when(active)
    def _():
        wait_tile_chunks(j, False)
        half = wgu_scr.shape[0] // 2
        t_lo, t_hi = _unpack_halves(xbuf[j % 2])
        gu = _dot(t_lo, wgu_scr[0:half, :]) + _dot(t_hi, wgu_scr[half:2 * half, :])
        act = jax.nn.silu(gu[:, 0:f]) * gu[:, f:2 * f]
        ybuf[j % 2] = _pack_halves(_dot(act.astype(BF16), wd_scr[...]))
        for_tile_chunks(j, start, True)


def _experts_call(tile_exp, n_active, src, nval, xloc, w_gate, w_up, w_down, layer, n_tiles):
    _, n_exp, d_model, f = w_gate.shape
    xl_flat = xloc.reshape(-1, xloc.shape[-1])
    wspec = lambda a, b: pl.BlockSpec((None, None, a, b), lambda i, ex, *_: (layer, ex[i], 0, 0))
    buf = pltpu.VMEM((2, EXPERT_TILE, d_model // 2), jnp.uint32)
    out = pl.pallas_call(
        _experts_kernel,
        grid_spec=pltpu.PrefetchScalarGridSpec(
            num_scalar_prefetch=4,
            grid=(n_tiles + 2,),
            in_specs=[pl.BlockSpec(memory_space=pl.ANY), wspec(d_model, f), wspec(d_model, f), wspec(f, d_model)],
            out_specs=pl.BlockSpec(memory_space=pl.ANY),
            scratch_shapes=[buf, buf, pltpu.VMEM((d_model, 2 * f), BF16), pltpu.VMEM((f, d_model), BF16),
                            pltpu.SemaphoreType.DMA((2,)), pltpu.SemaphoreType.DMA((2,))],
        ),
        out_shape=jax.ShapeDtypeStruct(xl_flat.shape, xl_flat.dtype),
        input_output_aliases={4: 0},
        compiler_params=pltpu.CompilerParams(dimension_semantics=("arbitrary",), vmem_limit_bytes=VMEM_LIMIT),
        name="moe_experts",
    )(tile_exp, n_active, src, nval, xl_flat, w_gate, w_up, w_down)
    return out.reshape(xloc.shape)


def _moe_experts(xloc, cnt, w_gate, w_up, w_down, *, layer):
    n_exp = w_gate.shape[1]
    nt, n_local, _ = xloc.shape
    total = nt * MOE_TILE
    cpt = EXPERT_TILE // SUBLANES
    n_chunks = n_local // SUBLANES
    counts = cnt[:, 0, :n_exp].astype(jnp.int32)
    run = (counts + SUBLANES - 1) // SUBLANES
    lend = jnp.cumsum(run, axis=1)
    loff = lend - run
    gend = jnp.cumsum(run, axis=0)
    gcum = gend - run
    seg = gend[-1]
    seg_tiles = (seg + cpt - 1) // cpt
    seg_end = jnp.cumsum(seg_tiles)
    seg_off = seg_end - seg_tiles
    n_tiles = (2 * total + nt * n_exp * (SUBLANES - 1) + n_exp * (EXPERT_TILE - 1)) // EXPERT_TILE + 1
    n_steps = n_tiles + 2
    n_active = seg_end[-1].reshape(1).astype(jnp.int32)
    tile_idx = jnp.minimum(jnp.arange(n_steps, dtype=jnp.int32), n_active - 1)
    tile_exp = jnp.minimum(jnp.sum((tile_idx[:, None] >= seg_end[None, :]).astype(jnp.int32), axis=1), n_exp - 1)
    is_e = tile_exp[:, None] == jnp.arange(n_exp, dtype=jnp.int32)[None, :]
    pick = lambda table: jnp.sum(jnp.where(is_e[:, None, :], table[None, :, :], 0), axis=2)
    pick1 = lambda vec: jnp.sum(jnp.where(is_e, vec[None, :], 0), axis=1)
    g = ((jnp.arange(n_steps, dtype=jnp.int32) - pick1(seg_off)) * cpt)[:, None] + jnp.arange(cpt, dtype=jnp.int32)
    gend_t, gcum_t, loff_t = pick(gend), pick(gcum), pick(loff)
    owner = jnp.logical_and(g[:, :, None] >= gcum_t[:, None, :], g[:, :, None] < gend_t[:, None, :])
    tile_base = jnp.arange(nt, dtype=jnp.int32) * n_chunks
    src = jnp.sum(jnp.where(owner, (tile_base[None, :] + loff_t - gcum_t)[:, None, :] + g[:, :, None], 0), axis=2)
    src = (src * SUBLANES).reshape(-1)
    nval = jnp.clip(pick1(seg) - g[:, 0], 0, cpt)
    return _experts_call(tile_exp, n_active, src, nval, xloc, w_gate, w_up, w_down, layer, n_tiles)


def _finish_kernel(x1_ref, yl_ref, route_ref, gfin_ref, yp_ref, ys_ref, *, n_p_steps):
    i = pl.program_id(0)
    x = _moe_combine(x1_ref, yl_ref, route_ref)
    ms = jnp.mean(x * x, axis=-1, keepdims=True)
    y = x * lax.rsqrt(ms + EPS) * gfin_ref[...]
    pb, pl_ = yp_ref.shape[0], yp_ref.shape[1]
    sb, sl = ys_ref.shape[0], ys_ref.shape[1]

    @pl.when(i < n_p_steps)
    def _():
        for l in range(pl_):
            yp_ref[:, l, :] = y[l * pb:(l + 1) * pb, :]

    @pl.when(i >= n_p_steps)
    def _():
        for l in range(sl):
            ys_ref[:, l, :] = y[l * sb:(l + 1) * sb, :]


def _finish_call(x1, yl, route, g_final, *, geo):
    d_model = x1.shape[-1]
    g = geo
    rows = g["step_rows"]
    n_p, n_s = g["n_p_steps"], g["n_s_steps"]
    return pl.pallas_call(
        functools.partial(_finish_kernel, n_p_steps=n_p),
        grid=(n_p + n_s,),
        in_specs=[pl.BlockSpec((rows, d_model), lambda i: (i, 0)),
                  pl.BlockSpec((rows // MOE_TILE, g["n_local"], d_model // 2), lambda i: (i, 0, 0)),
                  pl.BlockSpec((rows, LANES), lambda i: (i, 0)),
                  pl.BlockSpec((1, d_model), lambda i: (0, 0))],
        out_specs=[pl.BlockSpec((g["batch"], g["p_tile_l"], d_model), lambda i: (0, jnp.minimum(i, n_p - 1), 0)),
                   pl.BlockSpec((g["s_bblk"], g["dec_seq"], d_model), lambda i: (jnp.clip(i - n_p, 0, n_s - 1), 0, 0))],
        out_shape=[jax.ShapeDtypeStruct((g["batch"], g["seq"], d_model), F32),
                   jax.ShapeDtypeStruct((g["dec_batch"], g["dec_seq"], d_model), F32)],
        compiler_params=pltpu.CompilerParams(dimension_semantics=("arbitrary",), vmem_limit_bytes=VMEM_LIMIT),
        name="finish",
    )(x1, yl, route, g_final)


def kernel(x_prompt, x_sample, state_ssm_re, state_ssm_im, state_pool, meta_tokens, norm_mix_g, w_in, ssm_a_re,
           ssm_a_im, ssm_log_dt, ssm_b_re, ssm_b_im, ssm_c_re, ssm_c_im, ssm_d, w_glu, b_glu, w_pool, pool_scale,
           w_out, norm_ffn_g, w_router_group, b_router_group, w_router_expert, b_router_expert, w_gate, w_up,
           w_down, norm_final_g):
    batch, seq, d_model = x_prompt.shape
    dec_batch, dec_seq, _ = x_sample.shape
    depth, _, groups, nstate = state_ssm_re.shape
    n_meta = meta_tokens.shape[0]
    pool_buf, d_pool = state_pool.shape[2], state_pool.shape[3]
    n_pool_groups = w_pool.shape[1]
    windows = tuple(2 ** (k + 1) for k in range(n_pool_groups))
    assert pool_buf == max(windows) - 1 == POOL_HIST - 1
    n_expert_groups = w_router_group.shape[-1]
    n_experts = w_router_expert.shape[-1]
    cfg = dict(windows=windows, n_expert_groups=n_expert_groups, experts_per_group=n_experts // n_expert_groups)

    step_rows = 512
    p_tile_l, s_bblk, m_batch = step_rows // batch, step_rows // dec_seq, step_rows // n_meta
    n_p, n_s, n_m = batch * seq, dec_batch * dec_seq, m_batch * n_meta
    assert m_batch >= batch and seq % p_tile_l == 0 and dec_batch % s_bblk == 0
    total = n_p + n_s + n_m
    assert total % step_rows == 0 and step_rows % MOE_TILE == 0
    n_local = 2 * MOE_TILE + n_experts * (SUBLANES - 1)
    n_local = (n_local + LANES - 1) // LANES * LANES
    geo = dict(batch=batch, seq=seq, dec_batch=dec_batch, dec_seq=dec_seq, n_meta=n_meta, step_rows=step_rows,
               p_tile_l=p_tile_l, s_bblk=s_bblk, m_batch=m_batch, n_p_steps=seq // p_tile_l,
               n_s_steps=dec_batch // s_bblk, n_local=n_local)

    mq, pmat, a8 = _s5_prepare(ssm_a_re, ssm_a_im, ssm_log_dt, ssm_b_re, ssm_b_im, ssm_c_re, ssm_c_im)

    x = (x_prompt.astype(F32), x_sample.astype(F32), meta_tokens.astype(F32))

    row2 = lambda v: v.reshape(1, -1).astype(F32)
    outs = {k: [] for k in ("hp", "bp", "hs", "bs")}
    pending = None
    for l in range(depth):
        w_r = jnp.concatenate([w_router_group[l], w_router_expert[l]], axis=1).astype(F32)
        w_r = jnp.pad(w_r, ((0, 0), (0, LANES - w_r.shape[1])))
        b_r = jnp.concatenate([b_router_group[l], b_router_expert[l]]).astype(F32)
        lw32 = dict(g_mix=row2(norm_mix_g[l]), w_in=w_in[l].astype(F32), mq=mq[l], pmat=pmat[l], a8=a8[l],
                    ssm_d=row2(ssm_d[l]), w_glu=w_glu[l].astype(F32), b_glu=row2(b_glu[l]),
                    w_pool=w_pool[l].astype(F32), pool_scale=row2(pool_scale[l]), w_out=w_out[l].astype(F32),
                    g_ffn=row2(norm_ffn_g[l]), w_r=w_r,
                    b_r=jnp.pad(b_r, (0, LANES - b_r.shape[0])).reshape(1, LANES))
        lw = dict(lw32, **{k: lw32[k].astype(BF16) for k in ("w_in", "mq", "pmat", "w_glu", "w_pool", "w_out")})

        h0_s = jnp.concatenate([state_ssm_re[l], state_ssm_im[l]], axis=-1).transpose(1, 0, 2).astype(F32)
        buf0_s = jnp.pad(state_pool[l].astype(F32).transpose(1, 0, 2), ((1, 0), (0, 0), (0, 0)))
        redo_last = l < depth - 1
        res = _mixer_call(x, pending, h0_s, buf0_s, lw, geo=geo, cfg=cfg, snapshot=redo_last)
        x1, route, xloc, cnt, h_p, buf_p, h_s, buf_s = res[:8]
        if redo_last:
            x1, route, xloc, cnt, h_p, buf_p = _mixer_tail_call(x, pending, res[8], res[9], lw32, x1, route, xloc,
                                                                 cnt, geo=geo, cfg=cfg)
        outs["hp"].append(h_p)
        outs["bp"].append(buf_p)
        outs["hs"].append(h_s)
        outs["bs"].append(buf_s)
        x, pending = x1, (_moe_experts(xloc, cnt, w_gate, w_up, w_down, layer=l), route)

    y_prompt, y_sample = _finish_call(x, pending[0], pending[1], row2(norm_final_g), geo=geo)
    st = lambda hs: jnp.stack(hs).transpose(0, 2, 1, 3)
    pl_out = lambda bs: jnp.stack(bs)[:, 1:].transpose(0, 2, 1, 3)
    hp, hs = st(outs["hp"]), st(outs["hs"])
    return (y_prompt, y_sample, hp[..., :nstate], hp[..., nstate:], pl_out(outs["bp"]),
            hs[..., :nstate], hs[..., nstate:], pl_out(outs["bs"]))
```

```python
import functools

import numpy as np
import jax
import jax.numpy as jnp
from jax import lax
from jax.experimental import pallas as pl
from jax.experimental.pallas import tpu as pltpu

F32 = jnp.float32
BF16 = jnp.bfloat16

LANES = 128
SUBLANES = 8
CHUNK = 8
POOL_HIST = 16
MOE_TILE = 256
EXPERT_TILE = 512
VMEM_LIMIT = 56 * 1024 * 1024
EPS = 1e-6
NEG = -1e30
PAST_LEN = 16384


def _dot(a, b):
    return jnp.dot(a, b, preferred_element_type=F32)


def _mm(a, w, precise):
    if precise:
        return jnp.dot(a, w, precision=lax.Precision.HIGHEST, preferred_element_type=F32)
    return _dot(a.astype(BF16), w)


def _pack_halves(x):
    k = x.shape[-1] // 2
    return pltpu.pack_elementwise([x[:, :k], x[:, k:]], packed_dtype=BF16)


def _unpack_halves(p):
    lo = pltpu.unpack_elementwise(p, index=0, packed_dtype=BF16, unpacked_dtype=F32)
    hi = pltpu.unpack_elementwise(p, index=1, packed_dtype=BF16, unpacked_dtype=F32)
    return lo.astype(BF16), hi.astype(BF16)


def _dot_nt_f32(a, b):
    return lax.dot_general(a, b, (((1,), (1,)), ((), ())), precision=lax.Precision.HIGHEST,
                           preferred_element_type=F32)


def _s5_prep_kernel(a_re_ref, a_im_ref, ldt_ref, bt_ref, btx_ref, c_ref, cx_ref, mq_ref, p_ref, a8_ref):
    ch = bt_ref.shape[1]
    first_half = lax.broadcasted_iota(jnp.int32, (1, LANES), 1) < LANES // 2
    sgn = jnp.where(first_half, 1.0, -1.0)
    for s in range(CHUNK):
        a_re = a_re_ref[s]
        a_im = a_im_ref[s]
        dt = jnp.exp(ldt_ref[s])
        mag = jnp.exp(dt * a_re)
        abr = mag * jnp.cos(dt * a_im)
        abi = mag * jnp.sin(dt * a_im)
        den = a_re * a_re + a_im * a_im
        nr = abr - 1.0
        f_re = (nr * a_re + abi * a_im) / den
        f_im = (abi * a_re - nr * a_im) / den
        bb = f_re * bt_ref[s] - sgn * f_im * btx_ref[s]
        cc = c_ref[s] * sgn
        ccx = -cx_ref[s]
        bbx = f_re * btx_ref[s] + sgn * f_im * bt_ref[s]
        pr = jnp.ones_like(a_re)
        pi = jnp.zeros_like(a_re)
        e2, w2 = [], []
        for k in range(CHUNK + 1):
            e2.append(pr * cc + pi * ccx)
            w2.append(pr * bb - sgn * pi * bbx)
            if k < CHUNK:
                pr, pi = pr * abr - pi * abi, pr * abi + pi * abr
        a8_ref[s, 0:1, :] = pr
        a8_ref[s, 1:2, :] = -sgn * pi
        a8_ref[s, 2:3, :] = sgn * pi
        zero = jnp.zeros_like(bb)
        jpos = [(p - s) % CHUNK for p in range(CHUNK)]
        for p in range(CHUNK):
            j = jpos[p]
            taps = jnp.concatenate([e2[jp - j] if jp >= j else zero for jp in jpos], axis=0)
            rows = slice(p * ch, (p + 1) * ch)
            mq_ref[s, rows, 0:LANES] = _dot_nt_f32(bb, taps)
            mq_ref[s, rows, LANES:2 * LANES] = w2[CHUNK - 1 - j]
        pt = jnp.concatenate([e2[jp + 1] for jp in jpos], axis=0)
        p_ref[s] = pt.T


def _s5_prepare(a_re, a_im, log_dt, b_re, b_im, c_re, c_im):
    depth, groups, nstate = a_re.shape
    ch = b_re.shape[-1]
    dg = depth * groups
    assert ch * CHUNK == LANES and 2 * nstate == LANES and groups % CHUNK == 0
    dup = lambda x: jnp.concatenate([x, x], axis=-1).reshape(dg, 1, LANES).astype(F32)
    btr = jnp.swapaxes(b_re.reshape(dg, nstate, ch), 1, 2).astype(F32)
    bti = jnp.swapaxes(b_im.reshape(dg, nstate, ch), 1, 2).astype(F32)
    cr = c_re.reshape(dg, ch, nstate).astype(F32)
    ci = c_im.reshape(dg, ch, nstate).astype(F32)
    pair = lambda x, y: jnp.concatenate([x, y], axis=-1)
    blk = lambda *shape: pl.BlockSpec((CHUNK,) + shape, lambda g: (g,) + (0,) * len(shape))
    mq, pmat, a8 = pl.pallas_call(
        _s5_prep_kernel,
        grid=(dg // CHUNK,),
        in_specs=[blk(1, LANES), blk(1, LANES), blk(1, 1)] + [blk(ch, LANES)] * 4,
        out_specs=[blk(LANES, 2 * LANES), blk(LANES, LANES), blk(3, LANES)],
        out_shape=[jax.ShapeDtypeStruct((dg, LANES, 2 * LANES), F32), jax.ShapeDtypeStruct((dg, LANES, LANES), F32),
                   jax.ShapeDtypeStruct((dg, 3, LANES), F32)],
        name="s5_prep",
    )(dup(a_re), dup(a_im), log_dt.reshape(dg, 1, 1).astype(F32), pair(btr, bti), pair(bti, btr),
      pair(cr, ci), pair(ci, cr))
    shp = lambda x: x.reshape((depth, groups) + x.shape[1:])
    return shp(mq), shp(pmat), shp(a8)


N_MIXER_WEIGHTS = 14


def _moe_combine(x1_ref, yl_ref, route_ref):
    rows = x1_ref.shape[0]
    n_local = yl_ref.shape[1]
    lio = lax.broadcasted_iota(jnp.int32, (MOE_TILE, n_local), 1)
    parts = []
    for hh in range(rows // MOE_TILE):
        r = route_ref[hh * MOE_TILE:(hh + 1) * MOE_TILE, :]
        wperm = jnp.where(lio == r[:, 4:5].astype(jnp.int32), r[:, 2:3],
                          jnp.where(lio == r[:, 5:6].astype(jnp.int32), r[:, 3:4], 0.0)).astype(BF16)
        y_lo, y_hi = _unpack_halves(yl_ref[hh])
        parts.append(jnp.concatenate([_dot(wperm, y_lo), _dot(wperm, y_hi)], axis=-1))
    return x1_ref[...] + jnp.concatenate(parts, axis=0)


def _stream_step(get_x, w_refs, h_init, buf_init, first, pos_base, x1_ref, route_ref, xloc_ref, cnt_ref,
                 h_ref, buf_ref, u_scr, y_scr, cat_scr, *, bblk, tile_l, cfg, precise):
    (gmix_ref, win_ref, mq_ref, p_ref, a8_ref, d_ref, wglu_ref, bglu_ref, wpool_ref, pscale_ref, wout_ref,
     gffn_ref, wr_ref, br_ref) = w_refs
    windows = cfg["windows"]
    n_expert_groups = cfg["n_expert_groups"]
    experts_per_group = cfg["experts_per_group"]
    rows = tile_l * bblk
    nk = tile_l // CHUNK
    n = nk * bblk
    d_model = x1_ref.shape[-1]
    d_ssm = y_scr.shape[-1]
    d_pool = d_model - d_ssm
    hist = POOL_HIST * bblk

    def _init():
        h_ref[...] = h_init()
        cat_scr[0:hist, :] = buf_init().reshape(hist, d_pool)

    if first is True:
        _init()
    else:
        pl.when(first)(_init)

    x = get_x()
    ms = jnp.mean(x * x, axis=-1, keepdims=True)
    h = x * lax.rsqrt(ms + EPS) * gmix_ref[...]
    u = _mm(h, win_ref[...], precise)
    u_scr[...] = u.reshape(nk, CHUNK, bblk, d_model)

    slot = lax.broadcasted_iota(jnp.int32, (n, LANES), 1) // (LANES // CHUNK)
    for v in range(d_ssm // LANES):
        rolled = []
        for j in range(CHUNK):
            uj = u_scr[:, j, :, v * LANES:(v + 1) * LANES].reshape(n, LANES)
            rolled.append(pltpu.roll(uj, (LANES // CHUNK) * j, axis=1) if j else uj)
        ys = []
        for s in range(CHUNK):
            g = v * CHUNK + s
            ug = rolled[0]
            for j in range(1, CHUNK):
                ug = jnp.where(slot == (s + j) % CHUNK, rolled[j], ug)
            yq = _mm(ug, mq_ref[g], precise)
            sg = yq[:, LANES:]
            sg_sw = pltpu.roll(sg, LANES // 2, axis=1)
            a8c = a8_ref[g, 0:1, :]
            a8s = a8_ref[g, 1:2, :]
            a8w = a8_ref[g, 2:3, :]
            hcur = h_ref[g]
            hsw = pltpu.roll(hcur, LANES // 2, axis=1)
            hins = []
            for k in range(nk):
                hins.append(hcur)
                sk = sg[k * bblk:(k + 1) * bblk]
                skw = sg_sw[k * bblk:(k + 1) * bblk]
                hcur, hsw = a8c * hcur + a8s * hsw + sk, a8c * hsw + a8w * hcur + skw
            h_ref[g] = hcur
            hin = jnp.concatenate(hins, axis=0) if nk > 1 else hins[0]
            ys.append(yq[:, :LANES] + _mm(hin, p_ref[g], precise))
        for j in range(CHUNK):
            z = ys[0]
            for s in range(1, CHUNK):
                z = jnp.where(slot == (s + j) % CHUNK, ys[s], z)
            if j:
                z = pltpu.roll(z, LANES - (LANES // CHUNK) * j, axis=1)
            y_scr[:, j, :, v * LANES:(v + 1) * LANES] = z.reshape(nk, bblk, LANES)

    u_ssm = u_scr[:, :, :, 0:d_ssm].reshape(rows, d_ssm)
    y = y_scr[...].reshape(rows, d_ssm) + d_ref[...] * u_ssm
    gl = jax.nn.gelu(y)
    ssm_out = gl * jax.nn.sigmoid(_mm(gl, wglu_ref[...], precise) + bglu_ref[...])

    up = u_scr[:, :, :, d_ssm:].reshape(rows, d_pool)
    cat_scr[hist:hist + rows, :] = up
    row_pos = lax.broadcasted_iota(jnp.int32, (rows, 1), 0) // bblk
    pos = pos_base + row_pos
    pool_parts = []
    pgrp = d_pool // len(windows)
    for kk, w in enumerate(windows):
        lo = kk * pgrp
        acc = cat_scr[hist:hist + rows, lo:lo + pgrp]
        cur = acc
        for sft in range(1, w):
            acc = acc + cat_scr[hist - sft * bblk:hist - sft * bblk + rows, lo:lo + pgrp]
        inv = 1.0 / jnp.minimum(pos + 1, w).astype(F32)
        diff = acc * inv - cur
        pool_parts.append(_mm(diff, wpool_ref[kk], precise))
    pool_out = jnp.concatenate(pool_parts, axis=-1) * pscale_ref[...]
    tail = cat_scr[rows:rows + hist, :]
    cat_scr[0:hist, :] = tail
    buf_ref[...] = tail.reshape(POOL_HIST, bblk, d_pool)

    if precise:
        mix = jnp.concatenate([ssm_out, pool_out], axis=-1)
    else:
        mix = jnp.concatenate([ssm_out.astype(BF16), pool_out.astype(BF16)], axis=-1)
    x1 = x + _mm(mix, wout_ref[...], precise)
    x1_ref[...] = x1
    ms2 = jnp.mean(x1 * x1, axis=-1, keepdims=True)
    t = x1 * lax.rsqrt(ms2 + EPS) * gffn_ref[...]

    t_hi = t.astype(BF16)
    t_lo = (t - t_hi.astype(F32)).astype(BF16)
    w_r = wr_ref[...]
    w_hi = w_r.astype(BF16)
    w_lo = (w_r - w_hi.astype(F32)).astype(BF16)
    hi_both = _dot(t_hi, jnp.concatenate([w_hi, w_lo], axis=-1))
    logits = hi_both[:, :LANES] + hi_both[:, LANES:] + _dot(t_lo, w_hi) + br_ref[...]
    lane = lax.broadcasted_iota(jnp.int32, (rows, LANES), 1).astype(F32)
    big = float(4 * LANES)
    lg = jnp.where(lane < n_expert_groups, logits, NEG)
    mg = jnp.max(lg, axis=-1, keepdims=True)
    gsel = jnp.min(jnp.where(lg == mg, lane, big), axis=-1, keepdims=True)
    p_sel = 1.0 / jnp.sum(jnp.exp(lg - mg), axis=-1, keepdims=True)
    e_lo = n_expert_groups + gsel * experts_per_group
    le = jnp.where((lane >= e_lo) & (lane < e_lo + experts_per_group), logits, NEG)
    v1 = jnp.max(le, axis=-1, keepdims=True)
    i1 = jnp.min(jnp.where(le == v1, lane, big), axis=-1, keepdims=True)
    le2 = jnp.where(lane == i1, NEG, le)
    v2 = jnp.max(le2, axis=-1, keepdims=True)
    i2 = jnp.min(jnp.where(le2 == v2, lane, big), axis=-1, keepdims=True)
    ex = jnp.exp(v2 - v1)
    q1 = p_sel / (1.0 + ex)
    q2 = q1 * ex
    e1 = i1 - n_expert_groups
    e2 = i2 - n_expert_groups
    route = jnp.where(lane == 0.0, e1, jnp.where(lane == 1.0, e2, jnp.where(lane == 2.0, q1,
                      jnp.where(lane == 3.0, q2, 0.0))))

    lf = lax.broadcasted_iota(jnp.int32, (MOE_TILE, LANES), 1).astype(F32)
    n_local = xloc_ref.shape[1]
    before_e = (lax.broadcasted_iota(jnp.int32, (LANES, LANES), 0)
                < lax.broadcasted_iota(jnp.int32, (LANES, LANES), 1)).astype(BF16)
    earlier = (lax.broadcasted_iota(jnp.int32, (MOE_TILE, MOE_TILE), 0)
               > lax.broadcasted_iota(jnp.int32, (MOE_TILE, MOE_TILE), 1)).astype(BF16)
    local_row = lax.broadcasted_iota(jnp.int32, (n_local, MOE_TILE), 0)
    route_parts = []
    for hh in range(rows // MOE_TILE):
        sl = slice(hh * MOE_TILE, (hh + 1) * MOE_TILE)
        rt = route[sl]
        oh1 = lf == rt[:, 0:1]
        oh2 = lf == rt[:, 1:2]
        oh = jnp.logical_or(oh1, oh2).astype(F32)
        cnt = jnp.sum(oh, axis=0, keepdims=True)
        run_chunks = jnp.floor((cnt + (SUBLANES - 1)) * (1.0 / SUBLANES))
        loff = SUBLANES * _dot(jnp.broadcast_to(run_chunks, (SUBLANES, LANES)).astype(BF16), before_e)[0:1]
        where_to = loff + _dot(earlier, oh.astype(BF16))
        lpos1 = jnp.sum(jnp.where(oh1, where_to, 0.0), axis=-1, keepdims=True)
        lpos2 = jnp.sum(jnp.where(oh2, where_to, 0.0), axis=-1, keepdims=True)
        lpos_t = jnp.where(lf == 0.0, lpos1, jnp.where(lf == 1.0, lpos2, 0.0)).T
        perm = jnp.logical_or(local_row == lpos_t[0:1].astype(jnp.int32),
                              local_row == lpos_t[1:2].astype(jnp.int32)).astype(BF16)
        xloc_ref[hh] = _pack_halves(_dot(perm, t_hi[sl]))
        cnt_ref[hh] = cnt
        route_parts.append(jnp.where(lf == 4.0, lpos1, jnp.where(lf == 5.0, lpos2, rt)))
    route_ref[...] = jnp.concatenate(route_parts, axis=0)


def _split_inputs(refs, source, geo):
    if source == "raw":
        xp_ref, xs_ref, meta_ref = refs[:3]
        by_position = lambda ref: (lambda: jnp.concatenate([ref[:, l, :] for l in range(ref.shape[1])], axis=0))
        meta = lambda: jnp.concatenate([jnp.broadcast_to(meta_ref[l:l + 1, :], (geo["m_batch"], meta_ref.shape[1]))
                                        for l in range(meta_ref.shape[0])], axis=0)
        return dict(meta=meta, prompt=by_position(xp_ref), sample=by_position(xs_ref)), refs[3:]
    if source == "pending":
        x_ref, yl_ref, rprev_ref = refs[:3]
        get, refs = (lambda: _moe_combine(x_ref, yl_ref, rprev_ref)), refs[3:]
    else:
        x_ref = refs[0]
        get, refs = (lambda: x_ref[...]), refs[1:]
    return dict(meta=get, prompt=get, sample=get), refs


def _mixer_kernel(*refs, cfg, geo, snapshot, source):
    get_x, refs = _split_inputs(refs, source, geo)
    h0s_ref, buf0s_ref = refs[:2]
    w_refs = refs[2:2 + N_MIXER_WEIGHTS]
    rest = refs[2 + N_MIXER_WEIGHTS:]
    x1_ref, route_ref, xloc_ref, cnt_ref, hp_ref, bufp_ref, hs_ref, bufs_ref = rest[:8]
    rest = rest[8:]
    if snapshot:
        hsnap_ref, bsnap_ref = rest[:2]
        rest = rest[2:]
    hm, bufm, u_m, y_m, cat_m, u_p, y_p, cat_p, u_s, y_s, cat_s = rest
    outs = (x1_ref, route_ref, xloc_ref, cnt_ref)
    s = pl.program_id(0)
    n_p = geo["n_p_steps"]
    batch = geo["batch"]

    @pl.when(s == 0)
    def _meta():
        _stream_step(get_x["meta"], w_refs, lambda: jnp.zeros(hm.shape, F32), lambda: jnp.zeros(bufm.shape, F32), True, 0,
                     *outs, hm, bufm, u_m, y_m, cat_m, bblk=geo["m_batch"], tile_l=geo["n_meta"], cfg=cfg,
                     precise=False)

    @pl.when(jnp.logical_and(s >= 1, s <= n_p))
    def _prompt():
        if snapshot:
            @pl.when(s == n_p)
            def _():
                hsnap_ref[...] = hp_ref[...]
                bsnap_ref[...] = bufp_ref[...]
        _stream_step(get_x["prompt"], w_refs, lambda: hm[:, 0:batch, :], lambda: bufm[:, 0:batch, :], s == 1,
                     geo["n_meta"] + (s - 1) * geo["p_tile_l"], *outs, hp_ref, bufp_ref, u_p, y_p, cat_p,
                     bblk=batch, tile_l=geo["p_tile_l"], cfg=cfg, precise=False)

    @pl.when(s > n_p)
    def _sample():
        _stream_step(get_x["sample"], w_refs, lambda: h0s_ref[...], lambda: buf0s_ref[...], True, PAST_LEN,
                     *outs, hs_ref, bufs_ref, u_s, y_s, cat_s, bblk=geo["s_bblk"], tile_l=geo["dec_seq"], cfg=cfg,
                     precise=False)


def _mixer_weights(lw):
    return [lw["g_mix"], lw["w_in"], lw["mq"], lw["pmat"], lw["a8"], lw["ssm_d"], lw["w_glu"], lw["b_glu"],
            lw["w_pool"], lw["pool_scale"], lw["w_out"], lw["g_ffn"], lw["w_r"], lw["b_r"]]


def _stream_scratch(tile_l, bblk, d_model, d_ssm):
    return [pltpu.VMEM((tile_l // CHUNK, CHUNK, bblk, d_model), F32),
            pltpu.VMEM((tile_l // CHUNK, CHUNK, bblk, d_ssm), F32),
            pltpu.VMEM(((POOL_HIST + tile_l) * bblk, d_model - d_ssm), F32)]


def _input_specs(x_in, pending, geo, d_model, blk, pblk, sblk):
    g = geo
    rows = g["step_rows"]
    if isinstance(x_in, tuple):
        specs = [pl.BlockSpec((g["batch"], g["p_tile_l"], d_model), lambda s: (0, pblk(s), 0)),
                 pl.BlockSpec((g["s_bblk"], g["dec_seq"], d_model), lambda s: (sblk(s), 0, 0)),
                 pl.BlockSpec(x_in[2].shape, lambda s: (0, 0))]
        return specs, list(x_in), "raw"
    specs = [pl.BlockSpec((rows, d_model), lambda s: (blk(s), 0))]
    if pending is None:
        return specs, [x_in], "flat"
    specs += [pl.BlockSpec((rows // MOE_TILE, g["n_local"], d_model // 2), lambda s: (blk(s), 0, 0)),
              pl.BlockSpec((rows, LANES), lambda s: (blk(s), 0))]
    return specs, [x_in, *pending], "pending"


def _layer_spec(a, layer):
    nd = a.ndim
    return pl.BlockSpec((None,) + a.shape[1:], lambda s, _n=nd: (layer,) + (0,) * (_n - 1))


def _mixer_call(x_in, pending, h0s, buf0s, lw, *, geo, cfg, snapshot, layer):
    d_model = lw["w_in"].shape[1]
    total = (geo["n_p_steps"] + geo["n_s_steps"] + 1) * geo["step_rows"]
    groups, dec_batch, _ = h0s.shape
    d_pool = buf0s.shape[-1]
    d_ssm = d_model - d_pool
    g = geo
    n_p, n_s = g["n_p_steps"], g["n_s_steps"]
    rows = g["step_rows"]
    per_step = rows // MOE_TILE
    n_local = g["n_local"]
    blk = lambda s: jnp.where(s == 0, n_p + n_s, s - 1)
    sblk = lambda s: jnp.clip(s - (n_p + 1), 0, n_s - 1)
    weights = _mixer_weights(lw)
    h_s_spec = pl.BlockSpec((groups, g["s_bblk"], LANES), lambda s: (0, sblk(s), 0))
    buf_s_spec = pl.BlockSpec((POOL_HIST, g["s_bblk"], d_pool), lambda s: (0, sblk(s), 0))
    x_specs, x_args, source = _input_specs(x_in, pending, geo, d_model, blk,
                                           lambda s: jnp.clip(s - 1, 0, n_p - 1), sblk)
    in_specs = x_specs + [h_s_spec, buf_s_spec] + [_layer_spec(w, layer) for w in weights]
    h_p_shape, buf_p_shape = (groups, g["batch"], LANES), (POOL_HIST, g["batch"], d_pool)
    out_specs = [pl.BlockSpec((rows, d_model), lambda s: (blk(s), 0)),
                 pl.BlockSpec((rows, LANES), lambda s: (blk(s), 0)),
                 pl.BlockSpec((per_step, n_local, d_model // 2), lambda s: (blk(s), 0, 0)),
                 pl.BlockSpec((per_step, 1, LANES), lambda s: (blk(s), 0, 0)),
                 pl.BlockSpec(h_p_shape, lambda s: (0, 0, 0)), pl.BlockSpec(buf_p_shape, lambda s: (0, 0, 0)),
                 h_s_spec, buf_s_spec]
    out_shape = [jax.ShapeDtypeStruct((total, d_model), F32), jax.ShapeDtypeStruct((total, LANES), F32),
                 jax.ShapeDtypeStruct((total // MOE_TILE, n_local, d_model // 2), jnp.uint32),
                 jax.ShapeDtypeStruct((total // MOE_TILE, 1, LANES), F32),
                 jax.ShapeDtypeStruct(h_p_shape, F32), jax.ShapeDtypeStruct(buf_p_shape, F32),
                 jax.ShapeDtypeStruct((groups, dec_batch, LANES), F32),
                 jax.ShapeDtypeStruct((POOL_HIST, dec_batch, d_pool), F32)]
    if snapshot:
        out_specs += [pl.BlockSpec(h_p_shape, lambda s: (0, 0, 0)), pl.BlockSpec(buf_p_shape, lambda s: (0, 0, 0))]
        out_shape += [jax.ShapeDtypeStruct(h_p_shape, F32), jax.ShapeDtypeStruct(buf_p_shape, F32)]
    scratch = ([pltpu.VMEM((groups, g["m_batch"], LANES), F32), pltpu.VMEM((POOL_HIST, g["m_batch"], d_pool), F32)]
               + _stream_scratch(g["n_meta"], g["m_batch"], d_model, d_ssm)
               + _stream_scratch(g["p_tile_l"], g["batch"], d_model, d_ssm)
               + _stream_scratch(g["dec_seq"], g["s_bblk"], d_model, d_ssm))
    return pl.pallas_call(
        functools.partial(_mixer_kernel, cfg=cfg, geo=geo, snapshot=snapshot, source=source),
        grid=(1 + n_p + n_s,),
        in_specs=in_specs,
        out_specs=out_specs,
        out_shape=out_shape,
        scratch_shapes=scratch,
        compiler_params=pltpu.CompilerParams(dimension_semantics=("arbitrary",), vmem_limit_bytes=VMEM_LIMIT),
        name="mixer",
    )(*x_args, h0s, buf0s, *weights)


def _mixer_tail_kernel(*refs, cfg, geo, source):
    get_x, refs = _split_inputs(refs, source, geo)
    get_x = get_x["prompt"]
    h0_ref, buf0_ref = refs[:2]
    w_refs = refs[2:2 + N_MIXER_WEIGHTS]
    rest = refs[2 + N_MIXER_WEIGHTS + 4:]
    x1_ref, route_ref, xloc_ref, cnt_ref, h_ref, buf_ref, u_scr, y_scr, cat_scr = rest
    pos = geo["n_meta"] + (geo["n_p_steps"] - 1) * geo["p_tile_l"]
    _stream_step(get_x, w_refs, lambda: h0_ref[...], lambda: buf0_ref[...], True, pos, x1_ref, route_ref, xloc_ref,
                 cnt_ref, h_ref, buf_ref, u_scr, y_scr, cat_scr, bblk=geo["batch"], tile_l=geo["p_tile_l"], cfg=cfg,
                 precise=True)


def _mixer_tail_call(x_in, pending, h0, buf0, lw32, x1, route, xloc, cnt, *, geo, cfg, layer):
    d_model = x1.shape[-1]
    d_pool = buf0.shape[-1]
    g = geo
    rows = g["step_rows"]
    per_step = rows // MOE_TILE
    last = g["n_p_steps"] - 1
    weights = _mixer_weights(lw32)

    def full(a):
        nd = a.ndim
        return pl.BlockSpec(a.shape, lambda s, _n=nd: (0,) * _n)

    x_specs, x_args, source = _input_specs(x_in, pending, geo, d_model, lambda s: last, lambda s: last, lambda s: 0)
    n_in = len(x_args) + 2 + len(weights)
    return pl.pallas_call(
        functools.partial(_mixer_tail_kernel, cfg=cfg, geo=geo, source=source),
        grid=(1,),
        in_specs=(x_specs + [full(h0), full(buf0)] + [_layer_spec(w, layer) for w in weights]
                  + [pl.BlockSpec(memory_space=pl.ANY)] * 4),
        out_specs=[pl.BlockSpec((rows, d_model), lambda s: (last, 0)),
                   pl.BlockSpec((rows, LANES), lambda s: (last, 0)),
                   pl.BlockSpec((per_step, g["n_local"], d_model // 2), lambda s: (last, 0, 0)),
                   pl.BlockSpec((per_step, 1, LANES), lambda s: (last, 0, 0)),
                   full(h0), full(buf0)],
        out_shape=[jax.ShapeDtypeStruct(a.shape, a.dtype) for a in (x1, route, xloc, cnt, h0, buf0)],
        scratch_shapes=_stream_scratch(g["p_tile_l"], g["batch"], d_model, d_model - d_pool),
        input_output_aliases={n_in + k: k for k in range(4)},
        compiler_params=pltpu.CompilerParams(dimension_semantics=("arbitrary",), vmem_limit_bytes=VMEM_LIMIT),
        name="mixer_tail",
    )(*x_args, h0, buf0, *weights, x1, route, xloc, cnt)


def _experts_kernel(exp_ref, nact_ref, src_ref, nval_ref, xl_in_ref, wg_ref, wu_ref, wd_ref, rows_ref,
                    xbuf, ybuf, wgu_scr, wd_scr, sem_in, sem_out):
    del xl_in_ref
    j = pl.program_id(0)
    f = wg_ref.shape[-1]
    chunks_per_tile = EXPERT_TILE // SUBLANES
    nact = nact_ref[0]

    def for_tile_chunks(tile, op, scatter):
        slot = tile % 2

        def body(k, carry):
            hbm = rows_ref.at[pl.ds(pl.multiple_of(src_ref[tile * chunks_per_tile + k], SUBLANES), SUBLANES), :]
            local = pl.ds(pl.multiple_of(k * SUBLANES, SUBLANES), SUBLANES)
            if scatter:
                op(pltpu.make_async_copy(ybuf.at[slot, local, :], hbm, sem_out.at[slot]))
            else:
                op(pltpu.make_async_copy(hbm, xbuf.at[slot, local, :], sem_in.at[slot]))
            return carry

        lax.fori_loop(0, nval_ref[tile], body, 0)

    def wait_tile_chunks(tile, scatter):
        slot = tile % 2
        full = nval_ref[tile] == chunks_per_tile

        @pl.when(full)
        def _():
            whole = rows_ref.at[pl.ds(0, EXPERT_TILE), :]
            if scatter:
                pltpu.make_async_copy(ybuf.at[slot], whole, sem_out.at[slot]).wait()
            else:
                pltpu.make_async_copy(whole, xbuf.at[slot], sem_in.at[slot]).wait()

        @pl.when(jnp.logical_not(full))
        def _():
            for_tile_chunks(tile, wait, scatter)

    start = lambda cp: cp.start()
    wait = lambda cp: cp.wait()

    @pl.when(j == 0)
    def _():
        xbuf[...] = jnp.zeros_like(xbuf)
        for_tile_chunks(0, start, False)

    @pl.when(j + 1 < nact)
    def _():
        for_tile_chunks(j + 1, start, False)

    @pl.when(jnp.logical_and(j >= 2, j - 2 < nact))
    def _():
        wait_tile_chunks(j - 2, True)

    active = j < nact
    changed = jnp.logical_or(j == 0, exp_ref[j] != exp_ref[jnp.maximum(j - 1, 0)])

    @pl.when(jnp.logical_and(active, changed))
    def _():
        wgu_scr[:, 0:f] = wg_ref[...].astype(BF16)
        wgu_scr[:, f:2 * f] = wu_ref[...].astype(BF16)
        wd_scr[...] = wd_ref[...].astype(BF16)

    @pl.when(active)
    def _():
        wait_tile_chunks(j, False)
        half = wgu_scr.shape[0] // 2
        t_lo, t_hi = _unpack_halves(xbuf[j % 2])
        gu = _dot(t_lo, wgu_scr[0:half, :]) + _dot(t_hi, wgu_scr[half:2 * half, :])
        act = jax.nn.silu(gu[:, 0:f]) * gu[:, f:2 * f]
        ybuf[j % 2] = _pack_halves(_dot(act.astype(BF16), wd_scr[...]))
        for_tile_chunks(j, start, True)


def _experts_call(tile_exp, n_active, src, nval, xloc, w_gate, w_up, w_down, layer, n_tiles):
    _, n_exp, d_model, f = w_gate.shape
    xl_flat = xloc.reshape(-1, xloc.shape[-1])
    wspec = lambda a, b: pl.BlockSpec((None, None, a, b), lambda i, ex, *_: (layer, ex[i], 0, 0))
    buf = pltpu.VMEM((2, EXPERT_TILE, d_model // 2), jnp.uint32)
    out = pl.pallas_call(
        _experts_kernel,
        grid_spec=pltpu.PrefetchScalarGridSpec(
            num_scalar_prefetch=4,
            grid=(n_tiles + 2,),
            in_specs=[pl.BlockSpec(memory_space=pl.ANY), wspec(d_model, f), wspec(d_model, f), wspec(f, d_model)],
            out_specs=pl.BlockSpec(memory_space=pl.ANY),
            scratch_shapes=[buf, buf, pltpu.VMEM((d_model, 2 * f), BF16), pltpu.VMEM((f, d_model), BF16),
                            pltpu.SemaphoreType.DMA((2,)), pltpu.SemaphoreType.DMA((2,))],
        ),
        out_shape=jax.ShapeDtypeStruct(xl_flat.shape, xl_flat.dtype),
        input_output_aliases={4: 0},
        compiler_params=pltpu.CompilerParams(dimension_semantics=("arbitrary",), vmem_limit_bytes=VMEM_LIMIT),
        name="moe_experts",
    )(tile_exp, n_active, src, nval, xl_flat, w_gate, w_up, w_down)
    return out.reshape(xloc.shape)


def _moe_experts(xloc, cnt, w_gate, w_up, w_down, *, layer):
    n_exp = w_gate.shape[1]
    nt, n_local, _ = xloc.shape
    total = nt * MOE_TILE
    cpt = EXPERT_TILE // SUBLANES
    n_chunks = n_local // SUBLANES
    counts = cnt[:, 0, :n_exp].astype(jnp.int32)
    run = (counts + SUBLANES - 1) // SUBLANES
    lend = jnp.cumsum(run, axis=1)
    loff = lend - run
    gend = jnp.cumsum(run, axis=0)
    gcum = gend - run
    seg = gend[-1]
    seg_tiles = (seg + cpt - 1) // cpt
    seg_end = jnp.cumsum(seg_tiles)
    seg_off = seg_end - seg_tiles
    n_tiles = (2 * total + nt * n_exp * (SUBLANES - 1) + n_exp * (EXPERT_TILE - 1)) // EXPERT_TILE + 1
    n_steps = n_tiles + 2
    n_active = seg_end[-1].reshape(1).astype(jnp.int32)
    tile_idx = jnp.minimum(jnp.arange(n_steps, dtype=jnp.int32), n_active - 1)
    tile_exp = jnp.minimum(jnp.sum((tile_idx[:, None] >= seg_end[None, :]).astype(jnp.int32), axis=1), n_exp - 1)
    is_e = tile_exp[:, None] == jnp.arange(n_exp, dtype=jnp.int32)[None, :]
    pick = lambda table: jnp.sum(jnp.where(is_e[:, None, :], table[None, :, :], 0), axis=2)
    pick1 = lambda vec: jnp.sum(jnp.where(is_e, vec[None, :], 0), axis=1)
    g = ((jnp.arange(n_steps, dtype=jnp.int32) - pick1(seg_off)) * cpt)[:, None] + jnp.arange(cpt, dtype=jnp.int32)
    gend_t, gcum_t, loff_t = pick(gend), pick(gcum), pick(loff)
    owner = jnp.logical_and(g[:, :, None] >= gcum_t[:, None, :], g[:, :, None] < gend_t[:, None, :])
    tile_base = jnp.arange(nt, dtype=jnp.int32) * n_chunks
    src = jnp.sum(jnp.where(owner, (tile_base[None, :] + loff_t - gcum_t)[:, None, :] + g[:, :, None], 0), axis=2)
    src = (src * SUBLANES).reshape(-1)
    nval = jnp.clip(pick1(seg) - g[:, 0], 0, cpt)
    return _experts_call(tile_exp, n_active, src, nval, xloc, w_gate, w_up, w_down, layer, n_tiles)


def _finish_kernel(x1_ref, yl_ref, route_ref, gfin_ref, yp_ref, ys_ref, *, n_p_steps):
    i = pl.program_id(0)
    x = _moe_combine(x1_ref, yl_ref, route_ref)
    ms = jnp.mean(x * x, axis=-1, keepdims=True)
    y = x * lax.rsqrt(ms + EPS) * gfin_ref[...]
    pb, pl_ = yp_ref.shape[0], yp_ref.shape[1]
    sb, sl = ys_ref.shape[0], ys_ref.shape[1]

    @pl.when(i < n_p_steps)
    def _():
        for l in range(pl_):
            yp_ref[:, l, :] = y[l * pb:(l + 1) * pb, :]

    @pl.when(i >= n_p_steps)
    def _():
        for l in range(sl):
            ys_ref[:, l, :] = y[l * sb:(l + 1) * sb, :]


def _finish_call(x1, yl, route, g_final, *, geo):
    d_model = x1.shape[-1]
    g = geo
    rows = g["step_rows"]
    n_p, n_s = g["n_p_steps"], g["n_s_steps"]
    return pl.pallas_call(
        functools.partial(_finish_kernel, n_p_steps=n_p),
        grid=(n_p + n_s,),
        in_specs=[pl.BlockSpec((rows, d_model), lambda i: (i, 0)),
                  pl.BlockSpec((rows // MOE_TILE, g["n_local"], d_model // 2), lambda i: (i, 0, 0)),
                  pl.BlockSpec((rows, LANES), lambda i: (i, 0)),
                  pl.BlockSpec((1, d_model), lambda i: (0, 0))],
        out_specs=[pl.BlockSpec((g["batch"], g["p_tile_l"], d_model), lambda i: (0, jnp.minimum(i, n_p - 1), 0)),
                   pl.BlockSpec((g["s_bblk"], g["dec_seq"], d_model), lambda i: (jnp.clip(i - n_p, 0, n_s - 1), 0, 0))],
        out_shape=[jax.ShapeDtypeStruct((g["batch"], g["seq"], d_model), F32),
                   jax.ShapeDtypeStruct((g["dec_batch"], g["dec_seq"], d_model), F32)],
        compiler_params=pltpu.CompilerParams(dimension_semantics=("arbitrary",), vmem_limit_bytes=VMEM_LIMIT),
        name="finish",
    )(x1, yl, route, g_final)


def kernel(x_prompt, x_sample, state_ssm_re, state_ssm_im, state_pool, meta_tokens, norm_mix_g, w_in, ssm_a_re,
           ssm_a_im, ssm_log_dt, ssm_b_re, ssm_b_im, ssm_c_re, ssm_c_im, ssm_d, w_glu, b_glu, w_pool, pool_scale,
           w_out, norm_ffn_g, w_router_group, b_router_group, w_router_expert, b_router_expert, w_gate, w_up,
           w_down, norm_final_g):
    batch, seq, d_model = x_prompt.shape
    dec_batch, dec_seq, _ = x_sample.shape
    depth, _, groups, nstate = state_ssm_re.shape
    n_meta = meta_tokens.shape[0]
    pool_buf, d_pool = state_pool.shape[2], state_pool.shape[3]
    n_pool_groups = w_pool.shape[1]
    windows = tuple(2 ** (k + 1) for k in range(n_pool_groups))
    assert pool_buf == max(windows) - 1 == POOL_HIST - 1
    n_expert_groups = w_router_group.shape[-1]
    n_experts = w_router_expert.shape[-1]
    cfg = dict(windows=windows, n_expert_groups=n_expert_groups, experts_per_group=n_experts // n_expert_groups)

    step_rows = 512
    p_tile_l, s_bblk, m_batch = step_rows // batch, step_rows // dec_seq, step_rows // n_meta
    n_p, n_s, n_m = batch * seq, dec_batch * dec_seq, m_batch * n_meta
    assert m_batch >= batch and seq % p_tile_l == 0 and dec_batch % s_bblk == 0
    total = n_p + n_s + n_m
    assert total % step_rows == 0 and step_rows % MOE_TILE == 0
    n_local = 2 * MOE_TILE + n_experts * (SUBLANES - 1)
    n_local = (n_local + LANES - 1) // LANES * LANES
    geo = dict(batch=batch, seq=seq, dec_batch=dec_batch, dec_seq=dec_seq, n_meta=n_meta, step_rows=step_rows,
               p_tile_l=p_tile_l, s_bblk=s_bblk, m_batch=m_batch, n_p_steps=seq // p_tile_l,
               n_s_steps=dec_batch // s_bblk, n_local=n_local)

    mq, pmat, a8 = _s5_prepare(ssm_a_re, ssm_a_im, ssm_log_dt, ssm_b_re, ssm_b_im, ssm_c_re, ssm_c_im)

    x = (x_prompt.astype(F32), x_sample.astype(F32), meta_tokens.astype(F32))

    row2 = lambda v: v.reshape(1, -1).astype(F32)
    outs = {k: [] for k in ("hp", "bp", "hs", "bs")}
    pending = None
    rows3 = lambda v: v.reshape(depth, 1, -1).astype(F32)
    w_r = jnp.concatenate([w_router_group, w_router_expert], axis=2).astype(F32)
    w_r = jnp.pad(w_r, ((0, 0), (0, 0), (0, LANES - w_r.shape[2])))
    b_r = jnp.concatenate([b_router_group, b_router_expert], axis=1).astype(F32)
    lw32 = dict(g_mix=rows3(norm_mix_g), w_in=w_in.astype(F32), mq=mq, pmat=pmat, a8=a8, ssm_d=rows3(ssm_d),
                w_glu=w_glu.astype(F32), b_glu=rows3(b_glu), w_pool=w_pool.astype(F32), pool_scale=rows3(pool_scale),
                w_out=w_out.astype(F32), g_ffn=rows3(norm_ffn_g), w_r=w_r,
                b_r=jnp.pad(b_r, ((0, 0), (0, LANES - b_r.shape[1])))[:, None, :])
    lw = dict(lw32, **{k: lw32[k].astype(BF16) for k in ("w_in", "mq", "pmat", "w_glu", "w_pool", "w_out")})
    for l in range(depth):
        h0_s = jnp.concatenate([state_ssm_re[l], state_ssm_im[l]], axis=-1).transpose(1, 0, 2).astype(F32)
        buf0_s = jnp.pad(state_pool[l].astype(F32).transpose(1, 0, 2), ((1, 0), (0, 0), (0, 0)))
        redo_last = l < depth - 1
        res = _mixer_call(x, pending, h0_s, buf0_s, lw, geo=geo, cfg=cfg, snapshot=redo_last, layer=l)
        x1, route, xloc, cnt, h_p, buf_p, h_s, buf_s = res[:8]
        if redo_last:
            x1, route, xloc, cnt, h_p, buf_p = _mixer_tail_call(x, pending, res[8], res[9], lw32, x1, route, xloc,
                                                                 cnt, geo=geo, cfg=cfg, layer=l)
        outs["hp"].append(h_p)
        outs["bp"].append(buf_p)
        outs["hs"].append(h_s)
        outs["bs"].append(buf_s)
        x, pending = x1, (_moe_experts(xloc, cnt, w_gate, w_up, w_down, layer=l), route)

    y_prompt, y_sample = _finish_call(x, pending[0], pending[1], row2(norm_final_g), geo=geo)
    st = lambda hs: jnp.stack(hs).transpose(0, 2, 1, 3)
    pl_out = lambda bs: jnp.stack(bs)[:, 1:].transpose(0, 2, 1, 3)
    hp, hs = st(outs["hp"]), st(outs["hs"])
    return (y_prompt, y_sample, hp[..., :nstate], hp[..., nstate:], pl_out(outs["bp"]),
            hs[..., :nstate], hs[..., nstate:], pl_out(outs["bs"]))
```

```python
import functools

import numpy as np
import jax
import jax.numpy as jnp
from jax import lax
from jax.experimental import pallas as pl
from jax.experimental.pallas import tpu as pltpu

F32 = jnp.float32
BF16 = jnp.bfloat16

LANES = 128
SUBLANES = 8
CHUNK = 8
POOL_HIST = 16
MOE_TILE = 256
EXPERT_TILE = 512
VMEM_LIMIT = 56 * 1024 * 1024
EPS = 1e-6
NEG = -1e30
PAST_LEN = 16384


def _dot(a, b):
    return jnp.dot(a, b, preferred_element_type=F32)


def _split_bf16(x):
    hi = x.astype(BF16)
    return hi, (x - hi.astype(F32)).astype(BF16)


def _dot3(a, b, dims=(((1,), (0,)), ((), ()))):
    a_hi, a_lo = _split_bf16(a)
    b_hi, b_lo = _split_bf16(b)
    dg = lambda x, y: lax.dot_general(x, y, dims, preferred_element_type=F32)
    return dg(a_hi, b_hi) + dg(a_lo, b_hi) + dg(a_hi, b_lo)


def _mm(a, w, precise):
    if precise:
        return _dot3(a, w)
    return _dot(a.astype(BF16), w)


def _pack_halves(x):
    k = x.shape[-1] // 2
    return pltpu.pack_elementwise([x[:, :k], x[:, k:]], packed_dtype=BF16)


def _unpack_halves(p):
    lo = pltpu.unpack_elementwise(p, index=0, packed_dtype=BF16, unpacked_dtype=F32)
    hi = pltpu.unpack_elementwise(p, index=1, packed_dtype=BF16, unpacked_dtype=F32)
    return lo.astype(BF16), hi.astype(BF16)


def _dot_nt_f32(a, b):
    return _dot3(a, b, (((1,), (1,)), ((), ())))


def _s5_prep_kernel(a_re_ref, a_im_ref, ldt_ref, bt_ref, btx_ref, c_ref, cx_ref, mq_ref, p_ref, a8_ref):
    ch = bt_ref.shape[1]
    first_half = lax.broadcasted_iota(jnp.int32, (1, LANES), 1) < LANES // 2
    sgn = jnp.where(first_half, 1.0, -1.0)
    for s in range(CHUNK):
        a_re = a_re_ref[s]
        a_im = a_im_ref[s]
        dt = jnp.exp(ldt_ref[s])
        mag = jnp.exp(dt * a_re)
        abr = mag * jnp.cos(dt * a_im)
        abi = mag * jnp.sin(dt * a_im)
        den = a_re * a_re + a_im * a_im
        nr = abr - 1.0
        f_re = (nr * a_re + abi * a_im) / den
        f_im = (abi * a_re - nr * a_im) / den
        bb = f_re * bt_ref[s] - sgn * f_im * btx_ref[s]
        cc = c_ref[s] * sgn
        ccx = -cx_ref[s]
        bbx = f_re * btx_ref[s] + sgn * f_im * bt_ref[s]
        pr = jnp.ones_like(a_re)
        pi = jnp.zeros_like(a_re)
        e2, w2 = [], []
        for k in range(CHUNK + 1):
            e2.append(pr * cc + pi * ccx)
            w2.append(pr * bb - sgn * pi * bbx)
            if k < CHUNK:
                pr, pi = pr * abr - pi * abi, pr * abi + pi * abr
        a8_ref[s, 0:1, :] = pr
        a8_ref[s, 1:2, :] = -sgn * pi
        a8_ref[s, 2:3, :] = sgn * pi
        zero = jnp.zeros_like(bb)
        jpos = [(p - s) % CHUNK for p in range(CHUNK)]
        for p in range(CHUNK):
            j = jpos[p]
            taps = jnp.concatenate([e2[jp - j] if jp >= j else zero for jp in jpos], axis=0)
            rows = slice(p * ch, (p + 1) * ch)
            mq_ref[s, rows, 0:LANES] = _dot_nt_f32(bb, taps)
            mq_ref[s, rows, LANES:2 * LANES] = w2[CHUNK - 1 - j]
        pt = jnp.concatenate([e2[jp + 1] for jp in jpos], axis=0)
        p_ref[s] = pt.T


def _s5_prepare(a_re, a_im, log_dt, b_re, b_im, c_re, c_im):
    depth, groups, nstate = a_re.shape
    ch = b_re.shape[-1]
    dg = depth * groups
    assert ch * CHUNK == LANES and 2 * nstate == LANES and groups % CHUNK == 0
    dup = lambda x: jnp.concatenate([x, x], axis=-1).reshape(dg, 1, LANES).astype(F32)
    btr = jnp.swapaxes(b_re.reshape(dg, nstate, ch), 1, 2).astype(F32)
    bti = jnp.swapaxes(b_im.reshape(dg, nstate, ch), 1, 2).astype(F32)
    cr = c_re.reshape(dg, ch, nstate).astype(F32)
    ci = c_im.reshape(dg, ch, nstate).astype(F32)
    pair = lambda x, y: jnp.concatenate([x, y], axis=-1)
    blk = lambda *shape: pl.BlockSpec((CHUNK,) + shape, lambda g: (g,) + (0,) * len(shape))
    mq, pmat, a8 = pl.pallas_call(
        _s5_prep_kernel,
        grid=(dg // CHUNK,),
        in_specs=[blk(1, LANES), blk(1, LANES), blk(1, 1)] + [blk(ch, LANES)] * 4,
        out_specs=[blk(LANES, 2 * LANES), blk(LANES, LANES), blk(3, LANES)],
        out_shape=[jax.ShapeDtypeStruct((dg, LANES, 2 * LANES), F32), jax.ShapeDtypeStruct((dg, LANES, LANES), F32),
                   jax.ShapeDtypeStruct((dg, 3, LANES), F32)],
        name="s5_prep",
    )(dup(a_re), dup(a_im), log_dt.reshape(dg, 1, 1).astype(F32), pair(btr, bti), pair(bti, btr),
      pair(cr, ci), pair(ci, cr))
    shp = lambda x: x.reshape((depth, groups) + x.shape[1:])
    return shp(mq), shp(pmat), shp(a8)


N_MIXER_WEIGHTS = 14


def _moe_combine(x1_ref, yl_ref, route_ref):
    rows = x1_ref.shape[0]
    n_local = yl_ref.shape[1]
    lio = lax.broadcasted_iota(jnp.int32, (MOE_TILE, n_local), 1)
    parts = []
    for hh in range(rows // MOE_TILE):
        r = route_ref[hh * MOE_TILE:(hh + 1) * MOE_TILE, :]
        wperm = jnp.where(lio == r[:, 4:5].astype(jnp.int32), r[:, 2:3],
                          jnp.where(lio == r[:, 5:6].astype(jnp.int32), r[:, 3:4], 0.0)).astype(BF16)
        y_lo, y_hi = _unpack_halves(yl_ref[hh])
        parts.append(jnp.concatenate([_dot(wperm, y_lo), _dot(wperm, y_hi)], axis=-1))
    return x1_ref[...] + jnp.concatenate(parts, axis=0)


def _stream_step(get_x, w_refs, h_init, buf_init, first, pos_base, x1_ref, route_ref, xloc_ref, cnt_ref,
                 h_ref, buf_ref, u_scr, y_scr, cat_scr, *, bblk, tile_l, cfg, precise):
    (gmix_ref, win_ref, mq_ref, p_ref, a8_ref, d_ref, wglu_ref, bglu_ref, wpool_ref, pscale_ref, wout_ref,
     gffn_ref, wr_ref, br_ref) = w_refs
    windows = cfg["windows"]
    n_expert_groups = cfg["n_expert_groups"]
    experts_per_group = cfg["experts_per_group"]
    rows = tile_l * bblk
    nk = tile_l // CHUNK
    n = nk * bblk
    d_model = x1_ref.shape[-1]
    d_ssm = y_scr.shape[-1]
    d_pool = d_model - d_ssm
    hist = POOL_HIST * bblk

    def _init():
        h_ref[...] = h_init()
        cat_scr[0:hist, :] = buf_init().reshape(hist, d_pool)

    if first is True:
        _init()
    else:
        pl.when(first)(_init)

    x = get_x()
    ms = jnp.mean(x * x, axis=-1, keepdims=True)
    h = x * lax.rsqrt(ms + EPS) * gmix_ref[...]
    u = _mm(h, win_ref[...], precise)
    u_scr[...] = u.reshape(nk, CHUNK, bblk, d_model)

    slot = lax.broadcasted_iota(jnp.int32, (n, LANES), 1) // (LANES // CHUNK)
    for v in range(d_ssm // LANES):
        rolled = []
        for j in range(CHUNK):
            uj = u_scr[:, j, :, v * LANES:(v + 1) * LANES].reshape(n, LANES)
            rolled.append(pltpu.roll(uj, (LANES // CHUNK) * j, axis=1) if j else uj)
        ys = []
        for s in range(CHUNK):
            g = v * CHUNK + s
            ug = rolled[0]
            for j in range(1, CHUNK):
                ug = jnp.where(slot == (s + j) % CHUNK, rolled[j], ug)
            yq = _mm(ug, mq_ref[g], precise)
            sg = yq[:, LANES:]
            sg_sw = pltpu.roll(sg, LANES // 2, axis=1)
            a8c = a8_ref[g, 0:1, :]
            a8s = a8_ref[g, 1:2, :]
            a8w = a8_ref[g, 2:3, :]
            hcur = h_ref[g]
            hsw = pltpu.roll(hcur, LANES // 2, axis=1)
            hins = []
            for k in range(nk):
                hins.append(hcur)
                sk = sg[k * bblk:(k + 1) * bblk]
                skw = sg_sw[k * bblk:(k + 1) * bblk]
                hcur, hsw = a8c * hcur + a8s * hsw + sk, a8c * hsw + a8w * hcur + skw
            h_ref[g] = hcur
            hin = jnp.concatenate(hins, axis=0) if nk > 1 else hins[0]
            ys.append(yq[:, :LANES] + _mm(hin, p_ref[g], precise))
        for j in range(CHUNK):
            z = ys[0]
            for s in range(1, CHUNK):
                z = jnp.where(slot == (s + j) % CHUNK, ys[s], z)
            if j:
                z = pltpu.roll(z, LANES - (LANES // CHUNK) * j, axis=1)
            y_scr[:, j, :, v * LANES:(v + 1) * LANES] = z.reshape(nk, bblk, LANES)

    u_ssm = u_scr[:, :, :, 0:d_ssm].reshape(rows, d_ssm)
    y = y_scr[...].reshape(rows, d_ssm) + d_ref[...] * u_ssm
    gl = jax.nn.gelu(y)
    ssm_out = gl * jax.nn.sigmoid(_mm(gl, wglu_ref[...], precise) + bglu_ref[...])

    up = u_scr[:, :, :, d_ssm:].reshape(rows, d_pool)
    cat_scr[hist:hist + rows, :] = up
    row_pos = lax.broadcasted_iota(jnp.int32, (rows, 1), 0) // bblk
    pos = pos_base + row_pos
    pool_parts = []
    pgrp = d_pool // len(windows)
    for kk, w in enumerate(windows):
        lo = kk * pgrp
        acc = cat_scr[hist:hist + rows, lo:lo + pgrp]
        cur = acc
        for sft in range(1, w):
            acc = acc + cat_scr[hist - sft * bblk:hist - sft * bblk + rows, lo:lo + pgrp]
        inv = 1.0 / jnp.minimum(pos + 1, w).astype(F32)
        diff = acc * inv - cur
        pool_parts.append(_mm(diff, wpool_ref[kk], precise))
    pool_out = jnp.concatenate(pool_parts, axis=-1) * pscale_ref[...]
    tail = cat_scr[rows:rows + hist, :]
    cat_scr[0:hist, :] = tail
    buf_ref[...] = tail.reshape(POOL_HIST, bblk, d_pool)

    if precise:
        mix = jnp.concatenate([ssm_out, pool_out], axis=-1)
    else:
        mix = jnp.concatenate([ssm_out.astype(BF16), pool_out.astype(BF16)], axis=-1)
    x1 = x + _mm(mix, wout_ref[...], precise)
    x1_ref[...] = x1
    ms2 = jnp.mean(x1 * x1, axis=-1, keepdims=True)
    t = x1 * lax.rsqrt(ms2 + EPS) * gffn_ref[...]

    t_hi, t_lo = _split_bf16(t)
    w_hi, w_lo = _split_bf16(wr_ref[...])
    hi_both = _dot(t_hi, jnp.concatenate([w_hi, w_lo], axis=-1))
    logits = hi_both[:, :LANES] + hi_both[:, LANES:] + _dot(t_lo, w_hi) + br_ref[...]
    lane = lax.broadcasted_iota(jnp.int32, (rows, LANES), 1).astype(F32)
    big = float(4 * LANES)
    lg = jnp.where(lane < n_expert_groups, logits, NEG)
    mg = jnp.max(lg, axis=-1, keepdims=True)
    gsel = jnp.min(jnp.where(lg == mg, lane, big), axis=-1, keepdims=True)
    p_sel = 1.0 / jnp.sum(jnp.exp(lg - mg), axis=-1, keepdims=True)
    e_lo = n_expert_groups + gsel * experts_per_group
    le = jnp.where((lane >= e_lo) & (lane < e_lo + experts_per_group), logits, NEG)
    v1 = jnp.max(le, axis=-1, keepdims=True)
    i1 = jnp.min(jnp.where(le == v1, lane, big), axis=-1, keepdims=True)
    le2 = jnp.where(lane == i1, NEG, le)
    v2 = jnp.max(le2, axis=-1, keepdims=True)
    i2 = jnp.min(jnp.where(le2 == v2, lane, big), axis=-1, keepdims=True)
    ex = jnp.exp(v2 - v1)
    q1 = p_sel / (1.0 + ex)
    q2 = q1 * ex
    e1 = i1 - n_expert_groups
    e2 = i2 - n_expert_groups
    route = jnp.where(lane == 0.0, e1, jnp.where(lane == 1.0, e2, jnp.where(lane == 2.0, q1,
                      jnp.where(lane == 3.0, q2, 0.0))))

    lf = lax.broadcasted_iota(jnp.int32, (MOE_TILE, LANES), 1).astype(F32)
    n_local = xloc_ref.shape[1]
    before_e = (lax.broadcasted_iota(jnp.int32, (LANES, LANES), 0)
                < lax.broadcasted_iota(jnp.int32, (LANES, LANES), 1)).astype(BF16)
    earlier = (lax.broadcasted_iota(jnp.int32, (MOE_TILE, MOE_TILE), 0)
               > lax.broadcasted_iota(jnp.int32, (MOE_TILE, MOE_TILE), 1)).astype(BF16)
    local_row = lax.broadcasted_iota(jnp.int32, (n_local, MOE_TILE), 0)
    route_parts = []
    for hh in range(rows // MOE_TILE):
        sl = slice(hh * MOE_TILE, (hh + 1) * MOE_TILE)
        rt = route[sl]
        oh1 = lf == rt[:, 0:1]
        oh2 = lf == rt[:, 1:2]
        oh = jnp.logical_or(oh1, oh2).astype(F32)
        cnt = jnp.sum(oh, axis=0, keepdims=True)
        run_chunks = jnp.floor((cnt + (SUBLANES - 1)) * (1.0 / SUBLANES))
        loff = SUBLANES * _dot(jnp.broadcast_to(run_chunks, (SUBLANES, LANES)).astype(BF16), before_e)[0:1]
        where_to = loff + _dot(earlier, oh.astype(BF16))
        lpos1 = jnp.sum(jnp.where(oh1, where_to, 0.0), axis=-1, keepdims=True)
        lpos2 = jnp.sum(jnp.where(oh2, where_to, 0.0), axis=-1, keepdims=True)
        lpos_t = jnp.where(lf == 0.0, lpos1, jnp.where(lf == 1.0, lpos2, 0.0)).T
        perm = jnp.logical_or(local_row == lpos_t[0:1].astype(jnp.int32),
                              local_row == lpos_t[1:2].astype(jnp.int32)).astype(BF16)
        xloc_ref[hh] = _pack_halves(_dot(perm, t_hi[sl]))
        cnt_ref[hh] = cnt
        route_parts.append(jnp.where(lf == 4.0, lpos1, jnp.where(lf == 5.0, lpos2, rt)))
    route_ref[...] = jnp.concatenate(route_parts, axis=0)


def _split_inputs(refs, source, geo):
    if source == "raw":
        xp_ref, xs_ref, meta_ref = refs[:3]
        by_position = lambda ref: (lambda: jnp.concatenate([ref[:, l, :] for l in range(ref.shape[1])], axis=0))
        meta = lambda: jnp.concatenate([jnp.broadcast_to(meta_ref[l:l + 1, :], (geo["m_batch"], meta_ref.shape[1]))
                                        for l in range(meta_ref.shape[0])], axis=0)
        return dict(meta=meta, prompt=by_position(xp_ref), sample=by_position(xs_ref)), refs[3:]
    if source == "pending":
        x_ref, yl_ref, rprev_ref = refs[:3]
        get, refs = (lambda: _moe_combine(x_ref, yl_ref, rprev_ref)), refs[3:]
    else:
        x_ref = refs[0]
        get, refs = (lambda: x_ref[...]), refs[1:]
    return dict(meta=get, prompt=get, sample=get), refs


def _mixer_kernel(*refs, cfg, geo, snapshot, source):
    get_x, refs = _split_inputs(refs, source, geo)
    h0s_ref, buf0s_ref = refs[:2]
    w_refs = refs[2:2 + N_MIXER_WEIGHTS]
    rest = refs[2 + N_MIXER_WEIGHTS:]
    x1_ref, route_ref, xloc_ref, cnt_ref, hp_ref, bufp_ref, hs_ref, bufs_ref = rest[:8]
    rest = rest[8:]
    if snapshot:
        hsnap_ref, bsnap_ref = rest[:2]
        rest = rest[2:]
    hm, bufm, u_m, y_m, cat_m, u_p, y_p, cat_p, u_s, y_s, cat_s = rest
    outs = (x1_ref, route_ref, xloc_ref, cnt_ref)
    s = pl.program_id(0)
    n_p = geo["n_p_steps"]
    batch = geo["batch"]

    @pl.when(s == 0)
    def _meta():
        _stream_step(get_x["meta"], w_refs, lambda: jnp.zeros(hm.shape, F32), lambda: jnp.zeros(bufm.shape, F32), True, 0,
                     *outs, hm, bufm, u_m, y_m, cat_m, bblk=geo["m_batch"], tile_l=geo["n_meta"], cfg=cfg,
                     precise=False)

    @pl.when(jnp.logical_and(s >= 1, s <= n_p))
    def _prompt():
        if snapshot:
            @pl.when(s == n_p)
            def _():
                hsnap_ref[...] = hp_ref[...]
                bsnap_ref[...] = bufp_ref[...]
        _stream_step(get_x["prompt"], w_refs, lambda: hm[:, 0:batch, :], lambda: bufm[:, 0:batch, :], s == 1,
                     geo["n_meta"] + (s - 1) * geo["p_tile_l"], *outs, hp_ref, bufp_ref, u_p, y_p, cat_p,
                     bblk=batch, tile_l=geo["p_tile_l"], cfg=cfg, precise=False)

    @pl.when(s > n_p)
    def _sample():
        _stream_step(get_x["sample"], w_refs, lambda: h0s_ref[...], lambda: buf0s_ref[...], True, PAST_LEN,
                     *outs, hs_ref, bufs_ref, u_s, y_s, cat_s, bblk=geo["s_bblk"], tile_l=geo["dec_seq"], cfg=cfg,
                     precise=False)


def _mixer_weights(lw):
    return [lw["g_mix"], lw["w_in"], lw["mq"], lw["pmat"], lw["a8"], lw["ssm_d"], lw["w_glu"], lw["b_glu"],
            lw["w_pool"], lw["pool_scale"], lw["w_out"], lw["g_ffn"], lw["w_r"], lw["b_r"]]


def _stream_scratch(tile_l, bblk, d_model, d_ssm):
    return [pltpu.VMEM((tile_l // CHUNK, CHUNK, bblk, d_model), F32),
            pltpu.VMEM((tile_l // CHUNK, CHUNK, bblk, d_ssm), F32),
            pltpu.VMEM(((POOL_HIST + tile_l) * bblk, d_model - d_ssm), F32)]


def _input_specs(x_in, pending, geo, d_model, blk, pblk, sblk):
    g = geo
    rows = g["step_rows"]
    if isinstance(x_in, tuple):
        specs = [pl.BlockSpec((g["batch"], g["p_tile_l"], d_model), lambda s: (0, pblk(s), 0)),
                 pl.BlockSpec((g["s_bblk"], g["dec_seq"], d_model), lambda s: (sblk(s), 0, 0)),
                 pl.BlockSpec(x_in[2].shape, lambda s: (0, 0))]
        return specs, list(x_in), "raw"
    specs = [pl.BlockSpec((rows, d_model), lambda s: (blk(s), 0))]
    if pending is None:
        return specs, [x_in], "flat"
    specs += [pl.BlockSpec((rows // MOE_TILE, g["n_local"], d_model // 2), lambda s: (blk(s), 0, 0)),
              pl.BlockSpec((rows, LANES), lambda s: (blk(s), 0))]
    return specs, [x_in, *pending], "pending"


def _layer_spec(a, layer):
    nd = a.ndim
    return pl.BlockSpec((None,) + a.shape[1:], lambda s, _n=nd: (layer,) + (0,) * (_n - 1))


def _mixer_call(x_in, pending, h0s, buf0s, lw, *, geo, cfg, snapshot, layer):
    d_model = lw["w_in"].shape[1]
    total = (geo["n_p_steps"] + geo["n_s_steps"] + 1) * geo["step_rows"]
    groups, dec_batch, _ = h0s.shape
    d_pool = buf0s.shape[-1]
    d_ssm = d_model - d_pool
    g = geo
    n_p, n_s = g["n_p_steps"], g["n_s_steps"]
    rows = g["step_rows"]
    per_step = rows // MOE_TILE
    n_local = g["n_local"]
    blk = lambda s: jnp.where(s == 0, n_p + n_s, s - 1)
    sblk = lambda s: jnp.clip(s - (n_p + 1), 0, n_s - 1)
    weights = _mixer_weights(lw)
    h_s_spec = pl.BlockSpec((groups, g["s_bblk"], LANES), lambda s: (0, sblk(s), 0))
    buf_s_spec = pl.BlockSpec((POOL_HIST, g["s_bblk"], d_pool), lambda s: (0, sblk(s), 0))
    x_specs, x_args, source = _input_specs(x_in, pending, geo, d_model, blk,
                                           lambda s: jnp.clip(s - 1, 0, n_p - 1), sblk)
    in_specs = x_specs + [h_s_spec, buf_s_spec] + [_layer_spec(w, layer) for w in weights]
    h_p_shape, buf_p_shape = (groups, g["batch"], LANES), (POOL_HIST, g["batch"], d_pool)
    out_specs = [pl.BlockSpec((rows, d_model), lambda s: (blk(s), 0)),
                 pl.BlockSpec((rows, LANES), lambda s: (blk(s), 0)),
                 pl.BlockSpec((per_step, n_local, d_model // 2), lambda s: (blk(s), 0, 0)),
                 pl.BlockSpec((per_step, 1, LANES), lambda s: (blk(s), 0, 0)),
                 pl.BlockSpec(h_p_shape, lambda s: (0, 0, 0)), pl.BlockSpec(buf_p_shape, lambda s: (0, 0, 0)),
                 h_s_spec, buf_s_spec]
    out_shape = [jax.ShapeDtypeStruct((total, d_model), F32), jax.ShapeDtypeStruct((total, LANES), F32),
                 jax.ShapeDtypeStruct((total // MOE_TILE, n_local, d_model // 2), jnp.uint32),
                 jax.ShapeDtypeStruct((total // MOE_TILE, 1, LANES), F32),
                 jax.ShapeDtypeStruct(h_p_shape, F32), jax.ShapeDtypeStruct(buf_p_shape, F32),
                 jax.ShapeDtypeStruct((groups, dec_batch, LANES), F32),
                 jax.ShapeDtypeStruct((POOL_HIST, dec_batch, d_pool), F32)]
    if snapshot:
        out_specs += [pl.BlockSpec(h_p_shape, lambda s: (0, 0, 0)), pl.BlockSpec(buf_p_shape, lambda s: (0, 0, 0))]
        out_shape += [jax.ShapeDtypeStruct(h_p_shape, F32), jax.ShapeDtypeStruct(buf_p_shape, F32)]
    scratch = ([pltpu.VMEM((groups, g["m_batch"], LANES), F32), pltpu.VMEM((POOL_HIST, g["m_batch"], d_pool), F32)]
               + _stream_scratch(g["n_meta"], g["m_batch"], d_model, d_ssm)
               + _stream_scratch(g["p_tile_l"], g["batch"], d_model, d_ssm)
               + _stream_scratch(g["dec_seq"], g["s_bblk"], d_model, d_ssm))
    return pl.pallas_call(
        functools.partial(_mixer_kernel, cfg=cfg, geo=geo, snapshot=snapshot, source=source),
        grid=(1 + n_p + n_s,),
        in_specs=in_specs,
        out_specs=out_specs,
        out_shape=out_shape,
        scratch_shapes=scratch,
        compiler_params=pltpu.CompilerParams(dimension_semantics=("arbitrary",), vmem_limit_bytes=VMEM_LIMIT),
        name="mixer",
    )(*x_args, h0s, buf0s, *weights)


def _mixer_tail_kernel(*refs, cfg, geo, source):
    get_x, refs = _split_inputs(refs, source, geo)
    get_x = get_x["prompt"]
    h0_ref, buf0_ref = refs[:2]
    w_refs = refs[2:2 + N_MIXER_WEIGHTS]
    rest = refs[2 + N_MIXER_WEIGHTS + 4:]
    x1_ref, route_ref, xloc_ref, cnt_ref, h_ref, buf_ref, u_scr, y_scr, cat_scr = rest
    pos = geo["n_meta"] + (geo["n_p_steps"] - 1) * geo["p_tile_l"]
    _stream_step(get_x, w_refs, lambda: h0_ref[...], lambda: buf0_ref[...], True, pos, x1_ref, route_ref, xloc_ref,
                 cnt_ref, h_ref, buf_ref, u_scr, y_scr, cat_scr, bblk=geo["batch"], tile_l=geo["p_tile_l"], cfg=cfg,
                 precise=True)


def _mixer_tail_call(x_in, pending, h0, buf0, lw32, x1, route, xloc, cnt, *, geo, cfg, layer):
    d_model = x1.shape[-1]
    d_pool = buf0.shape[-1]
    g = geo
    rows = g["step_rows"]
    per_step = rows // MOE_TILE
    last = g["n_p_steps"] - 1
    weights = _mixer_weights(lw32)

    def full(a):
        nd = a.ndim
        return pl.BlockSpec(a.shape, lambda s, _n=nd: (0,) * _n)

    x_specs, x_args, source = _input_specs(x_in, pending, geo, d_model, lambda s: last, lambda s: last, lambda s: 0)
    n_in = len(x_args) + 2 + len(weights)
    return pl.pallas_call(
        functools.partial(_mixer_tail_kernel, cfg=cfg, geo=geo, source=source),
        grid=(1,),
        in_specs=(x_specs + [full(h0), full(buf0)] + [_layer_spec(w, layer) for w in weights]
                  + [pl.BlockSpec(memory_space=pl.ANY)] * 4),
        out_specs=[pl.BlockSpec((rows, d_model), lambda s: (last, 0)),
                   pl.BlockSpec((rows, LANES), lambda s: (last, 0)),
                   pl.BlockSpec((per_step, g["n_local"], d_model // 2), lambda s: (last, 0, 0)),
                   pl.BlockSpec((per_step, 1, LANES), lambda s: (last, 0, 0)),
                   full(h0), full(buf0)],
        out_shape=[jax.ShapeDtypeStruct(a.shape, a.dtype) for a in (x1, route, xloc, cnt, h0, buf0)],
        scratch_shapes=_stream_scratch(g["p_tile_l"], g["batch"], d_model, d_model - d_pool),
        input_output_aliases={n_in + k: k for k in range(4)},
        compiler_params=pltpu.CompilerParams(dimension_semantics=("arbitrary",), vmem_limit_bytes=VMEM_LIMIT),
        name="mixer_tail",
    )(*x_args, h0, buf0, *weights, x1, route, xloc, cnt)


def _experts_kernel(exp_ref, nact_ref, src_ref, nval_ref, xl_in_ref, wg_ref, wu_ref, wd_ref, rows_ref,
                    xbuf, ybuf, wgu_scr, wd_scr, sem_in, sem_out):
    del xl_in_ref
    j = pl.program_id(0)
    f = wg_ref.shape[-1]
    chunks_per_tile = EXPERT_TILE // SUBLANES
    nact = nact_ref[0]

    def for_tile_chunks(tile, op, scatter):
        slot = tile % 2

        def body(k, carry):
            hbm = rows_ref.at[pl.ds(pl.multiple_of(src_ref[tile * chunks_per_tile + k], SUBLANES), SUBLANES), :]
            local = pl.ds(pl.multiple_of(k * SUBLANES, SUBLANES), SUBLANES)
            if scatter:
                op(pltpu.make_async_copy(ybuf.at[slot, local, :], hbm, sem_out.at[slot]))
            else:
                op(pltpu.make_async_copy(hbm, xbuf.at[slot, local, :], sem_in.at[slot]))
            return carry

        lax.fori_loop(0, nval_ref[tile], body, 0)

    def wait_tile_chunks(tile, scatter):
        slot = tile % 2
        full = nval_ref[tile] == chunks_per_tile

        @pl.when(full)
        def _():
            whole = rows_ref.at[pl.ds(0, EXPERT_TILE), :]
            if scatter:
                pltpu.make_async_copy(ybuf.at[slot], whole, sem_out.at[slot]).wait()
            else:
                pltpu.make_async_copy(whole, xbuf.at[slot], sem_in.at[slot]).wait()

        @pl.when(jnp.logical_not(full))
        def _():
            for_tile_chunks(tile, wait, scatter)

    start = lambda cp: cp.start()
    wait = lambda cp: cp.wait()

    @pl.when(j == 0)
    def _():
        xbuf[...] = jnp.zeros_like(xbuf)
        for_tile_chunks(0, start, False)

    @pl.when(j + 1 < nact)
    def _():
        for_tile_chunks(j + 1, start, False)

    @pl.when(jnp.logical_and(j >= 2, j - 2 < nact))
    def _():
        wait_tile_chunks(j - 2, True)

    active = j < nact
    changed = jnp.logical_or(j == 0, exp_ref[j] != exp_ref[jnp.maximum(j - 1, 0)])

    @pl.when(jnp.logical_and(active, changed))
    def _():
        wgu_scr[:, 0:f] = wg_ref[...].astype(BF16)
        wgu_scr[:, f:2 * f] = wu_ref[...].astype(BF16)
        wd_scr[...] = wd_ref[...].astype(BF16)

    @pl.when(active)
    def _():
        wait_tile_chunks(j, False)
        half = wgu_scr.shape[0] // 2
        t_lo, t_hi = _unpack_halves(xbuf[j % 2])
        gu = _dot(t_lo, wgu_scr[0:half, :]) + _dot(t_hi, wgu_scr[half:2 * half, :])
        act = jax.nn.silu(gu[:, 0:f]) * gu[:, f:2 * f]
        ybuf[j % 2] = _pack_halves(_dot(act.astype(BF16), wd_scr[...]))
        for_tile_chunks(j, start, True)


def _experts_call(tile_exp, n_active, src, nval, xloc, w_gate, w_up, w_down, layer, n_tiles):
    _, n_exp, d_model, f = w_gate.shape
    xl_flat = xloc.reshape(-1, xloc.shape[-1])
    wspec = lambda a, b: pl.BlockSpec((None, None, a, b), lambda i, ex, *_: (layer, ex[i], 0, 0))
    buf = pltpu.VMEM((2, EXPERT_TILE, d_model // 2), jnp.uint32)
    out = pl.pallas_call(
        _experts_kernel,
        grid_spec=pltpu.PrefetchScalarGridSpec(
            num_scalar_prefetch=4,
            grid=(n_tiles + 2,),
            in_specs=[pl.BlockSpec(memory_space=pl.ANY), wspec(d_model, f), wspec(d_model, f), wspec(f, d_model)],
            out_specs=pl.BlockSpec(memory_space=pl.ANY),
            scratch_shapes=[buf, buf, pltpu.VMEM((d_model, 2 * f), BF16), pltpu.VMEM((f, d_model), BF16),
                            pltpu.SemaphoreType.DMA((2,)), pltpu.SemaphoreType.DMA((2,))],
        ),
        out_shape=jax.ShapeDtypeStruct(xl_flat.shape, xl_flat.dtype),
        input_output_aliases={4: 0},
        compiler_params=pltpu.CompilerParams(dimension_semantics=("arbitrary",), vmem_limit_bytes=VMEM_LIMIT),
        name="moe_experts",
    )(tile_exp, n_active, src, nval, xl_flat, w_gate, w_up, w_down)
    return out.reshape(xloc.shape)


def _moe_experts(xloc, cnt, w_gate, w_up, w_down, *, layer):
    n_exp = w_gate.shape[1]
    nt, n_local, _ = xloc.shape
    total = nt * MOE_TILE
    cpt = EXPERT_TILE // SUBLANES
    n_chunks = n_local // SUBLANES
    counts = cnt[:, 0, :n_exp].astype(jnp.int32)
    run = (counts + SUBLANES - 1) // SUBLANES
    lend = jnp.cumsum(run, axis=1)
    loff = lend - run
    gend = jnp.cumsum(run, axis=0)
    gcum = gend - run
    seg = gend[-1]
    seg_tiles = (seg + cpt - 1) // cpt
    seg_end = jnp.cumsum(seg_tiles)
    seg_off = seg_end - seg_tiles
    n_tiles = (2 * total + nt * n_exp * (SUBLANES - 1) + n_exp * (EXPERT_TILE - 1)) // EXPERT_TILE + 1
    n_steps = n_tiles + 2
    n_active = seg_end[-1].reshape(1).astype(jnp.int32)
    tile_idx = jnp.minimum(jnp.arange(n_steps, dtype=jnp.int32), n_active - 1)
    tile_exp = jnp.minimum(jnp.sum((tile_idx[:, None] >= seg_end[None, :]).astype(jnp.int32), axis=1), n_exp - 1)
    is_e = tile_exp[:, None] == jnp.arange(n_exp, dtype=jnp.int32)[None, :]
    pick = lambda table: jnp.sum(jnp.where(is_e[:, None, :], table[None, :, :], 0), axis=2)
    pick1 = lambda vec: jnp.sum(jnp.where(is_e, vec[None, :], 0), axis=1)
    g = ((jnp.arange(n_steps, dtype=jnp.int32) - pick1(seg_off)) * cpt)[:, None] + jnp.arange(cpt, dtype=jnp.int32)
    gend_t, gcum_t, loff_t = pick(gend), pick(gcum), pick(loff)
    owner = jnp.logical_and(g[:, :, None] >= gcum_t[:, None, :], g[:, :, None] < gend_t[:, None, :])
    tile_base = jnp.arange(nt, dtype=jnp.int32) * n_chunks
    src = jnp.sum(jnp.where(owner, (tile_base[None, :] + loff_t - gcum_t)[:, None, :] + g[:, :, None], 0), axis=2)
    src = (src * SUBLANES).reshape(-1)
    nval = jnp.clip(pick1(seg) - g[:, 0], 0, cpt)
    return _experts_call(tile_exp, n_active, src, nval, xloc, w_gate, w_up, w_down, layer, n_tiles)


def _finish_kernel(x1_ref, yl_ref, route_ref, gfin_ref, yp_ref, ys_ref, *, n_p_steps):
    i = pl.program_id(0)
    x = _moe_combine(x1_ref, yl_ref, route_ref)
    ms = jnp.mean(x * x, axis=-1, keepdims=True)
    y = x * lax.rsqrt(ms + EPS) * gfin_ref[...]
    pb, pl_ = yp_ref.shape[0], yp_ref.shape[1]
    sb, sl = ys_ref.shape[0], ys_ref.shape[1]

    @pl.when(i < n_p_steps)
    def _():
        for l in range(pl_):
            yp_ref[:, l, :] = y[l * pb:(l + 1) * pb, :]

    @pl.when(i >= n_p_steps)
    def _():
        for l in range(sl):
            ys_ref[:, l, :] = y[l * sb:(l + 1) * sb, :]


def _finish_call(x1, yl, route, g_final, *, geo):
    d_model = x1.shape[-1]
    g = geo
    rows = g["step_rows"]
    n_p, n_s = g["n_p_steps"], g["n_s_steps"]
    return pl.pallas_call(
        functools.partial(_finish_kernel, n_p_steps=n_p),
        grid=(n_p + n_s,),
        in_specs=[pl.BlockSpec((rows, d_model), lambda i: (i, 0)),
                  pl.BlockSpec((rows // MOE_TILE, g["n_local"], d_model // 2), lambda i: (i, 0, 0)),
                  pl.BlockSpec((rows, LANES), lambda i: (i, 0)),
                  pl.BlockSpec((1, d_model), lambda i: (0, 0))],
        out_specs=[pl.BlockSpec((g["batch"], g["p_tile_l"], d_model), lambda i: (0, jnp.minimum(i, n_p - 1), 0)),
                   pl.BlockSpec((g["s_bblk"], g["dec_seq"], d_model), lambda i: (jnp.clip(i - n_p, 0, n_s - 1), 0, 0))],
        out_shape=[jax.ShapeDtypeStruct((g["batch"], g["seq"], d_model), F32),
                   jax.ShapeDtypeStruct((g["dec_batch"], g["dec_seq"], d_model), F32)],
        compiler_params=pltpu.CompilerParams(dimension_semantics=("arbitrary",), vmem_limit_bytes=VMEM_LIMIT),
        name="finish",
    )(x1, yl, route, g_final)


def kernel(x_prompt, x_sample, state_ssm_re, state_ssm_im, state_pool, meta_tokens, norm_mix_g, w_in, ssm_a_re,
           ssm_a_im, ssm_log_dt, ssm_b_re, ssm_b_im, ssm_c_re, ssm_c_im, ssm_d, w_glu, b_glu, w_pool, pool_scale,
           w_out, norm_ffn_g, w_router_group, b_router_group, w_router_expert, b_router_expert, w_gate, w_up,
           w_down, norm_final_g):
    batch, seq, d_model = x_prompt.shape
    dec_batch, dec_seq, _ = x_sample.shape
    depth, _, groups, nstate = state_ssm_re.shape
    n_meta = meta_tokens.shape[0]
    pool_buf, d_pool = state_pool.shape[2], state_pool.shape[3]
    n_pool_groups = w_pool.shape[1]
    windows = tuple(2 ** (k + 1) for k in range(n_pool_groups))
    assert pool_buf == max(windows) - 1 == POOL_HIST - 1
    n_expert_groups = w_router_group.shape[-1]
    n_experts = w_router_expert.shape[-1]
    cfg = dict(windows=windows, n_expert_groups=n_expert_groups, experts_per_group=n_experts // n_expert_groups)

    step_rows = 512
    p_tile_l, s_bblk, m_batch = step_rows // batch, step_rows // dec_seq, step_rows // n_meta
    n_p, n_s, n_m = batch * seq, dec_batch * dec_seq, m_batch * n_meta
    assert m_batch >= batch and seq % p_tile_l == 0 and dec_batch % s_bblk == 0
    total = n_p + n_s + n_m
    assert total % step_rows == 0 and step_rows % MOE_TILE == 0
    n_local = 2 * MOE_TILE + n_experts * (SUBLANES - 1)
    n_local = (n_local + LANES - 1) // LANES * LANES
    geo = dict(batch=batch, seq=seq, dec_batch=dec_batch, dec_seq=dec_seq, n_meta=n_meta, step_rows=step_rows,
               p_tile_l=p_tile_l, s_bblk=s_bblk, m_batch=m_batch, n_p_steps=seq // p_tile_l,
               n_s_steps=dec_batch // s_bblk, n_local=n_local)

    mq, pmat, a8 = _s5_prepare(ssm_a_re, ssm_a_im, ssm_log_dt, ssm_b_re, ssm_b_im, ssm_c_re, ssm_c_im)

    x = (x_prompt.astype(F32), x_sample.astype(F32), meta_tokens.astype(F32))

    row2 = lambda v: v.reshape(1, -1).astype(F32)
    outs = {k: [] for k in ("hp", "bp", "hs", "bs")}
    pending = None
    rows3 = lambda v: v.reshape(depth, 1, -1).astype(F32)
    w_r = jnp.concatenate([w_router_group, w_router_expert], axis=2).astype(F32)
    w_r = jnp.pad(w_r, ((0, 0), (0, 0), (0, LANES - w_r.shape[2])))
    b_r = jnp.concatenate([b_router_group, b_router_expert], axis=1).astype(F32)
    lw32 = dict(g_mix=rows3(norm_mix_g), w_in=w_in.astype(F32), mq=mq, pmat=pmat, a8=a8, ssm_d=rows3(ssm_d),
                w_glu=w_glu.astype(F32), b_glu=rows3(b_glu), w_pool=w_pool.astype(F32), pool_scale=rows3(pool_scale),
                w_out=w_out.astype(F32), g_ffn=rows3(norm_ffn_g), w_r=w_r,
                b_r=jnp.pad(b_r, ((0, 0), (0, LANES - b_r.shape[1])))[:, None, :])
    lw = dict(lw32, **{k: lw32[k].astype(BF16) for k in ("w_in", "mq", "pmat", "w_glu", "w_pool", "w_out")})
    for l in range(depth):
        h0_s = jnp.concatenate([state_ssm_re[l], state_ssm_im[l]], axis=-1).transpose(1, 0, 2).astype(F32)
        buf0_s = jnp.pad(state_pool[l].astype(F32).transpose(1, 0, 2), ((1, 0), (0, 0), (0, 0)))
        redo_last = l < depth - 1
        res = _mixer_call(x, pending, h0_s, buf0_s, lw, geo=geo, cfg=cfg, snapshot=redo_last, layer=l)
        x1, route, xloc, cnt, h_p, buf_p, h_s, buf_s = res[:8]
        if redo_last:
            x1, route, xloc, cnt, h_p, buf_p = _mixer_tail_call(x, pending, res[8], res[9], lw32, x1, route, xloc,
                                                                 cnt, geo=geo, cfg=cfg, layer=l)
        outs["hp"].append(h_p)
        outs["bp"].append(buf_p)
        outs["hs"].append(h_s)
        outs["bs"].append(buf_s)
        x, pending = x1, (_moe_experts(xloc, cnt, w_gate, w_up, w_down, layer=l), route)

    y_prompt, y_sample = _finish_call(x, pending[0], pending[1], row2(norm_final_g), geo=geo)
    st = lambda hs: jnp.stack(hs).transpose(0, 2, 1, 3)
    pl_out = lambda bs: jnp.stack(bs)[:, 1:].transpose(0, 2, 1, 3)
    hp, hs = st(outs["hp"]), st(outs["hs"])
    return (y_prompt, y_sample, hp[..., :nstate], hp[..., nstate:], pl_out(outs["bp"]),
            hs[..., :nstate], hs[..., nstate:], pl_out(outs["bs"]))
```

```python
import functools

import numpy as np
import jax
import jax.numpy as jnp
from jax import lax
from jax.experimental import pallas as pl
from jax.experimental.pallas import tpu as pltpu

F32 = jnp.float32
BF16 = jnp.bfloat16

LANES = 128
SUBLANES = 8
CHUNK = 8
POOL_HIST = 16
MOE_TILE = 256
EXPERT_TILE = 512
VMEM_LIMIT = 56 * 1024 * 1024
EPS = 1e-6
NEG = -1e30
PAST_LEN = 16384


def _dot(a, b):
    return jnp.dot(a, b, preferred_element_type=F32)


def _split_bf16(x):
    hi = x.astype(BF16)
    return hi, (x - hi.astype(F32)).astype(BF16)


def _dot3(a, b, dims=(((1,), (0,)), ((), ()))):
    a_hi, a_lo = _split_bf16(a)
    b_hi, b_lo = _split_bf16(b)
    dg = lambda x, y: lax.dot_general(x, y, dims, preferred_element_type=F32)
    return dg(a_hi, b_hi) + dg(a_lo, b_hi) + dg(a_hi, b_lo)


def _mm(a, w, precise):
    if precise:
        return _dot3(a, w)
    return _dot(a.astype(BF16), w)


def _pack_halves(x):
    k = x.shape[-1] // 2
    return pltpu.pack_elementwise([x[:, :k], x[:, k:]], packed_dtype=BF16)


def _unpack_halves(p):
    lo = pltpu.unpack_elementwise(p, index=0, packed_dtype=BF16, unpacked_dtype=F32)
    hi = pltpu.unpack_elementwise(p, index=1, packed_dtype=BF16, unpacked_dtype=F32)
    return lo.astype(BF16), hi.astype(BF16)


def _dot_nt_f32(a, b):
    return _dot3(a, b, (((1,), (1,)), ((), ())))


def _s5_prep_kernel(a_re_ref, a_im_ref, ldt_ref, bt_ref, btx_ref, c_ref, cx_ref, mq_ref, p_ref, a8_ref):
    ch = bt_ref.shape[1]
    first_half = lax.broadcasted_iota(jnp.int32, (1, LANES), 1) < LANES // 2
    sgn = jnp.where(first_half, 1.0, -1.0)
    for s in range(CHUNK):
        a_re = a_re_ref[s]
        a_im = a_im_ref[s]
        dt = jnp.exp(ldt_ref[s])
        mag = jnp.exp(dt * a_re)
        abr = mag * jnp.cos(dt * a_im)
        abi = mag * jnp.sin(dt * a_im)
        den = a_re * a_re + a_im * a_im
        nr = abr - 1.0
        f_re = (nr * a_re + abi * a_im) / den
        f_im = (abi * a_re - nr * a_im) / den
        bb = f_re * bt_ref[s] - sgn * f_im * btx_ref[s]
        cc = c_ref[s] * sgn
        ccx = -cx_ref[s]
        bbx = f_re * btx_ref[s] + sgn * f_im * bt_ref[s]
        pr = jnp.ones_like(a_re)
        pi = jnp.zeros_like(a_re)
        e2, w2 = [], []
        for k in range(CHUNK + 1):
            e2.append(pr * cc + pi * ccx)
            w2.append(pr * bb - sgn * pi * bbx)
            if k < CHUNK:
                pr, pi = pr * abr - pi * abi, pr * abi + pi * abr
        a8_ref[s, 0:1, :] = pr
        a8_ref[s, 1:2, :] = -sgn * pi
        a8_ref[s, 2:3, :] = sgn * pi
        zero = jnp.zeros_like(bb)
        jpos = [(p - s) % CHUNK for p in range(CHUNK)]
        for p in range(CHUNK):
            j = jpos[p]
            taps = jnp.concatenate([e2[jp - j] if jp >= j else zero for jp in jpos], axis=0)
            rows = slice(p * ch, (p + 1) * ch)
            mq_ref[s, rows, 0:LANES] = _dot_nt_f32(bb, taps)
            mq_ref[s, rows, LANES:2 * LANES] = w2[CHUNK - 1 - j]
        pt = jnp.concatenate([e2[jp + 1] for jp in jpos], axis=0)
        p_ref[s] = pt.T


def _s5_prepare(a_re, a_im, log_dt, b_re, b_im, c_re, c_im):
    depth, groups, nstate = a_re.shape
    ch = b_re.shape[-1]
    dg = depth * groups
    assert ch * CHUNK == LANES and 2 * nstate == LANES and groups % CHUNK == 0
    dup = lambda x: jnp.concatenate([x, x], axis=-1).reshape(dg, 1, LANES).astype(F32)
    btr = jnp.swapaxes(b_re.reshape(dg, nstate, ch), 1, 2).astype(F32)
    bti = jnp.swapaxes(b_im.reshape(dg, nstate, ch), 1, 2).astype(F32)
    cr = c_re.reshape(dg, ch, nstate).astype(F32)
    ci = c_im.reshape(dg, ch, nstate).astype(F32)
    pair = lambda x, y: jnp.concatenate([x, y], axis=-1)
    blk = lambda *shape: pl.BlockSpec((CHUNK,) + shape, lambda g: (g,) + (0,) * len(shape))
    mq, pmat, a8 = pl.pallas_call(
        _s5_prep_kernel,
        grid=(dg // CHUNK,),
        in_specs=[blk(1, LANES), blk(1, LANES), blk(1, 1)] + [blk(ch, LANES)] * 4,
        out_specs=[blk(LANES, 2 * LANES), blk(LANES, LANES), blk(3, LANES)],
        out_shape=[jax.ShapeDtypeStruct((dg, LANES, 2 * LANES), F32), jax.ShapeDtypeStruct((dg, LANES, LANES), F32),
                   jax.ShapeDtypeStruct((dg, 3, LANES), F32)],
        name="s5_prep",
    )(dup(a_re), dup(a_im), log_dt.reshape(dg, 1, 1).astype(F32), pair(btr, bti), pair(bti, btr),
      pair(cr, ci), pair(ci, cr))
    shp = lambda x: x.reshape((depth, groups) + x.shape[1:])
    return shp(mq), shp(pmat), shp(a8)


N_MIXER_WEIGHTS = 14


def _moe_combine(x1_ref, yl_ref, route_ref):
    rows = x1_ref.shape[0]
    n_local = yl_ref.shape[1]
    lio = lax.broadcasted_iota(jnp.int32, (MOE_TILE, n_local), 1)
    parts = []
    for hh in range(rows // MOE_TILE):
        r = route_ref[hh * MOE_TILE:(hh + 1) * MOE_TILE, :]
        wperm = jnp.where(lio == r[:, 4:5].astype(jnp.int32), r[:, 2:3],
                          jnp.where(lio == r[:, 5:6].astype(jnp.int32), r[:, 3:4], 0.0)).astype(BF16)
        y_lo, y_hi = _unpack_halves(yl_ref[hh])
        parts.append(jnp.concatenate([_dot(wperm, y_lo), _dot(wperm, y_hi)], axis=-1))
    return x1_ref[...] + jnp.concatenate(parts, axis=0)


def _stream_step(get_x, w_refs, h_init, buf_init, first, pos_base, x1_ref, route_ref, xloc_ref, cnt_ref,
                 h_ref, buf_ref, u_scr, y_scr, cat_scr, *, bblk, tile_l, cfg, precise):
    (gmix_ref, win_ref, mq_ref, p_ref, a8_ref, d_ref, wglu_ref, bglu_ref, wpool_ref, pscale_ref, wout_ref,
     gffn_ref, wr_ref, br_ref) = w_refs
    windows = cfg["windows"]
    n_expert_groups = cfg["n_expert_groups"]
    experts_per_group = cfg["experts_per_group"]
    rows = tile_l * bblk
    nk = tile_l // CHUNK
    n = nk * bblk
    d_model = x1_ref.shape[-1]
    d_ssm = y_scr.shape[-1]
    d_pool = d_model - d_ssm
    hist = POOL_HIST * bblk

    def _init():
        h_ref[...] = h_init()
        cat_scr[0:hist, :] = buf_init().reshape(hist, d_pool)

    if first is True:
        _init()
    else:
        pl.when(first)(_init)

    x = get_x()
    ms = jnp.mean(x * x, axis=-1, keepdims=True)
    h = x * lax.rsqrt(ms + EPS) * gmix_ref[...]
    u = _mm(h, win_ref[...], precise)
    u_scr[...] = u.reshape(nk, CHUNK, bblk, d_model)

    slot = lax.broadcasted_iota(jnp.int32, (n, LANES), 1) // (LANES // CHUNK)
    for v in range(d_ssm // LANES):
        rolled = []
        for j in range(CHUNK):
            uj = u_scr[:, j, :, v * LANES:(v + 1) * LANES].reshape(n, LANES)
            rolled.append(pltpu.roll(uj, (LANES // CHUNK) * j, axis=1) if j else uj)
        ys = []
        for s in range(CHUNK):
            g = v * CHUNK + s
            ug = rolled[0]
            for j in range(1, CHUNK):
                ug = jnp.where(slot == (s + j) % CHUNK, rolled[j], ug)
            yq = _mm(ug, mq_ref[g], precise)
            sg = yq[:, LANES:]
            sg_sw = pltpu.roll(sg, LANES // 2, axis=1)
            a8c = a8_ref[g, 0:1, :]
            a8s = a8_ref[g, 1:2, :]
            a8w = a8_ref[g, 2:3, :]
            hcur = h_ref[g]
            hsw = pltpu.roll(hcur, LANES // 2, axis=1)
            hins = []
            for k in range(nk):
                hins.append(hcur)
                sk = sg[k * bblk:(k + 1) * bblk]
                skw = sg_sw[k * bblk:(k + 1) * bblk]
                hcur, hsw = a8c * hcur + a8s * hsw + sk, a8c * hsw + a8w * hcur + skw
            h_ref[g] = hcur
            hin = jnp.concatenate(hins, axis=0) if nk > 1 else hins[0]
            ys.append(yq[:, :LANES] + _mm(hin, p_ref[g], precise))
        for j in range(CHUNK):
            z = ys[0]
            for s in range(1, CHUNK):
                z = jnp.where(slot == (s + j) % CHUNK, ys[s], z)
            if j:
                z = pltpu.roll(z, LANES - (LANES // CHUNK) * j, axis=1)
            y_scr[:, j, :, v * LANES:(v + 1) * LANES] = z.reshape(nk, bblk, LANES)

    u_ssm = u_scr[:, :, :, 0:d_ssm].reshape(rows, d_ssm)
    y = y_scr[...].reshape(rows, d_ssm) + d_ref[...] * u_ssm
    gl = jax.nn.gelu(y)
    ssm_out = gl * jax.nn.sigmoid(_mm(gl, wglu_ref[...], precise) + bglu_ref[...])

    up = u_scr[:, :, :, d_ssm:].reshape(rows, d_pool)
    cat_scr[hist:hist + rows, :] = up
    row_pos = lax.broadcasted_iota(jnp.int32, (rows, 1), 0) // bblk
    pos = pos_base + row_pos
    pool_parts = []
    pgrp = d_pool // len(windows)
    for kk, w in enumerate(windows):
        lo = kk * pgrp
        acc = cat_scr[hist:hist + rows, lo:lo + pgrp]
        cur = acc
        for sft in range(1, w):
            acc = acc + cat_scr[hist - sft * bblk:hist - sft * bblk + rows, lo:lo + pgrp]
        inv = 1.0 / jnp.minimum(pos + 1, w).astype(F32)
        diff = acc * inv - cur
        pool_parts.append(_mm(diff, wpool_ref[kk], precise))
    pool_out = jnp.concatenate(pool_parts, axis=-1) * pscale_ref[...]
    tail = cat_scr[rows:rows + hist, :]
    cat_scr[0:hist, :] = tail
    buf_ref[...] = tail.reshape(POOL_HIST, bblk, d_pool)

    if precise:
        mix = jnp.concatenate([ssm_out, pool_out], axis=-1)
    else:
        mix = jnp.concatenate([ssm_out.astype(BF16), pool_out.astype(BF16)], axis=-1)
    x1 = x + _mm(mix, wout_ref[...], precise)
    x1_ref[...] = x1
    ms2 = jnp.mean(x1 * x1, axis=-1, keepdims=True)
    t = x1 * lax.rsqrt(ms2 + EPS) * gffn_ref[...]

    t_hi, t_lo = _split_bf16(t)
    w_hi, w_lo = _split_bf16(wr_ref[...])
    hi_both = _dot(t_hi, jnp.concatenate([w_hi, w_lo], axis=-1))
    logits = hi_both[:, :LANES] + hi_both[:, LANES:] + _dot(t_lo, w_hi) + br_ref[...]
    lane = lax.broadcasted_iota(jnp.int32, (rows, LANES), 1).astype(F32)
    big = float(4 * LANES)
    lg = jnp.where(lane < n_expert_groups, logits, NEG)
    mg = jnp.max(lg, axis=-1, keepdims=True)
    gsel = jnp.min(jnp.where(lg == mg, lane, big), axis=-1, keepdims=True)
    p_sel = 1.0 / jnp.sum(jnp.exp(lg - mg), axis=-1, keepdims=True)
    e_lo = n_expert_groups + gsel * experts_per_group
    le = jnp.where((lane >= e_lo) & (lane < e_lo + experts_per_group), logits, NEG)
    v1 = jnp.max(le, axis=-1, keepdims=True)
    i1 = jnp.min(jnp.where(le == v1, lane, big), axis=-1, keepdims=True)
    le2 = jnp.where(lane == i1, NEG, le)
    v2 = jnp.max(le2, axis=-1, keepdims=True)
    i2 = jnp.min(jnp.where(le2 == v2, lane, big), axis=-1, keepdims=True)
    ex = jnp.exp(v2 - v1)
    q1 = p_sel / (1.0 + ex)
    q2 = q1 * ex
    e1 = i1 - n_expert_groups
    e2 = i2 - n_expert_groups
    route = jnp.where(lane == 0.0, e1, jnp.where(lane == 1.0, e2, jnp.where(lane == 2.0, q1,
                      jnp.where(lane == 3.0, q2, 0.0))))

    lf = lax.broadcasted_iota(jnp.int32, (MOE_TILE, LANES), 1).astype(F32)
    n_local = xloc_ref.shape[1]
    before_e = (lax.broadcasted_iota(jnp.int32, (LANES, LANES), 0)
                < lax.broadcasted_iota(jnp.int32, (LANES, LANES), 1)).astype(BF16)
    earlier = (lax.broadcasted_iota(jnp.int32, (MOE_TILE, MOE_TILE), 0)
               > lax.broadcasted_iota(jnp.int32, (MOE_TILE, MOE_TILE), 1)).astype(BF16)
    local_row = lax.broadcasted_iota(jnp.int32, (n_local, MOE_TILE), 0)
    route_parts = []
    for hh in range(rows // MOE_TILE):
        sl = slice(hh * MOE_TILE, (hh + 1) * MOE_TILE)
        rt = route[sl]
        oh1 = lf == rt[:, 0:1]
        oh2 = lf == rt[:, 1:2]
        oh = jnp.logical_or(oh1, oh2).astype(F32)
        cnt = jnp.sum(oh, axis=0, keepdims=True)
        run_chunks = jnp.floor((cnt + (SUBLANES - 1)) * (1.0 / SUBLANES))
        loff = SUBLANES * _dot(jnp.broadcast_to(run_chunks, (SUBLANES, LANES)).astype(BF16), before_e)[0:1]
        where_to = loff + _dot(earlier, oh.astype(BF16))
        lpos1 = jnp.sum(jnp.where(oh1, where_to, 0.0), axis=-1, keepdims=True)
        lpos2 = jnp.sum(jnp.where(oh2, where_to, 0.0), axis=-1, keepdims=True)
        lpos_t = jnp.where(lf == 0.0, lpos1, jnp.where(lf == 1.0, lpos2, 0.0)).T
        perm = jnp.logical_or(local_row == lpos_t[0:1].astype(jnp.int32),
                              local_row == lpos_t[1:2].astype(jnp.int32)).astype(BF16)
        xloc_ref[hh] = _pack_halves(_dot(perm, t_hi[sl]))
        cnt_ref[hh] = cnt
        route_parts.append(jnp.where(lf == 4.0, lpos1, jnp.where(lf == 5.0, lpos2, rt)))
    route_ref[...] = jnp.concatenate(route_parts, axis=0)


def _split_inputs(refs, source, geo):
    if source == "raw":
        xp_ref, xs_ref, meta_ref = refs[:3]
        by_position = lambda ref: (lambda: jnp.concatenate([ref[:, l, :] for l in range(ref.shape[1])], axis=0))
        meta = lambda: jnp.concatenate([jnp.broadcast_to(meta_ref[l:l + 1, :], (geo["m_batch"], meta_ref.shape[1]))
                                        for l in range(meta_ref.shape[0])], axis=0)
        return dict(meta=meta, prompt=by_position(xp_ref), sample=by_position(xs_ref)), refs[3:]
    if source == "pending":
        x_ref, yl_ref, rprev_ref = refs[:3]
        get, refs = (lambda: _moe_combine(x_ref, yl_ref, rprev_ref)), refs[3:]
    else:
        x_ref = refs[0]
        get, refs = (lambda: x_ref[...]), refs[1:]
    return dict(meta=get, prompt=get, sample=get), refs


def _mixer_kernel(*refs, cfg, geo, snapshot, source):
    get_x, refs = _split_inputs(refs, source, geo)
    h0s_ref, buf0s_ref = refs[:2]
    w_refs = refs[2:2 + N_MIXER_WEIGHTS]
    rest = refs[2 + N_MIXER_WEIGHTS:]
    x1_ref, route_ref, xloc_ref, cnt_ref, hp_ref, bufp_ref, hs_ref, bufs_ref = rest[:8]
    rest = rest[8:]
    if snapshot:
        hsnap_ref, bsnap_ref = rest[:2]
        rest = rest[2:]
    hm, bufm, u_m, y_m, cat_m, u_p, y_p, cat_p, u_s, y_s, cat_s = rest
    outs = (x1_ref, route_ref, xloc_ref, cnt_ref)
    s = pl.program_id(0)
    n_p = geo["n_p_steps"]
    batch = geo["batch"]

    @pl.when(s == 0)
    def _meta():
        _stream_step(get_x["meta"], w_refs, lambda: jnp.zeros(hm.shape, F32), lambda: jnp.zeros(bufm.shape, F32), True, 0,
                     *outs, hm, bufm, u_m, y_m, cat_m, bblk=geo["m_batch"], tile_l=geo["n_meta"], cfg=cfg,
                     precise=False)

    @pl.when(jnp.logical_and(s >= 1, s <= n_p))
    def _prompt():
        if snapshot:
            @pl.when(s == n_p)
            def _():
                hsnap_ref[...] = hp_ref[...]
                bsnap_ref[...] = bufp_ref[...]
        _stream_step(get_x["prompt"], w_refs, lambda: hm[:, 0:batch, :], lambda: bufm[:, 0:batch, :], s == 1,
                     geo["n_meta"] + (s - 1) * geo["p_tile_l"], *outs, hp_ref, bufp_ref, u_p, y_p, cat_p,
                     bblk=batch, tile_l=geo["p_tile_l"], cfg=cfg, precise=False)

    @pl.when(s > n_p)
    def _sample():
        _stream_step(get_x["sample"], w_refs, lambda: h0s_ref[...], lambda: buf0s_ref[...], True, PAST_LEN,
                     *outs, hs_ref, bufs_ref, u_s, y_s, cat_s, bblk=geo["s_bblk"], tile_l=geo["dec_seq"], cfg=cfg,
                     precise=False)


def _mixer_weights(lw):
    return [lw["g_mix"], lw["w_in"], lw["mq"], lw["pmat"], lw["a8"], lw["ssm_d"], lw["w_glu"], lw["b_glu"],
            lw["w_pool"], lw["pool_scale"], lw["w_out"], lw["g_ffn"], lw["w_r"], lw["b_r"]]


def _stream_scratch(tile_l, bblk, d_model, d_ssm):
    return [pltpu.VMEM((tile_l // CHUNK, CHUNK, bblk, d_model), F32),
            pltpu.VMEM((tile_l // CHUNK, CHUNK, bblk, d_ssm), F32),
            pltpu.VMEM(((POOL_HIST + tile_l) * bblk, d_model - d_ssm), F32)]


def _input_specs(x_in, pending, geo, d_model, blk, pblk, sblk):
    g = geo
    rows = g["step_rows"]
    if isinstance(x_in, tuple):
        specs = [pl.BlockSpec((g["batch"], g["p_tile_l"], d_model), lambda s: (0, pblk(s), 0)),
                 pl.BlockSpec((g["s_bblk"], g["dec_seq"], d_model), lambda s: (sblk(s), 0, 0)),
                 pl.BlockSpec(x_in[2].shape, lambda s: (0, 0))]
        return specs, list(x_in), "raw"
    specs = [pl.BlockSpec((rows, d_model), lambda s: (blk(s), 0))]
    if pending is None:
        return specs, [x_in], "flat"
    specs += [pl.BlockSpec((rows // MOE_TILE, g["n_local"], d_model // 2), lambda s: (blk(s), 0, 0)),
              pl.BlockSpec((rows, LANES), lambda s: (blk(s), 0))]
    return specs, [x_in, *pending], "pending"


def _layer_spec(a, layer):
    nd = a.ndim
    return pl.BlockSpec((None,) + a.shape[1:], lambda s, _n=nd: (layer,) + (0,) * (_n - 1))


def _mixer_call(x_in, pending, h0s, buf0s, lw, *, geo, cfg, snapshot, layer):
    d_model = lw["w_in"].shape[1]
    total = (geo["n_p_steps"] + geo["n_s_steps"] + 1) * geo["step_rows"]
    groups, dec_batch, _ = h0s.shape
    d_pool = buf0s.shape[-1]
    d_ssm = d_model - d_pool
    g = geo
    n_p, n_s = g["n_p_steps"], g["n_s_steps"]
    rows = g["step_rows"]
    per_step = rows // MOE_TILE
    n_local = g["n_local"]
    blk = lambda s: jnp.where(s == 0, n_p + n_s, s - 1)
    sblk = lambda s: jnp.clip(s - (n_p + 1), 0, n_s - 1)
    weights = _mixer_weights(lw)
    h_s_spec = pl.BlockSpec((groups, g["s_bblk"], LANES), lambda s: (0, sblk(s), 0))
    buf_s_spec = pl.BlockSpec((POOL_HIST, g["s_bblk"], d_pool), lambda s: (0, sblk(s), 0))
    x_specs, x_args, source = _input_specs(x_in, pending, geo, d_model, blk,
                                           lambda s: jnp.clip(s - 1, 0, n_p - 1), sblk)
    in_specs = x_specs + [h_s_spec, buf_s_spec] + [_layer_spec(w, layer) for w in weights]
    h_p_shape, buf_p_shape = (groups, g["batch"], LANES), (POOL_HIST, g["batch"], d_pool)
    out_specs = [pl.BlockSpec((rows, d_model), lambda s: (blk(s), 0)),
                 pl.BlockSpec((rows, LANES), lambda s: (blk(s), 0)),
                 pl.BlockSpec((per_step, n_local, d_model // 2), lambda s: (blk(s), 0, 0)),
                 pl.BlockSpec((per_step, 1, LANES), lambda s: (blk(s), 0, 0)),
                 pl.BlockSpec(h_p_shape, lambda s: (0, 0, 0)), pl.BlockSpec(buf_p_shape, lambda s: (0, 0, 0)),
                 h_s_spec, buf_s_spec]
    out_shape = [jax.ShapeDtypeStruct((total, d_model), F32), jax.ShapeDtypeStruct((total, LANES), F32),
                 jax.ShapeDtypeStruct((total // MOE_TILE, n_local, d_model // 2), jnp.uint32),
                 jax.ShapeDtypeStruct((total // MOE_TILE, 1, LANES), F32),
                 jax.ShapeDtypeStruct(h_p_shape, F32), jax.ShapeDtypeStruct(buf_p_shape, F32),
                 jax.ShapeDtypeStruct((groups, dec_batch, LANES), F32),
                 jax.ShapeDtypeStruct((POOL_HIST, dec_batch, d_pool), F32)]
    if snapshot:
        out_specs += [pl.BlockSpec(h_p_shape, lambda s: (0, 0, 0)), pl.BlockSpec(buf_p_shape, lambda s: (0, 0, 0))]
        out_shape += [jax.ShapeDtypeStruct(h_p_shape, F32), jax.ShapeDtypeStruct(buf_p_shape, F32)]
    scratch = ([pltpu.VMEM((groups, g["m_batch"], LANES), F32), pltpu.VMEM((POOL_HIST, g["m_batch"], d_pool), F32)]
               + _stream_scratch(g["n_meta"], g["m_batch"], d_model, d_ssm)
               + _stream_scratch(g["p_tile_l"], g["batch"], d_model, d_ssm)
               + _stream_scratch(g["dec_seq"], g["s_bblk"], d_model, d_ssm))
    return pl.pallas_call(
        functools.partial(_mixer_kernel, cfg=cfg, geo=geo, snapshot=snapshot, source=source),
        grid=(1 + n_p + n_s,),
        in_specs=in_specs,
        out_specs=out_specs,
        out_shape=out_shape,
        scratch_shapes=scratch,
        compiler_params=pltpu.CompilerParams(dimension_semantics=("arbitrary",), vmem_limit_bytes=VMEM_LIMIT),
        name="mixer",
    )(*x_args, h0s, buf0s, *weights)


def _mixer_tail_kernel(*refs, cfg, geo, source):
    get_x, refs = _split_inputs(refs, source, geo)
    get_x = get_x["prompt"]
    h0_ref, buf0_ref = refs[:2]
    w_refs = refs[2:2 + N_MIXER_WEIGHTS]
    rest = refs[2 + N_MIXER_WEIGHTS + 4:]
    x1_ref, route_ref, xloc_ref, cnt_ref, h_ref, buf_ref, u_scr, y_scr, cat_scr = rest
    pos = geo["n_meta"] + (geo["n_p_steps"] - 1) * geo["p_tile_l"]
    _stream_step(get_x, w_refs, lambda: h0_ref[...], lambda: buf0_ref[...], True, pos, x1_ref, route_ref, xloc_ref,
                 cnt_ref, h_ref, buf_ref, u_scr, y_scr, cat_scr, bblk=geo["batch"], tile_l=geo["p_tile_l"], cfg=cfg,
                 precise=True)


def _mixer_tail_call(x_in, pending, h0, buf0, lw32, x1, route, xloc, cnt, *, geo, cfg, layer):
    d_model = x1.shape[-1]
    d_pool = buf0.shape[-1]
    g = geo
    rows = g["step_rows"]
    per_step = rows // MOE_TILE
    last = g["n_p_steps"] - 1
    weights = _mixer_weights(lw32)

    def full(a):
        nd = a.ndim
        return pl.BlockSpec(a.shape, lambda s, _n=nd: (0,) * _n)

    x_specs, x_args, source = _input_specs(x_in, pending, geo, d_model, lambda s: last, lambda s: last, lambda s: 0)
    n_in = len(x_args) + 2 + len(weights)
    return pl.pallas_call(
        functools.partial(_mixer_tail_kernel, cfg=cfg, geo=geo, source=source),
        grid=(1,),
        in_specs=(x_specs + [full(h0), full(buf0)] + [_layer_spec(w, layer) for w in weights]
                  + [pl.BlockSpec(memory_space=pl.ANY)] * 4),
        out_specs=[pl.BlockSpec((rows, d_model), lambda s: (last, 0)),
                   pl.BlockSpec((rows, LANES), lambda s: (last, 0)),
                   pl.BlockSpec((per_step, g["n_local"], d_model // 2), lambda s: (last, 0, 0)),
                   pl.BlockSpec((per_step, 1, LANES), lambda s: (last, 0, 0)),
                   full(h0), full(buf0)],
        out_shape=[jax.ShapeDtypeStruct(a.shape, a.dtype) for a in (x1, route, xloc, cnt, h0, buf0)],
        scratch_shapes=_stream_scratch(g["p_tile_l"], g["batch"], d_model, d_model - d_pool),
        input_output_aliases={n_in + k: k for k in range(4)},
        compiler_params=pltpu.CompilerParams(dimension_semantics=("arbitrary",), vmem_limit_bytes=VMEM_LIMIT),
        name="mixer_tail",
    )(*x_args, h0, buf0, *weights, x1, route, xloc, cnt)


def _experts_kernel(exp_ref, nact_ref, src_ref, nval_ref, xl_in_ref, wg_ref, wu_ref, wd_ref, rows_ref,
                    xbuf, ybuf, wgu_scr, wd_scr, sem_in, sem_out):
    del xl_in_ref
    j = pl.program_id(0)
    f = wg_ref.shape[-1]
    chunks_per_tile = EXPERT_TILE // SUBLANES
    nact = nact_ref[0]

    def chunk_copy(tile, k, scatter):
        slot = tile % 2
        hbm = rows_ref.at[pl.ds(pl.multiple_of(src_ref[tile * chunks_per_tile + k], SUBLANES), SUBLANES), :]
        local = pl.ds(k * SUBLANES, SUBLANES) if isinstance(k, int) else pl.ds(
            pl.multiple_of(k * SUBLANES, SUBLANES), SUBLANES)
        if scatter:
            return pltpu.make_async_copy(ybuf.at[slot, local, :], hbm, sem_out.at[slot])
        return pltpu.make_async_copy(hbm, xbuf.at[slot, local, :], sem_in.at[slot])

    def for_tile_chunks(tile, op, scatter):
        def body(k, carry):
            op(chunk_copy(tile, k, scatter))
            return carry

        lax.fori_loop(0, nval_ref[tile], body, 0)

    def start_tile_chunks(tile, scatter):
        full = nval_ref[tile] == chunks_per_tile

        @pl.when(full)
        def _():
            for k in range(chunks_per_tile):
                chunk_copy(tile, k, scatter).start()

        @pl.when(jnp.logical_not(full))
        def _():
            for_tile_chunks(tile, start, scatter)

    def wait_tile_chunks(tile, scatter):
        slot = tile % 2
        full = nval_ref[tile] == chunks_per_tile

        @pl.when(full)
        def _():
            whole = rows_ref.at[pl.ds(0, EXPERT_TILE), :]
            if scatter:
                pltpu.make_async_copy(ybuf.at[slot], whole, sem_out.at[slot]).wait()
            else:
                pltpu.make_async_copy(whole, xbuf.at[slot], sem_in.at[slot]).wait()

        @pl.when(jnp.logical_not(full))
        def _():
            for_tile_chunks(tile, wait, scatter)

    start = lambda cp: cp.start()
    wait = lambda cp: cp.wait()

    @pl.when(j == 0)
    def _():
        xbuf[...] = jnp.zeros_like(xbuf)
        start_tile_chunks(0, False)

    @pl.when(j + 1 < nact)
    def _():
        start_tile_chunks(j + 1, False)

    @pl.when(jnp.logical_and(j >= 2, j - 2 < nact))
    def _():
        wait_tile_chunks(j - 2, True)

    active = j < nact
    changed = jnp.logical_or(j == 0, exp_ref[j] != exp_ref[jnp.maximum(j - 1, 0)])

    @pl.when(jnp.logical_and(active, changed))
    def _():
        wgu_scr[:, 0:f] = wg_ref[...].astype(BF16)
        wgu_scr[:, f:2 * f] = wu_ref[...].astype(BF16)
        wd_scr[...] = wd_ref[...].astype(BF16)

    @pl.when(active)
    def _():
        wait_tile_chunks(j, False)
        half = wgu_scr.shape[0] // 2
        t_lo, t_hi = _unpack_halves(xbuf[j % 2])
        gu = _dot(t_lo, wgu_scr[0:half, :]) + _dot(t_hi, wgu_scr[half:2 * half, :])
        act = jax.nn.silu(gu[:, 0:f]) * gu[:, f:2 * f]
        ybuf[j % 2] = _pack_halves(_dot(act.astype(BF16), wd_scr[...]))
        start_tile_chunks(j, True)


def _experts_call(tile_exp, n_active, src, nval, xloc, w_gate, w_up, w_down, layer, n_tiles):
    _, n_exp, d_model, f = w_gate.shape
    xl_flat = xloc.reshape(-1, xloc.shape[-1])
    wspec = lambda a, b: pl.BlockSpec((None, None, a, b), lambda i, ex, *_: (layer, ex[i], 0, 0))
    buf = pltpu.VMEM((2, EXPERT_TILE, d_model // 2), jnp.uint32)
    out = pl.pallas_call(
        _experts_kernel,
        grid_spec=pltpu.PrefetchScalarGridSpec(
            num_scalar_prefetch=4,
            grid=(n_tiles + 2,),
            in_specs=[pl.BlockSpec(memory_space=pl.ANY), wspec(d_model, f), wspec(d_model, f), wspec(f, d_model)],
            out_specs=pl.BlockSpec(memory_space=pl.ANY),
            scratch_shapes=[buf, buf, pltpu.VMEM((d_model, 2 * f), BF16), pltpu.VMEM((f, d_model), BF16),
                            pltpu.SemaphoreType.DMA((2,)), pltpu.SemaphoreType.DMA((2,))],
        ),
        out_shape=jax.ShapeDtypeStruct(xl_flat.shape, xl_flat.dtype),
        input_output_aliases={4: 0},
        compiler_params=pltpu.CompilerParams(dimension_semantics=("arbitrary",), vmem_limit_bytes=VMEM_LIMIT),
        name="moe_experts",
    )(tile_exp, n_active, src, nval, xl_flat, w_gate, w_up, w_down)
    return out.reshape(xloc.shape)


def _moe_experts(xloc, cnt, w_gate, w_up, w_down, *, layer):
    n_exp = w_gate.shape[1]
    nt, n_local, _ = xloc.shape
    total = nt * MOE_TILE
    cpt = EXPERT_TILE // SUBLANES
    n_chunks = n_local // SUBLANES
    counts = cnt[:, 0, :n_exp].astype(jnp.int32)
    run = (counts + SUBLANES - 1) // SUBLANES
    lend = jnp.cumsum(run, axis=1)
    loff = lend - run
    gend = jnp.cumsum(run, axis=0)
    gcum = gend - run
    seg = gend[-1]
    seg_tiles = (seg + cpt - 1) // cpt
    seg_end = jnp.cumsum(seg_tiles)
    seg_off = seg_end - seg_tiles
    n_tiles = (2 * total + nt * n_exp * (SUBLANES - 1) + n_exp * (EXPERT_TILE - 1)) // EXPERT_TILE + 1
    n_steps = n_tiles + 2
    n_active = seg_end[-1].reshape(1).astype(jnp.int32)
    tile_idx = jnp.minimum(jnp.arange(n_steps, dtype=jnp.int32), n_active - 1)
    tile_exp = jnp.minimum(jnp.sum((tile_idx[:, None] >= seg_end[None, :]).astype(jnp.int32), axis=1), n_exp - 1)
    is_e = tile_exp[:, None] == jnp.arange(n_exp, dtype=jnp.int32)[None, :]
    pick = lambda table: jnp.sum(jnp.where(is_e[:, None, :], table[None, :, :], 0), axis=2)
    pick1 = lambda vec: jnp.sum(jnp.where(is_e, vec[None, :], 0), axis=1)
    g = ((jnp.arange(n_steps, dtype=jnp.int32) - pick1(seg_off)) * cpt)[:, None] + jnp.arange(cpt, dtype=jnp.int32)
    gend_t, gcum_t, loff_t = pick(gend), pick(gcum), pick(loff)
    owner = jnp.logical_and(g[:, :, None] >= gcum_t[:, None, :], g[:, :, None] < gend_t[:, None, :])
    tile_base = jnp.arange(nt, dtype=jnp.int32) * n_chunks
    src = jnp.sum(jnp.where(owner, (tile_base[None, :] + loff_t - gcum_t)[:, None, :] + g[:, :, None], 0), axis=2)
    src = (src * SUBLANES).reshape(-1)
    nval = jnp.clip(pick1(seg) - g[:, 0], 0, cpt)
    return _experts_call(tile_exp, n_active, src, nval, xloc, w_gate, w_up, w_down, layer, n_tiles)


def _finish_kernel(x1_ref, yl_ref, route_ref, gfin_ref, yp_ref, ys_ref, *, n_p_steps):
    i = pl.program_id(0)
    x = _moe_combine(x1_ref, yl_ref, route_ref)
    ms = jnp.mean(x * x, axis=-1, keepdims=True)
    y = x * lax.rsqrt(ms + EPS) * gfin_ref[...]
    pb, pl_ = yp_ref.shape[0], yp_ref.shape[1]
    sb, sl = ys_ref.shape[0], ys_ref.shape[1]

    @pl.when(i < n_p_steps)
    def _():
        for l in range(pl_):
            yp_ref[:, l, :] = y[l * pb:(l + 1) * pb, :]

    @pl.when(i >= n_p_steps)
    def _():
        for l in range(sl):
            ys_ref[:, l, :] = y[l * sb:(l + 1) * sb, :]


def _finish_call(x1, yl, route, g_final, *, geo):
    d_model = x1.shape[-1]
    g = geo
    rows = g["step_rows"]
    n_p, n_s = g["n_p_steps"], g["n_s_steps"]
    return pl.pallas_call(
        functools.partial(_finish_kernel, n_p_steps=n_p),
        grid=(n_p + n_s,),
        in_specs=[pl.BlockSpec((rows, d_model), lambda i: (i, 0)),
                  pl.BlockSpec((rows // MOE_TILE, g["n_local"], d_model // 2), lambda i: (i, 0, 0)),
                  pl.BlockSpec((rows, LANES), lambda i: (i, 0)),
                  pl.BlockSpec((1, d_model), lambda i: (0, 0))],
        out_specs=[pl.BlockSpec((g["batch"], g["p_tile_l"], d_model), lambda i: (0, jnp.minimum(i, n_p - 1), 0)),
                   pl.BlockSpec((g["s_bblk"], g["dec_seq"], d_model), lambda i: (jnp.clip(i - n_p, 0, n_s - 1), 0, 0))],
        out_shape=[jax.ShapeDtypeStruct((g["batch"], g["seq"], d_model), F32),
                   jax.ShapeDtypeStruct((g["dec_batch"], g["dec_seq"], d_model), F32)],
        compiler_params=pltpu.CompilerParams(dimension_semantics=("arbitrary",), vmem_limit_bytes=VMEM_LIMIT),
        name="finish",
    )(x1, yl, route, g_final)


def kernel(x_prompt, x_sample, state_ssm_re, state_ssm_im, state_pool, meta_tokens, norm_mix_g, w_in, ssm_a_re,
           ssm_a_im, ssm_log_dt, ssm_b_re, ssm_b_im, ssm_c_re, ssm_c_im, ssm_d, w_glu, b_glu, w_pool, pool_scale,
           w_out, norm_ffn_g, w_router_group, b_router_group, w_router_expert, b_router_expert, w_gate, w_up,
           w_down, norm_final_g):
    batch, seq, d_model = x_prompt.shape
    dec_batch, dec_seq, _ = x_sample.shape
    depth, _, groups, nstate = state_ssm_re.shape
    n_meta = meta_tokens.shape[0]
    pool_buf, d_pool = state_pool.shape[2], state_pool.shape[3]
    n_pool_groups = w_pool.shape[1]
    windows = tuple(2 ** (k + 1) for k in range(n_pool_groups))
    assert pool_buf == max(windows) - 1 == POOL_HIST - 1
    n_expert_groups = w_router_group.shape[-1]
    n_experts = w_router_expert.shape[-1]
    cfg = dict(windows=windows, n_expert_groups=n_expert_groups, experts_per_group=n_experts // n_expert_groups)

    step_rows = 512
    p_tile_l, s_bblk, m_batch = step_rows // batch, step_rows // dec_seq, step_rows // n_meta
    n_p, n_s, n_m = batch * seq, dec_batch * dec_seq, m_batch * n_meta
    assert m_batch >= batch and seq % p_tile_l == 0 and dec_batch % s_bblk == 0
    total = n_p + n_s + n_m
    assert total % step_rows == 0 and step_rows % MOE_TILE == 0
    n_local = 2 * MOE_TILE + n_experts * (SUBLANES - 1)
    n_local = (n_local + LANES - 1) // LANES * LANES
    geo = dict(batch=batch, seq=seq, dec_batch=dec_batch, dec_seq=dec_seq, n_meta=n_meta, step_rows=step_rows,
               p_tile_l=p_tile_l, s_bblk=s_bblk, m_batch=m_batch, n_p_steps=seq // p_tile_l,
               n_s_steps=dec_batch // s_bblk, n_local=n_local)

    mq, pmat, a8 = _s5_prepare(ssm_a_re, ssm_a_im, ssm_log_dt, ssm_b_re, ssm_b_im, ssm_c_re, ssm_c_im)

    x = (x_prompt.astype(F32), x_sample.astype(F32), meta_tokens.astype(F32))

    row2 = lambda v: v.reshape(1, -1).astype(F32)
    outs = {k: [] for k in ("hp", "bp", "hs", "bs")}
    pending = None
    rows3 = lambda v: v.reshape(depth, 1, -1).astype(F32)
    w_r = jnp.concatenate([w_router_group, w_router_expert], axis=2).astype(F32)
    w_r = jnp.pad(w_r, ((0, 0), (0, 0), (0, LANES - w_r.shape[2])))
    b_r = jnp.concatenate([b_router_group, b_router_expert], axis=1).astype(F32)
    lw32 = dict(g_mix=rows3(norm_mix_g), w_in=w_in.astype(F32), mq=mq, pmat=pmat, a8=a8, ssm_d=rows3(ssm_d),
                w_glu=w_glu.astype(F32), b_glu=rows3(b_glu), w_pool=w_pool.astype(F32), pool_scale=rows3(pool_scale),
                w_out=w_out.astype(F32), g_ffn=rows3(norm_ffn_g), w_r=w_r,
                b_r=jnp.pad(b_r, ((0, 0), (0, LANES - b_r.shape[1])))[:, None, :])
    lw = dict(lw32, **{k: lw32[k].astype(BF16) for k in ("w_in", "mq", "pmat", "w_glu", "w_pool", "w_out")})
    for l in range(depth):
        h0_s = jnp.concatenate([state_ssm_re[l], state_ssm_im[l]], axis=-1).transpose(1, 0, 2).astype(F32)
        buf0_s = jnp.pad(state_pool[l].astype(F32).transpose(1, 0, 2), ((1, 0), (0, 0), (0, 0)))
        redo_last = l < depth - 1
        res = _mixer_call(x, pending, h0_s, buf0_s, lw, geo=geo, cfg=cfg, snapshot=redo_last, layer=l)
        x1, route, xloc, cnt, h_p, buf_p, h_s, buf_s = res[:8]
        if redo_last:
            x1, route, xloc, cnt, h_p, buf_p = _mixer_tail_call(x, pending, res[8], res[9], lw32, x1, route, xloc,
                                                                 cnt, geo=geo, cfg=cfg, layer=l)
        outs["hp"].append(h_p)
        outs["bp"].append(buf_p)
        outs["hs"].append(h_s)
        outs["bs"].append(buf_s)
        x, pending = x1, (_moe_experts(xloc, cnt, w_gate, w_up, w_down, layer=l), route)

    y_prompt, y_sample = _finish_call(x, pending[0], pending[1], row2(norm_final_g), geo=geo)
    st = lambda hs: jnp.stack(hs).transpose(0, 2, 1, 3)
    pl_out = lambda bs: jnp.stack(bs)[:, 1:].transpose(0, 2, 1, 3)
    hp, hs = st(outs["hp"]), st(outs["hs"])
    return (y_prompt, y_sample, hp[..., :nstate], hp[..., nstate:], pl_out(outs["bp"]),
            hs[..., :nstate], hs[..., nstate:], pl_out(outs["bs"]))
```

```python
import functools

import numpy as np
import jax
import jax.numpy as jnp
from jax import lax
from jax.experimental import pallas as pl
from jax.experimental.pallas import tpu as pltpu

F32 = jnp.float32
BF16 = jnp.bfloat16

LANES = 128
SUBLANES = 8
CHUNK = 8
POOL_HIST = 16
MOE_TILE = 256
EXPERT_TILE = 512
VMEM_LIMIT = 56 * 1024 * 1024
EPS = 1e-6
NEG = -1e30
PAST_LEN = 16384


def _dot(a, b):
    return jnp.dot(a, b, preferred_element_type=F32)


def _split_bf16(x):
    hi = x.astype(BF16)
    return hi, (x - hi.astype(F32)).astype(BF16)


def _dot3(a, b, dims=(((1,), (0,)), ((), ()))):
    a_hi, a_lo = _split_bf16(a)
    b_hi, b_lo = _split_bf16(b)
    dg = lambda x, y: lax.dot_general(x, y, dims, preferred_element_type=F32)
    return dg(a_hi, b_hi) + dg(a_lo, b_hi) + dg(a_hi, b_lo)


def _mm(a, w, precise):
    if precise:
        return _dot3(a, w)
    return _dot(a.astype(BF16), w)


def _pack_halves(x):
    k = x.shape[-1] // 2
    return pltpu.pack_elementwise([x[:, :k], x[:, k:]], packed_dtype=BF16)


def _unpack_halves(p):
    lo = pltpu.unpack_elementwise(p, index=0, packed_dtype=BF16, unpacked_dtype=F32)
    hi = pltpu.unpack_elementwise(p, index=1, packed_dtype=BF16, unpacked_dtype=F32)
    return lo.astype(BF16), hi.astype(BF16)


def _dot_nt_f32(a, b):
    return _dot3(a, b, (((1,), (1,)), ((), ())))


def _s5_prep_kernel(a_re_ref, a_im_ref, ldt_ref, bt_ref, btx_ref, c_ref, cx_ref, mq_ref, p_ref, a8_ref):
    ch = bt_ref.shape[1]
    first_half = lax.broadcasted_iota(jnp.int32, (1, LANES), 1) < LANES // 2
    sgn = jnp.where(first_half, 1.0, -1.0)
    for s in range(CHUNK):
        a_re = a_re_ref[s]
        a_im = a_im_ref[s]
        dt = jnp.exp(ldt_ref[s])
        mag = jnp.exp(dt * a_re)
        abr = mag * jnp.cos(dt * a_im)
        abi = mag * jnp.sin(dt * a_im)
        den = a_re * a_re + a_im * a_im
        nr = abr - 1.0
        f_re = (nr * a_re + abi * a_im) / den
        f_im = (abi * a_re - nr * a_im) / den
        bb = f_re * bt_ref[s] - sgn * f_im * btx_ref[s]
        cc = c_ref[s] * sgn
        ccx = -cx_ref[s]
        bbx = f_re * btx_ref[s] + sgn * f_im * bt_ref[s]
        pr = jnp.ones_like(a_re)
        pi = jnp.zeros_like(a_re)
        e2, w2 = [], []
        for k in range(CHUNK + 1):
            e2.append(pr * cc + pi * ccx)
            w2.append(pr * bb - sgn * pi * bbx)
            if k < CHUNK:
                pr, pi = pr * abr - pi * abi, pr * abi + pi * abr
        a8_ref[s, 0:1, :] = pr
        a8_ref[s, 1:2, :] = -sgn * pi
        a8_ref[s, 2:3, :] = sgn * pi
        zero = jnp.zeros_like(bb)
        jpos = [(p - s) % CHUNK for p in range(CHUNK)]
        for p in range(CHUNK):
            j = jpos[p]
            taps = jnp.concatenate([e2[jp - j] if jp >= j else zero for jp in jpos], axis=0)
            rows = slice(p * ch, (p + 1) * ch)
            mq_ref[s, rows, 0:LANES] = _dot_nt_f32(bb, taps)
            mq_ref[s, rows, LANES:2 * LANES] = w2[CHUNK - 1 - j]
        pt = jnp.concatenate([e2[jp + 1] for jp in jpos], axis=0)
        p_ref[s] = pt.T


def _s5_prepare(a_re, a_im, log_dt, b_re, b_im, c_re, c_im):
    depth, groups, nstate = a_re.shape
    ch = b_re.shape[-1]
    dg = depth * groups
    assert ch * CHUNK == LANES and 2 * nstate == LANES and groups % CHUNK == 0
    dup = lambda x: jnp.concatenate([x, x], axis=-1).reshape(dg, 1, LANES).astype(F32)
    btr = jnp.swapaxes(b_re.reshape(dg, nstate, ch), 1, 2).astype(F32)
    bti = jnp.swapaxes(b_im.reshape(dg, nstate, ch), 1, 2).astype(F32)
    cr = c_re.reshape(dg, ch, nstate).astype(F32)
    ci = c_im.reshape(dg, ch, nstate).astype(F32)
    pair = lambda x, y: jnp.concatenate([x, y], axis=-1)
    blk = lambda *shape: pl.BlockSpec((CHUNK,) + shape, lambda g: (g,) + (0,) * len(shape))
    mq, pmat, a8 = pl.pallas_call(
        _s5_prep_kernel,
        grid=(dg // CHUNK,),
        in_specs=[blk(1, LANES), blk(1, LANES), blk(1, 1)] + [blk(ch, LANES)] * 4,
        out_specs=[blk(LANES, 2 * LANES), blk(LANES, LANES), blk(3, LANES)],
        out_shape=[jax.ShapeDtypeStruct((dg, LANES, 2 * LANES), F32), jax.ShapeDtypeStruct((dg, LANES, LANES), F32),
                   jax.ShapeDtypeStruct((dg, 3, LANES), F32)],
        name="s5_prep",
    )(dup(a_re), dup(a_im), log_dt.reshape(dg, 1, 1).astype(F32), pair(btr, bti), pair(bti, btr),
      pair(cr, ci), pair(ci, cr))
    shp = lambda x: x.reshape((depth, groups) + x.shape[1:])
    return shp(mq), shp(pmat), shp(a8)


N_MIXER_WEIGHTS = 14


def _moe_combine(x1_ref, yl_ref, route_ref):
    rows = x1_ref.shape[0]
    n_local = yl_ref.shape[1]
    lio = lax.broadcasted_iota(jnp.int32, (MOE_TILE, n_local), 1)
    parts = []
    for hh in range(rows // MOE_TILE):
        r = route_ref[hh * MOE_TILE:(hh + 1) * MOE_TILE, :]
        wperm = jnp.where(lio == r[:, 4:5].astype(jnp.int32), r[:, 2:3],
                          jnp.where(lio == r[:, 5:6].astype(jnp.int32), r[:, 3:4], 0.0)).astype(BF16)
        y_lo, y_hi = _unpack_halves(yl_ref[hh])
        parts.append(jnp.concatenate([_dot(wperm, y_lo), _dot(wperm, y_hi)], axis=-1))
    return x1_ref[...] + jnp.concatenate(parts, axis=0)


def _stream_step(get_x, w_refs, h_init, buf_init, first, pos_base, x1_ref, route_ref, xloc_ref, cnt_ref,
                 h_ref, buf_ref, u_scr, y_scr, cat_scr, *, bblk, tile_l, cfg, precise):
    (gmix_ref, win_ref, mq_ref, p_ref, a8_ref, d_ref, wglu_ref, bglu_ref, wpool_ref, pscale_ref, wout_ref,
     gffn_ref, wr_ref, br_ref) = w_refs
    windows = cfg["windows"]
    n_expert_groups = cfg["n_expert_groups"]
    experts_per_group = cfg["experts_per_group"]
    rows = tile_l * bblk
    nk = tile_l // CHUNK
    n = nk * bblk
    d_model = x1_ref.shape[-1]
    d_ssm = y_scr.shape[-1]
    d_pool = d_model - d_ssm
    hist = POOL_HIST * bblk

    def _init():
        h_ref[...] = h_init()
        cat_scr[0:hist, :] = buf_init().reshape(hist, d_pool)

    if first is True:
        _init()
    else:
        pl.when(first)(_init)

    x = get_x()
    ms = jnp.mean(x * x, axis=-1, keepdims=True)
    h = x * lax.rsqrt(ms + EPS) * gmix_ref[...]
    u = _mm(h, win_ref[...], precise)
    u_scr[...] = u.reshape(nk, CHUNK, bblk, d_model)

    slot = lax.broadcasted_iota(jnp.int32, (n, LANES), 1) // (LANES // CHUNK)
    for v in range(d_ssm // LANES):
        rolled = []
        for j in range(CHUNK):
            uj = u_scr[:, j, :, v * LANES:(v + 1) * LANES].reshape(n, LANES)
            rolled.append(pltpu.roll(uj, (LANES // CHUNK) * j, axis=1) if j else uj)
        ys = []
        for s in range(CHUNK):
            g = v * CHUNK + s
            ug = rolled[0]
            for j in range(1, CHUNK):
                ug = jnp.where(slot == (s + j) % CHUNK, rolled[j], ug)
            yq = _mm(ug, mq_ref[g], precise)
            sg = yq[:, LANES:]
            sg_sw = pltpu.roll(sg, LANES // 2, axis=1)
            a8c = a8_ref[g, 0:1, :]
            a8s = a8_ref[g, 1:2, :]
            a8w = a8_ref[g, 2:3, :]
            hcur = h_ref[g]
            hsw = pltpu.roll(hcur, LANES // 2, axis=1)
            hins = []
            for k in range(nk):
                hins.append(hcur)
                sk = sg[k * bblk:(k + 1) * bblk]
                skw = sg_sw[k * bblk:(k + 1) * bblk]
                hcur, hsw = a8c * hcur + a8s * hsw + sk, a8c * hsw + a8w * hcur + skw
            h_ref[g] = hcur
            hin = jnp.concatenate(hins, axis=0) if nk > 1 else hins[0]
            ys.append(yq[:, :LANES] + _mm(hin, p_ref[g], precise))
        for j in range(CHUNK):
            z = ys[0]
            for s in range(1, CHUNK):
                z = jnp.where(slot == (s + j) % CHUNK, ys[s], z)
            if j:
                z = pltpu.roll(z, LANES - (LANES // CHUNK) * j, axis=1)
            y_scr[:, j, :, v * LANES:(v + 1) * LANES] = z.reshape(nk, bblk, LANES)

    u_ssm = u_scr[:, :, :, 0:d_ssm].reshape(rows, d_ssm)
    y = y_scr[...].reshape(rows, d_ssm) + d_ref[...] * u_ssm
    gl = jax.nn.gelu(y)
    ssm_out = gl * jax.nn.sigmoid(_mm(gl, wglu_ref[...], precise) + bglu_ref[...])

    up = u_scr[:, :, :, d_ssm:].reshape(rows, d_pool)
    cat_scr[hist:hist + rows, :] = up
    row_pos = lax.broadcasted_iota(jnp.int32, (rows, 1), 0) // bblk
    pos = pos_base + row_pos
    pool_parts = []
    pgrp = d_pool // len(windows)
    for kk, w in enumerate(windows):
        lo = kk * pgrp
        acc = cat_scr[hist:hist + rows, lo:lo + pgrp]
        cur = acc
        for sft in range(1, w):
            acc = acc + cat_scr[hist - sft * bblk:hist - sft * bblk + rows, lo:lo + pgrp]
        inv = 1.0 / jnp.minimum(pos + 1, w).astype(F32)
        diff = acc * inv - cur
        pool_parts.append(_mm(diff, wpool_ref[kk], precise))
    pool_out = jnp.concatenate(pool_parts, axis=-1) * pscale_ref[...]
    tail = cat_scr[rows:rows + hist, :]
    cat_scr[0:hist, :] = tail
    buf_ref[...] = tail.reshape(POOL_HIST, bblk, d_pool)

    if precise:
        mix = jnp.concatenate([ssm_out, pool_out], axis=-1)
    else:
        mix = jnp.concatenate([ssm_out.astype(BF16), pool_out.astype(BF16)], axis=-1)
    x1 = x + _mm(mix, wout_ref[...], precise)
    x1_ref[...] = x1
    ms2 = jnp.mean(x1 * x1, axis=-1, keepdims=True)
    t = x1 * lax.rsqrt(ms2 + EPS) * gffn_ref[...]

    t_hi, t_lo = _split_bf16(t)
    w_hi, w_lo = _split_bf16(wr_ref[...])
    hi_both = _dot(t_hi, jnp.concatenate([w_hi, w_lo], axis=-1))
    logits = hi_both[:, :LANES] + hi_both[:, LANES:] + _dot(t_lo, w_hi) + br_ref[...]
    lane = lax.broadcasted_iota(jnp.int32, (rows, LANES), 1).astype(F32)
    big = float(4 * LANES)
    lg = jnp.where(lane < n_expert_groups, logits, NEG)
    mg = jnp.max(lg, axis=-1, keepdims=True)
    gsel = jnp.min(jnp.where(lg == mg, lane, big), axis=-1, keepdims=True)
    p_sel = 1.0 / jnp.sum(jnp.exp(lg - mg), axis=-1, keepdims=True)
    e_lo = n_expert_groups + gsel * experts_per_group
    le = jnp.where((lane >= e_lo) & (lane < e_lo + experts_per_group), logits, NEG)
    v1 = jnp.max(le, axis=-1, keepdims=True)
    i1 = jnp.min(jnp.where(le == v1, lane, big), axis=-1, keepdims=True)
    le2 = jnp.where(lane == i1, NEG, le)
    v2 = jnp.max(le2, axis=-1, keepdims=True)
    i2 = jnp.min(jnp.where(le2 == v2, lane, big), axis=-1, keepdims=True)
    ex = jnp.exp(v2 - v1)
    q1 = p_sel / (1.0 + ex)
    q2 = q1 * ex
    e1 = i1 - n_expert_groups
    e2 = i2 - n_expert_groups
    route = jnp.where(lane == 0.0, e1, jnp.where(lane == 1.0, e2, jnp.where(lane == 2.0, q1,
                      jnp.where(lane == 3.0, q2, 0.0))))

    lf = lax.broadcasted_iota(jnp.int32, (MOE_TILE, LANES), 1).astype(F32)
    n_local = xloc_ref.shape[1]
    before_e = (lax.broadcasted_iota(jnp.int32, (LANES, LANES), 0)
                < lax.broadcasted_iota(jnp.int32, (LANES, LANES), 1)).astype(BF16)
    earlier = (lax.broadcasted_iota(jnp.int32, (MOE_TILE, MOE_TILE), 0)
               > lax.broadcasted_iota(jnp.int32, (MOE_TILE, MOE_TILE), 1)).astype(BF16)
    local_row = lax.broadcasted_iota(jnp.int32, (n_local, MOE_TILE), 0)
    route_parts = []
    for hh in range(rows // MOE_TILE):
        sl = slice(hh * MOE_TILE, (hh + 1) * MOE_TILE)
        rt = route[sl]
        oh1 = lf == rt[:, 0:1]
        oh2 = lf == rt[:, 1:2]
        oh = jnp.logical_or(oh1, oh2).astype(F32)
        cnt = jnp.sum(oh, axis=0, keepdims=True)
        run_chunks = jnp.floor((cnt + (SUBLANES - 1)) * (1.0 / SUBLANES))
        loff = SUBLANES * _dot(jnp.broadcast_to(run_chunks, (SUBLANES, LANES)).astype(BF16), before_e)[0:1]
        where_to = loff + _dot(earlier, oh.astype(BF16))
        lpos1 = jnp.sum(jnp.where(oh1, where_to, 0.0), axis=-1, keepdims=True)
        lpos2 = jnp.sum(jnp.where(oh2, where_to, 0.0), axis=-1, keepdims=True)
        lpos_t = jnp.where(lf == 0.0, lpos1, jnp.where(lf == 1.0, lpos2, 0.0)).T
        perm = jnp.logical_or(local_row == lpos_t[0:1].astype(jnp.int32),
                              local_row == lpos_t[1:2].astype(jnp.int32)).astype(BF16)
        xloc_ref[hh] = _pack_halves(_dot(perm, t_hi[sl]))
        cnt_ref[hh] = cnt
        route_parts.append(jnp.where(lf == 4.0, lpos1, jnp.where(lf == 5.0, lpos2, rt)))
    route_ref[...] = jnp.concatenate(route_parts, axis=0)


def _split_inputs(refs, source, geo):
    if source == "raw":
        xp_ref, xs_ref, meta_ref = refs[:3]
        by_position = lambda ref: (lambda: jnp.concatenate([ref[:, l, :] for l in range(ref.shape[1])], axis=0))
        meta = lambda: jnp.concatenate([jnp.broadcast_to(meta_ref[l:l + 1, :], (geo["m_batch"], meta_ref.shape[1]))
                                        for l in range(meta_ref.shape[0])], axis=0)
        return dict(meta=meta, prompt=by_position(xp_ref), sample=by_position(xs_ref)), refs[3:]
    if source == "pending":
        x_ref, yl_ref, rprev_ref = refs[:3]
        get, refs = (lambda: _moe_combine(x_ref, yl_ref, rprev_ref)), refs[3:]
    else:
        x_ref = refs[0]
        get, refs = (lambda: x_ref[...]), refs[1:]
    return dict(meta=get, prompt=get, sample=get), refs


def _mixer_kernel(*refs, cfg, geo, snapshot, source):
    get_x, refs = _split_inputs(refs, source, geo)
    h0s_ref, buf0s_ref = refs[:2]
    w_refs = refs[2:2 + N_MIXER_WEIGHTS]
    rest = refs[2 + N_MIXER_WEIGHTS:]
    x1_ref, route_ref, xloc_ref, cnt_ref, hp_ref, bufp_ref, hs_ref, bufs_ref = rest[:8]
    rest = rest[8:]
    if snapshot:
        hsnap_ref, bsnap_ref = rest[:2]
        rest = rest[2:]
    hm, bufm, u_m, y_m, cat_m, u_p, y_p, cat_p, u_s, y_s, cat_s = rest
    outs = (x1_ref, route_ref, xloc_ref, cnt_ref)
    s = pl.program_id(0)
    n_p = geo["n_p_steps"]
    batch = geo["batch"]

    @pl.when(s == 0)
    def _meta():
        _stream_step(get_x["meta"], w_refs, lambda: jnp.zeros(hm.shape, F32), lambda: jnp.zeros(bufm.shape, F32), True, 0,
                     *outs, hm, bufm, u_m, y_m, cat_m, bblk=geo["m_batch"], tile_l=geo["n_meta"], cfg=cfg,
                     precise=False)

    @pl.when(jnp.logical_and(s >= 1, s <= n_p))
    def _prompt():
        if snapshot:
            @pl.when(s == n_p)
            def _():
                hsnap_ref[...] = hp_ref[...]
                bsnap_ref[...] = bufp_ref[...]
        _stream_step(get_x["prompt"], w_refs, lambda: hm[:, 0:batch, :], lambda: bufm[:, 0:batch, :], s == 1,
                     geo["n_meta"] + (s - 1) * geo["p_tile_l"], *outs, hp_ref, bufp_ref, u_p, y_p, cat_p,
                     bblk=batch, tile_l=geo["p_tile_l"], cfg=cfg, precise=False)

    @pl.when(s > n_p)
    def _sample():
        _stream_step(get_x["sample"], w_refs, lambda: h0s_ref[...], lambda: buf0s_ref[...], True, PAST_LEN,
                     *outs, hs_ref, bufs_ref, u_s, y_s, cat_s, bblk=geo["s_bblk"], tile_l=geo["dec_seq"], cfg=cfg,
                     precise=False)


def _mixer_weights(lw):
    return [lw["g_mix"], lw["w_in"], lw["mq"], lw["pmat"], lw["a8"], lw["ssm_d"], lw["w_glu"], lw["b_glu"],
            lw["w_pool"], lw["pool_scale"], lw["w_out"], lw["g_ffn"], lw["w_r"], lw["b_r"]]


def _stream_scratch(tile_l, bblk, d_model, d_ssm):
    return [pltpu.VMEM((tile_l // CHUNK, CHUNK, bblk, d_model), F32),
            pltpu.VMEM((tile_l // CHUNK, CHUNK, bblk, d_ssm), F32),
            pltpu.VMEM(((POOL_HIST + tile_l) * bblk, d_model - d_ssm), F32)]


def _input_specs(x_in, pending, geo, d_model, blk, pblk, sblk):
    g = geo
    rows = g["step_rows"]
    if isinstance(x_in, tuple):
        specs = [pl.BlockSpec((g["batch"], g["p_tile_l"], d_model), lambda s: (0, pblk(s), 0)),
                 pl.BlockSpec((g["s_bblk"], g["dec_seq"], d_model), lambda s: (sblk(s), 0, 0)),
                 pl.BlockSpec(x_in[2].shape, lambda s: (0, 0))]
        return specs, list(x_in), "raw"
    specs = [pl.BlockSpec((rows, d_model), lambda s: (blk(s), 0))]
    if pending is None:
        return specs, [x_in], "flat"
    specs += [pl.BlockSpec((rows // MOE_TILE, g["n_local"], d_model // 2), lambda s: (blk(s), 0, 0)),
              pl.BlockSpec((rows, LANES), lambda s: (blk(s), 0))]
    return specs, [x_in, *pending], "pending"


def _layer_spec(a, layer):
    nd = a.ndim
    return pl.BlockSpec((None,) + a.shape[1:], lambda s, _n=nd: (layer,) + (0,) * (_n - 1))


def _mixer_call(x_in, pending, h0s, buf0s, lw, *, geo, cfg, snapshot, layer):
    d_model = lw["w_in"].shape[1]
    total = (geo["n_p_steps"] + geo["n_s_steps"] + 1) * geo["step_rows"]
    groups, dec_batch, _ = h0s.shape
    d_pool = buf0s.shape[-1]
    d_ssm = d_model - d_pool
    g = geo
    n_p, n_s = g["n_p_steps"], g["n_s_steps"]
    rows = g["step_rows"]
    per_step = rows // MOE_TILE
    n_local = g["n_local"]
    blk = lambda s: jnp.where(s == 0, n_p + n_s, s - 1)
    sblk = lambda s: jnp.clip(s - (n_p + 1), 0, n_s - 1)
    weights = _mixer_weights(lw)
    h_s_spec = pl.BlockSpec((groups, g["s_bblk"], LANES), lambda s: (0, sblk(s), 0))
    buf_s_spec = pl.BlockSpec((POOL_HIST, g["s_bblk"], d_pool), lambda s: (0, sblk(s), 0))
    x_specs, x_args, source = _input_specs(x_in, pending, geo, d_model, blk,
                                           lambda s: jnp.clip(s - 1, 0, n_p - 1), sblk)
    in_specs = x_specs + [h_s_spec, buf_s_spec] + [_layer_spec(w, layer) for w in weights]
    h_p_shape, buf_p_shape = (groups, g["batch"], LANES), (POOL_HIST, g["batch"], d_pool)
    out_specs = [pl.BlockSpec((rows, d_model), lambda s: (blk(s), 0)),
                 pl.BlockSpec((rows, LANES), lambda s: (blk(s), 0)),
                 pl.BlockSpec((per_step, n_local, d_model // 2), lambda s: (blk(s), 0, 0)),
                 pl.BlockSpec((per_step, 1, LANES), lambda s: (blk(s), 0, 0)),
                 pl.BlockSpec(h_p_shape, lambda s: (0, 0, 0)), pl.BlockSpec(buf_p_shape, lambda s: (0, 0, 0)),
                 h_s_spec, buf_s_spec]
    out_shape = [jax.ShapeDtypeStruct((total, d_model), F32), jax.ShapeDtypeStruct((total, LANES), F32),
                 jax.ShapeDtypeStruct((total // MOE_TILE, n_local, d_model // 2), jnp.uint32),
                 jax.ShapeDtypeStruct((total // MOE_TILE, 1, LANES), F32),
                 jax.ShapeDtypeStruct(h_p_shape, F32), jax.ShapeDtypeStruct(buf_p_shape, F32),
                 jax.ShapeDtypeStruct((groups, dec_batch, LANES), F32),
                 jax.ShapeDtypeStruct((POOL_HIST, dec_batch, d_pool), F32)]
    if snapshot:
        out_specs += [pl.BlockSpec(h_p_shape, lambda s: (0, 0, 0)), pl.BlockSpec(buf_p_shape, lambda s: (0, 0, 0))]
        out_shape += [jax.ShapeDtypeStruct(h_p_shape, F32), jax.ShapeDtypeStruct(buf_p_shape, F32)]
    scratch = ([pltpu.VMEM((groups, g["m_batch"], LANES), F32), pltpu.VMEM((POOL_HIST, g["m_batch"], d_pool), F32)]
               + _stream_scratch(g["n_meta"], g["m_batch"], d_model, d_ssm)
               + _stream_scratch(g["p_tile_l"], g["batch"], d_model, d_ssm)
               + _stream_scratch(g["dec_seq"], g["s_bblk"], d_model, d_ssm))
    return pl.pallas_call(
        functools.partial(_mixer_kernel, cfg=cfg, geo=geo, snapshot=snapshot, source=source),
        grid=(1 + n_p + n_s,),
        in_specs=in_specs,
        out_specs=out_specs,
        out_shape=out_shape,
        scratch_shapes=scratch,
        compiler_params=pltpu.CompilerParams(dimension_semantics=("arbitrary",), vmem_limit_bytes=VMEM_LIMIT),
        name="mixer",
    )(*x_args, h0s, buf0s, *weights)


def _mixer_tail_kernel(*refs, cfg, geo, source):
    get_x, refs = _split_inputs(refs, source, geo)
    get_x = get_x["prompt"]
    h0_ref, buf0_ref = refs[:2]
    w_refs = refs[2:2 + N_MIXER_WEIGHTS]
    rest = refs[2 + N_MIXER_WEIGHTS + 4:]
    x1_ref, route_ref, xloc_ref, cnt_ref, h_ref, buf_ref, u_scr, y_scr, cat_scr = rest
    pos = geo["n_meta"] + (geo["n_p_steps"] - 1) * geo["p_tile_l"]
    _stream_step(get_x, w_refs, lambda: h0_ref[...], lambda: buf0_ref[...], True, pos, x1_ref, route_ref, xloc_ref,
                 cnt_ref, h_ref, buf_ref, u_scr, y_scr, cat_scr, bblk=geo["batch"], tile_l=geo["p_tile_l"], cfg=cfg,
                 precise=True)


def _mixer_tail_call(x_in, pending, h0, buf0, lw32, x1, route, xloc, cnt, *, geo, cfg, layer):
    d_model = x1.shape[-1]
    d_pool = buf0.shape[-1]
    g = geo
    rows = g["step_rows"]
    per_step = rows // MOE_TILE
    last = g["n_p_steps"] - 1
    weights = _mixer_weights(lw32)

    def full(a):
        nd = a.ndim
        return pl.BlockSpec(a.shape, lambda s, _n=nd: (0,) * _n)

    x_specs, x_args, source = _input_specs(x_in, pending, geo, d_model, lambda s: last, lambda s: last, lambda s: 0)
    n_in = len(x_args) + 2 + len(weights)
    return pl.pallas_call(
        functools.partial(_mixer_tail_kernel, cfg=cfg, geo=geo, source=source),
        grid=(1,),
        in_specs=(x_specs + [full(h0), full(buf0)] + [_layer_spec(w, layer) for w in weights]
                  + [pl.BlockSpec(memory_space=pl.ANY)] * 4),
        out_specs=[pl.BlockSpec((rows, d_model), lambda s: (last, 0)),
                   pl.BlockSpec((rows, LANES), lambda s: (last, 0)),
                   pl.BlockSpec((per_step, g["n_local"], d_model // 2), lambda s: (last, 0, 0)),
                   pl.BlockSpec((per_step, 1, LANES), lambda s: (last, 0, 0)),
                   full(h0), full(buf0)],
        out_shape=[jax.ShapeDtypeStruct(a.shape, a.dtype) for a in (x1, route, xloc, cnt, h0, buf0)],
        scratch_shapes=_stream_scratch(g["p_tile_l"], g["batch"], d_model, d_model - d_pool),
        input_output_aliases={n_in + k: k for k in range(4)},
        compiler_params=pltpu.CompilerParams(dimension_semantics=("arbitrary",), vmem_limit_bytes=VMEM_LIMIT),
        name="mixer_tail",
    )(*x_args, h0, buf0, *weights, x1, route, xloc, cnt)


def _experts_kernel(exp_ref, nact_ref, src_ref, nval_ref, xl_in_ref, wg_ref, wu_ref, wd_ref, rows_ref,
                    xbuf, ybuf, wgu_scr, wd_scr, sem_in, sem_out):
    del xl_in_ref
    j = pl.program_id(0)
    f = wg_ref.shape[-1]
    chunks_per_tile = EXPERT_TILE // SUBLANES
    nact = nact_ref[0]

    def chunk_copy(tile, k, scatter):
        slot = tile % 2
        hbm = rows_ref.at[pl.ds(pl.multiple_of(src_ref[tile * chunks_per_tile + k], SUBLANES), SUBLANES), :]
        local = pl.ds(k * SUBLANES, SUBLANES) if isinstance(k, int) else pl.ds(
            pl.multiple_of(k * SUBLANES, SUBLANES), SUBLANES)
        if scatter:
            return pltpu.make_async_copy(ybuf.at[slot, local, :], hbm, sem_out.at[slot])
        return pltpu.make_async_copy(hbm, xbuf.at[slot, local, :], sem_in.at[slot])

    def for_tile_chunks(tile, op, scatter):
        def body(k, carry):
            op(chunk_copy(tile, k, scatter))
            return carry

        lax.fori_loop(0, nval_ref[tile], body, 0)

    def start_tile_chunks(tile, scatter):
        full = nval_ref[tile] == chunks_per_tile

        @pl.when(full)
        def _():
            for k in range(chunks_per_tile):
                chunk_copy(tile, k, scatter).start()

        @pl.when(jnp.logical_not(full))
        def _():
            for_tile_chunks(tile, start, scatter)

    def wait_tile_chunks(tile, scatter):
        slot = tile % 2
        full = nval_ref[tile] == chunks_per_tile

        @pl.when(full)
        def _():
            whole = rows_ref.at[pl.ds(0, EXPERT_TILE), :]
            if scatter:
                pltpu.make_async_copy(ybuf.at[slot], whole, sem_out.at[slot]).wait()
            else:
                pltpu.make_async_copy(whole, xbuf.at[slot], sem_in.at[slot]).wait()

        @pl.when(jnp.logical_not(full))
        def _():
            for_tile_chunks(tile, wait, scatter)

    start = lambda cp: cp.start()
    wait = lambda cp: cp.wait()

    @pl.when(j == 0)
    def _():
        xbuf[...] = jnp.zeros_like(xbuf)
        start_tile_chunks(0, False)

    @pl.when(j + 1 < nact)
    def _():
        start_tile_chunks(j + 1, False)

    @pl.when(jnp.logical_and(j >= 2, j - 2 < nact))
    def _():
        wait_tile_chunks(j - 2, True)

    active = j < nact
    changed = jnp.logical_or(j == 0, exp_ref[j] != exp_ref[jnp.maximum(j - 1, 0)])

    @pl.when(jnp.logical_and(active, changed))
    def _():
        wgu_scr[:, 0:f] = wg_ref[...].astype(BF16)
        wgu_scr[:, f:2 * f] = wu_ref[...].astype(BF16)
        wd_scr[...] = wd_ref[...].astype(BF16)

    @pl.when(active)
    def _():
        wait_tile_chunks(j, False)
        half = wgu_scr.shape[0] // 2
        t_lo, t_hi = _unpack_halves(xbuf[j % 2])
        gu = _dot(t_lo, wgu_scr[0:half, :]) + _dot(t_hi, wgu_scr[half:2 * half, :])
        act = jax.nn.silu(gu[:, 0:f]) * gu[:, f:2 * f]
        ybuf[j % 2] = _pack_halves(_dot(act.astype(BF16), wd_scr[...]))
        start_tile_chunks(j, True)


def _experts_call(tile_exp, n_active, src, nval, xloc, w_gate, w_up, w_down, layer, n_tiles):
    _, n_exp, d_model, f = w_gate.shape
    xl_flat = xloc.reshape(-1, xloc.shape[-1])
    wspec = lambda a, b: pl.BlockSpec((None, None, a, b), lambda i, ex, *_: (layer, ex[i], 0, 0))
    buf = pltpu.VMEM((2, EXPERT_TILE, d_model // 2), jnp.uint32)
    out = pl.pallas_call(
        _experts_kernel,
        grid_spec=pltpu.PrefetchScalarGridSpec(
            num_scalar_prefetch=4,
            grid=(n_tiles + 2,),
            in_specs=[pl.BlockSpec(memory_space=pl.ANY), wspec(d_model, f), wspec(d_model, f), wspec(f, d_model)],
            out_specs=pl.BlockSpec(memory_space=pl.ANY),
            scratch_shapes=[buf, buf, pltpu.VMEM((d_model, 2 * f), BF16), pltpu.VMEM((f, d_model), BF16),
                            pltpu.SemaphoreType.DMA((2,)), pltpu.SemaphoreType.DMA((2,))],
        ),
        out_shape=jax.ShapeDtypeStruct(xl_flat.shape, xl_flat.dtype),
        input_output_aliases={4: 0},
        compiler_params=pltpu.CompilerParams(dimension_semantics=("arbitrary",), vmem_limit_bytes=VMEM_LIMIT),
        name="moe_experts",
    )(tile_exp, n_active, src, nval, xl_flat, w_gate, w_up, w_down)
    return out.reshape(xloc.shape)


def _moe_experts(xloc, cnt, w_gate, w_up, w_down, *, layer):
    n_exp = w_gate.shape[1]
    nt, n_local, _ = xloc.shape
    total = nt * MOE_TILE
    cpt = EXPERT_TILE // SUBLANES
    n_chunks = n_local // SUBLANES
    counts = cnt[:, 0, :n_exp].astype(jnp.int32)
    run = (counts + SUBLANES - 1) // SUBLANES
    lend = jnp.cumsum(run, axis=1)
    loff = lend - run
    gend = jnp.cumsum(run, axis=0)
    gcum = gend - run
    seg = gend[-1]
    seg_tiles = (seg + cpt - 1) // cpt
    seg_end = jnp.cumsum(seg_tiles)
    seg_off = seg_end - seg_tiles
    n_tiles = (2 * total + nt * n_exp * (SUBLANES - 1) + n_exp * (EXPERT_TILE - 1)) // EXPERT_TILE + 1
    n_steps = n_tiles + 2
    n_active = seg_end[-1].reshape(1).astype(jnp.int32)
    tile_idx = jnp.minimum(jnp.arange(n_steps, dtype=jnp.int32), n_active - 1)
    tile_exp = jnp.minimum(jnp.sum((tile_idx[:, None] >= seg_end[None, :]).astype(jnp.int32), axis=1), n_exp - 1)
    is_e = tile_exp[:, None] == jnp.arange(n_exp, dtype=jnp.int32)[None, :]
    pick = lambda table: jnp.sum(jnp.where(is_e[:, None, :], table[None, :, :], 0), axis=2)
    pick1 = lambda vec: jnp.sum(jnp.where(is_e, vec[None, :], 0), axis=1)
    g = ((jnp.arange(n_steps, dtype=jnp.int32) - pick1(seg_off)) * cpt)[:, None] + jnp.arange(cpt, dtype=jnp.int32)
    gend_t, gcum_t, loff_t = pick(gend), pick(gcum), pick(loff)
    owner = jnp.logical_and(g[:, :, None] >= gcum_t[:, None, :], g[:, :, None] < gend_t[:, None, :])
    tile_base = jnp.arange(nt, dtype=jnp.int32) * n_chunks
    src = jnp.sum(jnp.where(owner, (tile_base[None, :] + loff_t - gcum_t)[:, None, :] + g[:, :, None], 0), axis=2)
    src = (src * SUBLANES).reshape(-1)
    nval = jnp.clip(pick1(seg) - g[:, 0], 0, cpt)
    return _experts_call(tile_exp, n_active, src, nval, xloc, w_gate, w_up, w_down, layer, n_tiles)


def _finish_kernel(x1_ref, yl_ref, route_ref, gfin_ref, yp_ref, ys_ref, *, n_p_steps):
    i = pl.program_id(0)
    x = _moe_combine(x1_ref, yl_ref, route_ref)
    ms = jnp.mean(x * x, axis=-1, keepdims=True)
    y = x * lax.rsqrt(ms + EPS) * gfin_ref[...]
    pb, pl_ = yp_ref.shape[0], yp_ref.shape[1]
    sb, sl = ys_ref.shape[0], ys_ref.shape[1]

    @pl.when(i < n_p_steps)
    def _():
        for l in range(pl_):
            yp_ref[:, l, :] = y[l * pb:(l + 1) * pb, :]

    @pl.when(i >= n_p_steps)
    def _():
        for l in range(sl):
            ys_ref[:, l, :] = y[l * sb:(l + 1) * sb, :]


def _finish_call(x1, yl, route, g_final, *, geo):
    d_model = x1.shape[-1]
    g = geo
    rows = g["step_rows"]
    n_p, n_s = g["n_p_steps"], g["n_s_steps"]
    return pl.pallas_call(
        functools.partial(_finish_kernel, n_p_steps=n_p),
        grid=(n_p + n_s,),
        in_specs=[pl.BlockSpec((rows, d_model), lambda i: (i, 0)),
                  pl.BlockSpec((rows // MOE_TILE, g["n_local"], d_model // 2), lambda i: (i, 0, 0)),
                  pl.BlockSpec((rows, LANES), lambda i: (i, 0)),
                  pl.BlockSpec((1, d_model), lambda i: (0, 0))],
        out_specs=[pl.BlockSpec((g["batch"], g["p_tile_l"], d_model), lambda i: (0, jnp.minimum(i, n_p - 1), 0)),
                   pl.BlockSpec((g["s_bblk"], g["dec_seq"], d_model), lambda i: (jnp.clip(i - n_p, 0, n_s - 1), 0, 0))],
        out_shape=[jax.ShapeDtypeStruct((g["batch"], g["seq"], d_model), F32),
                   jax.ShapeDtypeStruct((g["dec_batch"], g["dec_seq"], d_model), F32)],
        compiler_params=pltpu.CompilerParams(dimension_semantics=("arbitrary",), vmem_limit_bytes=VMEM_LIMIT),
        name="finish",
    )(x1, yl, route, g_final)


def kernel(x_prompt, x_sample, state_ssm_re, state_ssm_im, state_pool, meta_tokens, norm_mix_g, w_in, ssm_a_re,
           ssm_a_im, ssm_log_dt, ssm_b_re, ssm_b_im, ssm_c_re, ssm_c_im, ssm_d, w_glu, b_glu, w_pool, pool_scale,
           w_out, norm_ffn_g, w_router_group, b_router_group, w_router_expert, b_router_expert, w_gate, w_up,
           w_down, norm_final_g):
    batch, seq, d_model = x_prompt.shape
    dec_batch, dec_seq, _ = x_sample.shape
    depth, _, _, nstate = state_ssm_re.shape
    n_meta = meta_tokens.shape[0]
    pool_buf, d_pool = state_pool.shape[2], state_pool.shape[3]
    n_pool_groups = w_pool.shape[1]
    windows = tuple(2 ** (k + 1) for k in range(n_pool_groups))
    assert pool_buf == max(windows) - 1 == POOL_HIST - 1
    n_expert_groups = w_router_group.shape[-1]
    n_experts = w_router_expert.shape[-1]
    cfg = dict(windows=windows, n_expert_groups=n_expert_groups, experts_per_group=n_experts // n_expert_groups)

    step_rows = 512
    p_tile_l, s_bblk, m_batch = step_rows // batch, step_rows // dec_seq, step_rows // n_meta
    assert m_batch >= batch and seq % p_tile_l == 0 and dec_batch % s_bblk == 0
    assert m_batch * n_meta == step_rows and step_rows % MOE_TILE == 0
    n_local = 2 * MOE_TILE + n_experts * (SUBLANES - 1)
    n_local = (n_local + LANES - 1) // LANES * LANES
    geo = dict(batch=batch, seq=seq, dec_batch=dec_batch, dec_seq=dec_seq, n_meta=n_meta, step_rows=step_rows,
               p_tile_l=p_tile_l, s_bblk=s_bblk, m_batch=m_batch, n_p_steps=seq // p_tile_l,
               n_s_steps=dec_batch // s_bblk, n_local=n_local)

    mq, pmat, a8 = _s5_prepare(ssm_a_re, ssm_a_im, ssm_log_dt, ssm_b_re, ssm_b_im, ssm_c_re, ssm_c_im)

    x = (x_prompt.astype(F32), x_sample.astype(F32), meta_tokens.astype(F32))

    row2 = lambda v: v.reshape(1, -1).astype(F32)
    outs = {k: [] for k in ("hp", "bp", "hs", "bs")}
    pending = None
    rows3 = lambda v: v.reshape(depth, 1, -1).astype(F32)
    w_r = jnp.concatenate([w_router_group, w_router_expert], axis=2).astype(F32)
    w_r = jnp.pad(w_r, ((0, 0), (0, 0), (0, LANES - w_r.shape[2])))
    b_r = jnp.concatenate([b_router_group, b_router_expert], axis=1).astype(F32)
    lw32 = dict(g_mix=rows3(norm_mix_g), w_in=w_in.astype(F32), mq=mq, pmat=pmat, a8=a8, ssm_d=rows3(ssm_d),
                w_glu=w_glu.astype(F32), b_glu=rows3(b_glu), w_pool=w_pool.astype(F32), pool_scale=rows3(pool_scale),
                w_out=w_out.astype(F32), g_ffn=rows3(norm_ffn_g), w_r=w_r,
                b_r=jnp.pad(b_r, ((0, 0), (0, LANES - b_r.shape[1])))[:, None, :])
    lw = dict(lw32, **{k: lw32[k].astype(BF16) for k in ("w_in", "mq", "pmat", "w_glu", "w_pool", "w_out")})
    for l in range(depth):
        h0_s = jnp.concatenate([state_ssm_re[l], state_ssm_im[l]], axis=-1).transpose(1, 0, 2).astype(F32)
        buf0_s = jnp.pad(state_pool[l].astype(F32).transpose(1, 0, 2), ((1, 0), (0, 0), (0, 0)))
        redo_last = l < depth - 1
        res = _mixer_call(x, pending, h0_s, buf0_s, lw, geo=geo, cfg=cfg, snapshot=redo_last, layer=l)
        x1, route, xloc, cnt, h_p, buf_p, h_s, buf_s = res[:8]
        if redo_last:
            x1, route, xloc, cnt, h_p, buf_p = _mixer_tail_call(x, pending, res[8], res[9], lw32, x1, route, xloc,
                                                                 cnt, geo=geo, cfg=cfg, layer=l)
        outs["hp"].append(h_p)
        outs["bp"].append(buf_p)
        outs["hs"].append(h_s)
        outs["bs"].append(buf_s)
        x, pending = x1, (_moe_experts(xloc, cnt, w_gate, w_up, w_down, layer=l), route)

    y_prompt, y_sample = _finish_call(x, pending[0], pending[1], row2(norm_final_g), geo=geo)
    st = lambda hs: jnp.stack(hs).transpose(0, 2, 1, 3)
    pl_out = lambda bs: jnp.stack(bs)[:, 1:].transpose(0, 2, 1, 3)
    hp, hs = st(outs["hp"]), st(outs["hs"])
    return (y_prompt, y_sample, hp[..., :nstate], hp[..., nstate:], pl_out(outs["bp"]),
            hs[..., :nstate], hs[..., nstate:], pl_out(outs["bs"]))
```

```python
import functools

import numpy as np
import jax
import jax.numpy as jnp
from jax import lax
from jax.experimental import pallas as pl
from jax.experimental.pallas import tpu as pltpu

F32 = jnp.float32
BF16 = jnp.bfloat16

LANES = 128
SUBLANES = 8
CHUNK = 8
POOL_HIST = 16
MOE_TILE = 256
EXPERT_TILE = 512
VMEM_LIMIT = 56 * 1024 * 1024
EPS = 1e-6
NEG = -1e30
PAST_LEN = 16384


def _dot(a, b):
    return jnp.dot(a, b, preferred_element_type=F32)


def _split_bf16(x):
    hi = x.astype(BF16)
    return hi, (x - hi.astype(F32)).astype(BF16)


def _dot3(a, b, dims=(((1,), (0,)), ((), ()))):
    a_hi, a_lo = _split_bf16(a)
    b_hi, b_lo = _split_bf16(b)
    dg = lambda x, y: lax.dot_general(x, y, dims, preferred_element_type=F32)
    return dg(a_hi, b_hi) + dg(a_lo, b_hi) + dg(a_hi, b_lo)


def _mm(a, w, precise):
    if precise:
        return _dot3(a, w)
    return _dot(a.astype(BF16), w)


def _pack_halves(x):
    k = x.shape[-1] // 2
    return pltpu.pack_elementwise([x[:, :k], x[:, k:]], packed_dtype=BF16)


def _unpack_halves(p):
    lo = pltpu.unpack_elementwise(p, index=0, packed_dtype=BF16, unpacked_dtype=F32)
    hi = pltpu.unpack_elementwise(p, index=1, packed_dtype=BF16, unpacked_dtype=F32)
    return lo.astype(BF16), hi.astype(BF16)


def _dot_nt_f32(a, b):
    return _dot3(a, b, (((1,), (1,)), ((), ())))


def _s5_prep_kernel(a_re_ref, a_im_ref, ldt_ref, bt_ref, btx_ref, c_ref, cx_ref, mq_ref, p_ref, a8_ref):
    ch = bt_ref.shape[1]
    first_half = lax.broadcasted_iota(jnp.int32, (1, LANES), 1) < LANES // 2
    sgn = jnp.where(first_half, 1.0, -1.0)
    for s in range(CHUNK):
        a_re = a_re_ref[s]
        a_im = a_im_ref[s]
        dt = jnp.exp(ldt_ref[s])
        mag = jnp.exp(dt * a_re)
        abr = mag * jnp.cos(dt * a_im)
        abi = mag * jnp.sin(dt * a_im)
        den = a_re * a_re + a_im * a_im
        nr = abr - 1.0
        f_re = (nr * a_re + abi * a_im) / den
        f_im = (abi * a_re - nr * a_im) / den
        bb = f_re * bt_ref[s] - sgn * f_im * btx_ref[s]
        cc = c_ref[s] * sgn
        ccx = -cx_ref[s]
        bbx = f_re * btx_ref[s] + sgn * f_im * bt_ref[s]
        pr = jnp.ones_like(a_re)
        pi = jnp.zeros_like(a_re)
        e2, w2 = [], []
        for k in range(CHUNK + 1):
            e2.append(pr * cc + pi * ccx)
            w2.append(pr * bb - sgn * pi * bbx)
            if k < CHUNK:
                pr, pi = pr * abr - pi * abi, pr * abi + pi * abr
        a8_ref[s, 0:1, :] = pr
        a8_ref[s, 1:2, :] = -sgn * pi
        a8_ref[s, 2:3, :] = sgn * pi
        zero = jnp.zeros_like(bb)
        jpos = [(p - s) % CHUNK for p in range(CHUNK)]
        for p in range(CHUNK):
            j = jpos[p]
            taps = jnp.concatenate([e2[jp - j] if jp >= j else zero for jp in jpos], axis=0)
            rows = slice(p * ch, (p + 1) * ch)
            mq_ref[s, rows, 0:LANES] = _dot_nt_f32(bb, taps)
            mq_ref[s, rows, LANES:2 * LANES] = w2[CHUNK - 1 - j]
        pt = jnp.concatenate([e2[jp + 1] for jp in jpos], axis=0)
        p_ref[s] = pt.T


def _s5_prepare(a_re, a_im, log_dt, b_re, b_im, c_re, c_im):
    depth, groups, nstate = a_re.shape
    ch = b_re.shape[-1]
    dg = depth * groups
    assert ch * CHUNK == LANES and 2 * nstate == LANES and groups % CHUNK == 0
    dup = lambda x: jnp.concatenate([x, x], axis=-1).reshape(dg, 1, LANES).astype(F32)
    btr = jnp.swapaxes(b_re.reshape(dg, nstate, ch), 1, 2).astype(F32)
    bti = jnp.swapaxes(b_im.reshape(dg, nstate, ch), 1, 2).astype(F32)
    cr = c_re.reshape(dg, ch, nstate).astype(F32)
    ci = c_im.reshape(dg, ch, nstate).astype(F32)
    pair = lambda x, y: jnp.concatenate([x, y], axis=-1)
    blk = lambda *shape: pl.BlockSpec((CHUNK,) + shape, lambda g: (g,) + (0,) * len(shape))
    mq, pmat, a8 = pl.pallas_call(
        _s5_prep_kernel,
        grid=(dg // CHUNK,),
        in_specs=[blk(1, LANES), blk(1, LANES), blk(1, 1)] + [blk(ch, LANES)] * 4,
        out_specs=[blk(LANES, 2 * LANES), blk(LANES, LANES), blk(3, LANES)],
        out_shape=[jax.ShapeDtypeStruct((dg, LANES, 2 * LANES), F32), jax.ShapeDtypeStruct((dg, LANES, LANES), F32),
                   jax.ShapeDtypeStruct((dg, 3, LANES), F32)],
        name="s5_prep",
    )(dup(a_re), dup(a_im), log_dt.reshape(dg, 1, 1).astype(F32), pair(btr, bti), pair(bti, btr),
      pair(cr, ci), pair(ci, cr))
    shp = lambda x: x.reshape((depth, groups) + x.shape[1:])
    return shp(mq), shp(pmat), shp(a8)


N_MIXER_WEIGHTS = 14


def _moe_combine(x1_ref, yl_ref, route_ref):
    rows = x1_ref.shape[0]
    n_local = yl_ref.shape[1]
    lio = lax.broadcasted_iota(jnp.int32, (MOE_TILE, n_local), 1)
    parts = []
    for hh in range(rows // MOE_TILE):
        r = route_ref[hh * MOE_TILE:(hh + 1) * MOE_TILE, :]
        wperm = jnp.where(lio == r[:, 4:5].astype(jnp.int32), r[:, 2:3],
                          jnp.where(lio == r[:, 5:6].astype(jnp.int32), r[:, 3:4], 0.0)).astype(BF16)
        y_lo, y_hi = _unpack_halves(yl_ref[hh])
        parts.append(jnp.concatenate([_dot(wperm, y_lo), _dot(wperm, y_hi)], axis=-1))
    return x1_ref[...] + jnp.concatenate(parts, axis=0)


def _stream_step(get_x, w_refs, h_init, buf_init, first, pos_base, x1_ref, route_ref, xloc_ref, cnt_ref,
                 h_ref, buf_ref, u_scr, y_scr, cat_scr, *, bblk, tile_l, cfg, precise):
    (gmix_ref, win_ref, mq_ref, p_ref, a8_ref, d_ref, wglu_ref, bglu_ref, wpool_ref, pscale_ref, wout_ref,
     gffn_ref, wr_ref, br_ref) = w_refs
    windows = cfg["windows"]
    n_expert_groups = cfg["n_expert_groups"]
    experts_per_group = cfg["experts_per_group"]
    rows = tile_l * bblk
    nk = tile_l // CHUNK
    n = nk * bblk
    d_model = x1_ref.shape[-1]
    d_ssm = y_scr.shape[-1]
    d_pool = d_model - d_ssm
    hist = POOL_HIST * bblk

    def _init():
        h_ref[...] = h_init()
        cat_scr[0:hist, :] = buf_init().reshape(hist, d_pool)

    if first is True:
        _init()
    else:
        pl.when(first)(_init)

    x = get_x()
    ms = jnp.mean(x * x, axis=-1, keepdims=True)
    h = x * lax.rsqrt(ms + EPS) * gmix_ref[...]
    u = _mm(h, win_ref[...], precise)
    u_scr[...] = u.reshape(nk, CHUNK, bblk, d_model)

    slot = lax.broadcasted_iota(jnp.int32, (n, LANES), 1) // (LANES // CHUNK)
    for v in range(d_ssm // LANES):
        rolled = []
        for j in range(CHUNK):
            uj = u_scr[:, j, :, v * LANES:(v + 1) * LANES].reshape(n, LANES)
            rolled.append(pltpu.roll(uj, (LANES // CHUNK) * j, axis=1) if j else uj)
        ys = []
        for s in range(CHUNK):
            g = v * CHUNK + s
            ug = rolled[0]
            for j in range(1, CHUNK):
                ug = jnp.where(slot == (s + j) % CHUNK, rolled[j], ug)
            yq = _mm(ug, mq_ref[g], precise)
            sg = yq[:, LANES:]
            sg_sw = pltpu.roll(sg, LANES // 2, axis=1)
            a8c = a8_ref[g, 0:1, :]
            a8s = a8_ref[g, 1:2, :]
            a8w = a8_ref[g, 2:3, :]
            hcur = h_ref[g]
            hsw = pltpu.roll(hcur, LANES // 2, axis=1)
            hins = []
            for k in range(nk):
                hins.append(hcur)
                sk = sg[k * bblk:(k + 1) * bblk]
                skw = sg_sw[k * bblk:(k + 1) * bblk]
                hcur, hsw = a8c * hcur + a8s * hsw + sk, a8c * hsw + a8w * hcur + skw
            h_ref[g] = hcur
            hin = jnp.concatenate(hins, axis=0) if nk > 1 else hins[0]
            ys.append(yq[:, :LANES] + _mm(hin, p_ref[g], precise))
        for j in range(CHUNK):
            z = ys[0]
            for s in range(1, CHUNK):
                z = jnp.where(slot == (s + j) % CHUNK, ys[s], z)
            if j:
                z = pltpu.roll(z, LANES - (LANES // CHUNK) * j, axis=1)
            y_scr[:, j, :, v * LANES:(v + 1) * LANES] = z.reshape(nk, bblk, LANES)

    u_ssm = u_scr[:, :, :, 0:d_ssm].reshape(rows, d_ssm)
    y = y_scr[...].reshape(rows, d_ssm) + d_ref[...] * u_ssm
    gl = jax.nn.gelu(y)
    ssm_out = gl * jax.nn.sigmoid(_mm(gl, wglu_ref[...], precise) + bglu_ref[...])

    up = u_scr[:, :, :, d_ssm:].reshape(rows, d_pool)
    cat_scr[hist:hist + rows, :] = up
    row_pos = lax.broadcasted_iota(jnp.int32, (rows, 1), 0) // bblk
    pos = pos_base + row_pos
    pool_parts = []
    pgrp = d_pool // len(windows)
    for kk, w in enumerate(windows):
        lo = kk * pgrp
        acc = cat_scr[hist:hist + rows, lo:lo + pgrp]
        cur = acc
        for sft in range(1, w):
            acc = acc + cat_scr[hist - sft * bblk:hist - sft * bblk + rows, lo:lo + pgrp]
        inv = 1.0 / jnp.minimum(pos + 1, w).astype(F32)
        diff = acc * inv - cur
        pool_parts.append(_mm(diff, wpool_ref[kk], precise))
    pool_out = jnp.concatenate(pool_parts, axis=-1) * pscale_ref[...]
    tail = cat_scr[rows:rows + hist, :]
    cat_scr[0:hist, :] = tail
    buf_ref[...] = tail.reshape(POOL_HIST, bblk, d_pool)

    if precise:
        mix = jnp.concatenate([ssm_out, pool_out], axis=-1)
    else:
        mix = jnp.concatenate([ssm_out.astype(BF16), pool_out.astype(BF16)], axis=-1)
    x1 = x + _mm(mix, wout_ref[...], precise)
    x1_ref[...] = x1
    ms2 = jnp.mean(x1 * x1, axis=-1, keepdims=True)
    t = x1 * lax.rsqrt(ms2 + EPS) * gffn_ref[...]

    t_hi, t_lo = _split_bf16(t)
    w_hi, w_lo = _split_bf16(wr_ref[...])
    hi_both = _dot(t_hi, jnp.concatenate([w_hi, w_lo], axis=-1))
    logits = hi_both[:, :LANES] + hi_both[:, LANES:] + _dot(t_lo, w_hi) + br_ref[...]
    lane = lax.broadcasted_iota(jnp.int32, (rows, LANES), 1).astype(F32)
    big = float(4 * LANES)
    lg = jnp.where(lane < n_expert_groups, logits, NEG)
    mg = jnp.max(lg, axis=-1, keepdims=True)
    gsel = jnp.min(jnp.where(lg == mg, lane, big), axis=-1, keepdims=True)
    p_sel = 1.0 / jnp.sum(jnp.exp(lg - mg), axis=-1, keepdims=True)
    e_lo = n_expert_groups + gsel * experts_per_group
    le = jnp.where((lane >= e_lo) & (lane < e_lo + experts_per_group), logits, NEG)
    v1 = jnp.max(le, axis=-1, keepdims=True)
    i1 = jnp.min(jnp.where(le == v1, lane, big), axis=-1, keepdims=True)
    le2 = jnp.where(lane == i1, NEG, le)
    v2 = jnp.max(le2, axis=-1, keepdims=True)
    i2 = jnp.min(jnp.where(le2 == v2, lane, big), axis=-1, keepdims=True)
    ex = jnp.exp(v2 - v1)
    q1 = p_sel / (1.0 + ex)
    q2 = q1 * ex
    e1 = i1 - n_expert_groups
    e2 = i2 - n_expert_groups
    route = jnp.where(lane == 0.0, e1, jnp.where(lane == 1.0, e2, jnp.where(lane == 2.0, q1,
                      jnp.where(lane == 3.0, q2, 0.0))))

    lf = lax.broadcasted_iota(jnp.int32, (MOE_TILE, LANES), 1).astype(F32)
    n_local = xloc_ref.shape[1]
    before_e = (lax.broadcasted_iota(jnp.int32, (LANES, LANES), 0)
                < lax.broadcasted_iota(jnp.int32, (LANES, LANES), 1)).astype(BF16)
    earlier = (lax.broadcasted_iota(jnp.int32, (MOE_TILE, MOE_TILE), 0)
               > lax.broadcasted_iota(jnp.int32, (MOE_TILE, MOE_TILE), 1)).astype(BF16)
    local_row = lax.broadcasted_iota(jnp.int32, (n_local, MOE_TILE), 0)
    route_parts = []
    for hh in range(rows // MOE_TILE):
        sl = slice(hh * MOE_TILE, (hh + 1) * MOE_TILE)
        rt = route[sl]
        oh1 = lf == rt[:, 0:1]
        oh2 = lf == rt[:, 1:2]
        oh = jnp.logical_or(oh1, oh2).astype(F32)
        cnt = jnp.sum(oh, axis=0, keepdims=True)
        run_chunks = jnp.floor((cnt + (SUBLANES - 1)) * (1.0 / SUBLANES))
        loff = SUBLANES * _dot(jnp.broadcast_to(run_chunks, (SUBLANES, LANES)).astype(BF16), before_e)[0:1]
        where_to = loff + _dot(earlier, oh.astype(BF16))
        lpos1 = jnp.sum(jnp.where(oh1, where_to, 0.0), axis=-1, keepdims=True)
        lpos2 = jnp.sum(jnp.where(oh2, where_to, 0.0), axis=-1, keepdims=True)
        lpos_t = jnp.where(lf == 0.0, lpos1, jnp.where(lf == 1.0, lpos2, 0.0)).T
        perm = jnp.logical_or(local_row == lpos_t[0:1].astype(jnp.int32),
                              local_row == lpos_t[1:2].astype(jnp.int32)).astype(BF16)
        xloc_ref[hh] = _pack_halves(_dot(perm, t_hi[sl]))
        cnt_ref[hh] = cnt
        route_parts.append(jnp.where(lf == 4.0, lpos1, jnp.where(lf == 5.0, lpos2, rt)))
    route_ref[...] = jnp.concatenate(route_parts, axis=0)


def _split_inputs(refs, source, geo):
    if source == "raw":
        xp_ref, xs_ref, meta_ref = refs[:3]
        by_position = lambda ref: (lambda: jnp.concatenate([ref[:, l, :] for l in range(ref.shape[1])], axis=0))
        meta = lambda: jnp.concatenate([jnp.broadcast_to(meta_ref[l:l + 1, :], (geo["m_batch"], meta_ref.shape[1]))
                                        for l in range(meta_ref.shape[0])], axis=0)
        return dict(meta=meta, prompt=by_position(xp_ref), sample=by_position(xs_ref)), refs[3:]
    if source == "pending":
        x_ref, yl_ref, rprev_ref = refs[:3]
        get, refs = (lambda: _moe_combine(x_ref, yl_ref, rprev_ref)), refs[3:]
    else:
        x_ref = refs[0]
        get, refs = (lambda: x_ref[...]), refs[1:]
    return dict(meta=get, prompt=get, sample=get), refs


def _mixer_kernel(*refs, cfg, geo, skip_last, source):
    get_x, refs = _split_inputs(refs, source, geo)
    h0s_ref, buf0s_ref = refs[:2]
    w_refs = refs[2:2 + N_MIXER_WEIGHTS]
    rest = refs[2 + N_MIXER_WEIGHTS:]
    x1_ref, route_ref, xloc_ref, cnt_ref, hp_ref, bufp_ref, hs_ref, bufs_ref = rest[:8]
    hm, bufm, u_m, y_m, cat_m, u_p, y_p, cat_p, u_s, y_s, cat_s = rest[8:]
    outs = (x1_ref, route_ref, xloc_ref, cnt_ref)
    s = pl.program_id(0)
    n_p = geo["n_p_steps"]
    n_p_here = n_p - 1 if skip_last else n_p
    batch = geo["batch"]

    @pl.when(s == 0)
    def _meta():
        _stream_step(get_x["meta"], w_refs, lambda: jnp.zeros(hm.shape, F32), lambda: jnp.zeros(bufm.shape, F32), True, 0,
                     *outs, hm, bufm, u_m, y_m, cat_m, bblk=geo["m_batch"], tile_l=geo["n_meta"], cfg=cfg,
                     precise=False)

    if skip_last:
        @pl.when(s == n_p)
        def _placeholder():
            for ref in outs:
                ref[...] = jnp.zeros_like(ref)

    @pl.when(jnp.logical_and(s >= 1, s <= n_p_here))
    def _prompt():
        _stream_step(get_x["prompt"], w_refs, lambda: hm[:, 0:batch, :], lambda: bufm[:, 0:batch, :], s == 1,
                     geo["n_meta"] + (s - 1) * geo["p_tile_l"], *outs, hp_ref, bufp_ref, u_p, y_p, cat_p,
                     bblk=batch, tile_l=geo["p_tile_l"], cfg=cfg, precise=False)

    @pl.when(s > n_p)
    def _sample():
        _stream_step(get_x["sample"], w_refs, lambda: h0s_ref[...], lambda: buf0s_ref[...], True, PAST_LEN,
                     *outs, hs_ref, bufs_ref, u_s, y_s, cat_s, bblk=geo["s_bblk"], tile_l=geo["dec_seq"], cfg=cfg,
                     precise=False)


def _mixer_weights(lw):
    return [lw["g_mix"], lw["w_in"], lw["mq"], lw["pmat"], lw["a8"], lw["ssm_d"], lw["w_glu"], lw["b_glu"],
            lw["w_pool"], lw["pool_scale"], lw["w_out"], lw["g_ffn"], lw["w_r"], lw["b_r"]]


def _stream_scratch(tile_l, bblk, d_model, d_ssm):
    return [pltpu.VMEM((tile_l // CHUNK, CHUNK, bblk, d_model), F32),
            pltpu.VMEM((tile_l // CHUNK, CHUNK, bblk, d_ssm), F32),
            pltpu.VMEM(((POOL_HIST + tile_l) * bblk, d_model - d_ssm), F32)]


def _input_specs(x_in, pending, geo, d_model, blk, pblk, sblk):
    g = geo
    rows = g["step_rows"]
    if isinstance(x_in, tuple):
        specs = [pl.BlockSpec((g["batch"], g["p_tile_l"], d_model), lambda s: (0, pblk(s), 0)),
                 pl.BlockSpec((g["s_bblk"], g["dec_seq"], d_model), lambda s: (sblk(s), 0, 0)),
                 pl.BlockSpec(x_in[2].shape, lambda s: (0, 0))]
        return specs, list(x_in), "raw"
    specs = [pl.BlockSpec((rows, d_model), lambda s: (blk(s), 0))]
    if pending is None:
        return specs, [x_in], "flat"
    specs += [pl.BlockSpec((rows // MOE_TILE, g["n_local"], d_model // 2), lambda s: (blk(s), 0, 0)),
              pl.BlockSpec((rows, LANES), lambda s: (blk(s), 0))]
    return specs, [x_in, *pending], "pending"


def _layer_spec(a, layer):
    nd = a.ndim
    return pl.BlockSpec((None,) + a.shape[1:], lambda s, _n=nd: (layer,) + (0,) * (_n - 1))


def _mixer_call(x_in, pending, h0s, buf0s, lw, *, geo, cfg, skip_last, layer):
    d_model = lw["w_in"].shape[1]
    total = (geo["n_p_steps"] + geo["n_s_steps"] + 1) * geo["step_rows"]
    groups, dec_batch, _ = h0s.shape
    d_pool = buf0s.shape[-1]
    d_ssm = d_model - d_pool
    g = geo
    n_p, n_s = g["n_p_steps"], g["n_s_steps"]
    rows = g["step_rows"]
    per_step = rows // MOE_TILE
    n_local = g["n_local"]
    blk = lambda s: jnp.where(s == 0, n_p + n_s, s - 1)
    sblk = lambda s: jnp.clip(s - (n_p + 1), 0, n_s - 1)
    weights = _mixer_weights(lw)
    h_s_spec = pl.BlockSpec((groups, g["s_bblk"], LANES), lambda s: (0, sblk(s), 0))
    buf_s_spec = pl.BlockSpec((POOL_HIST, g["s_bblk"], d_pool), lambda s: (0, sblk(s), 0))
    x_specs, x_args, source = _input_specs(x_in, pending, geo, d_model, blk,
                                           lambda s: jnp.clip(s - 1, 0, n_p - 1), sblk)
    in_specs = x_specs + [h_s_spec, buf_s_spec] + [_layer_spec(w, layer) for w in weights]
    h_p_shape, buf_p_shape = (groups, g["batch"], LANES), (POOL_HIST, g["batch"], d_pool)
    out_specs = [pl.BlockSpec((rows, d_model), lambda s: (blk(s), 0)),
                 pl.BlockSpec((rows, LANES), lambda s: (blk(s), 0)),
                 pl.BlockSpec((per_step, n_local, d_model // 2), lambda s: (blk(s), 0, 0)),
                 pl.BlockSpec((per_step, 1, LANES), lambda s: (blk(s), 0, 0)),
                 pl.BlockSpec(h_p_shape, lambda s: (0, 0, 0)), pl.BlockSpec(buf_p_shape, lambda s: (0, 0, 0)),
                 h_s_spec, buf_s_spec]
    out_shape = [jax.ShapeDtypeStruct((total, d_model), F32), jax.ShapeDtypeStruct((total, LANES), F32),
                 jax.ShapeDtypeStruct((total // MOE_TILE, n_local, d_model // 2), jnp.uint32),
                 jax.ShapeDtypeStruct((total // MOE_TILE, 1, LANES), F32),
                 jax.ShapeDtypeStruct(h_p_shape, F32), jax.ShapeDtypeStruct(buf_p_shape, F32),
                 jax.ShapeDtypeStruct((groups, dec_batch, LANES), F32),
                 jax.ShapeDtypeStruct((POOL_HIST, dec_batch, d_pool), F32)]
    scratch = ([pltpu.VMEM((groups, g["m_batch"], LANES), F32), pltpu.VMEM((POOL_HIST, g["m_batch"], d_pool), F32)]
               + _stream_scratch(g["n_meta"], g["m_batch"], d_model, d_ssm)
               + _stream_scratch(g["p_tile_l"], g["batch"], d_model, d_ssm)
               + _stream_scratch(g["dec_seq"], g["s_bblk"], d_model, d_ssm))
    return pl.pallas_call(
        functools.partial(_mixer_kernel, cfg=cfg, geo=geo, skip_last=skip_last, source=source),
        grid=(1 + n_p + n_s,),
        in_specs=in_specs,
        out_specs=out_specs,
        out_shape=out_shape,
        scratch_shapes=scratch,
        compiler_params=pltpu.CompilerParams(dimension_semantics=("arbitrary",), vmem_limit_bytes=VMEM_LIMIT),
        name="mixer",
    )(*x_args, h0s, buf0s, *weights)


def _mixer_tail_kernel(*refs, cfg, geo, source):
    get_x, refs = _split_inputs(refs, source, geo)
    get_x = get_x["prompt"]
    h0_ref, buf0_ref = refs[:2]
    w_refs = refs[2:2 + N_MIXER_WEIGHTS]
    rest = refs[2 + N_MIXER_WEIGHTS + 4:]
    x1_ref, route_ref, xloc_ref, cnt_ref, h_ref, buf_ref, u_scr, y_scr, cat_scr = rest
    pos = geo["n_meta"] + (geo["n_p_steps"] - 1) * geo["p_tile_l"]
    _stream_step(get_x, w_refs, lambda: h0_ref[...], lambda: buf0_ref[...], True, pos, x1_ref, route_ref, xloc_ref,
                 cnt_ref, h_ref, buf_ref, u_scr, y_scr, cat_scr, bblk=geo["batch"], tile_l=geo["p_tile_l"], cfg=cfg,
                 precise=True)


def _mixer_tail_call(x_in, pending, h0, buf0, lw32, x1, route, xloc, cnt, *, geo, cfg, layer):
    d_model = x1.shape[-1]
    d_pool = buf0.shape[-1]
    g = geo
    rows = g["step_rows"]
    per_step = rows // MOE_TILE
    last = g["n_p_steps"] - 1
    weights = _mixer_weights(lw32)

    def full(a):
        nd = a.ndim
        return pl.BlockSpec(a.shape, lambda s, _n=nd: (0,) * _n)

    x_specs, x_args, source = _input_specs(x_in, pending, geo, d_model, lambda s: last, lambda s: last, lambda s: 0)
    n_in = len(x_args) + 2 + len(weights)
    return pl.pallas_call(
        functools.partial(_mixer_tail_kernel, cfg=cfg, geo=geo, source=source),
        grid=(1,),
        in_specs=(x_specs + [full(h0), full(buf0)] + [_layer_spec(w, layer) for w in weights]
                  + [pl.BlockSpec(memory_space=pl.ANY)] * 4),
        out_specs=[pl.BlockSpec((rows, d_model), lambda s: (last, 0)),
                   pl.BlockSpec((rows, LANES), lambda s: (last, 0)),
                   pl.BlockSpec((per_step, g["n_local"], d_model // 2), lambda s: (last, 0, 0)),
                   pl.BlockSpec((per_step, 1, LANES), lambda s: (last, 0, 0)),
                   full(h0), full(buf0)],
        out_shape=[jax.ShapeDtypeStruct(a.shape, a.dtype) for a in (x1, route, xloc, cnt, h0, buf0)],
        scratch_shapes=_stream_scratch(g["p_tile_l"], g["batch"], d_model, d_model - d_pool),
        input_output_aliases={n_in + k: k for k in range(4)},
        compiler_params=pltpu.CompilerParams(dimension_semantics=("arbitrary",), vmem_limit_bytes=VMEM_LIMIT),
        name="mixer_tail",
    )(*x_args, h0, buf0, *weights, x1, route, xloc, cnt)


def _experts_kernel(exp_ref, nact_ref, src_ref, nval_ref, xl_in_ref, wg_ref, wu_ref, wd_ref, rows_ref,
                    xbuf, ybuf, wgu_scr, wd_scr, sem_in, sem_out):
    del xl_in_ref
    j = pl.program_id(0)
    f = wg_ref.shape[-1]
    chunks_per_tile = EXPERT_TILE // SUBLANES
    nact = nact_ref[0]

    def chunk_copy(tile, k, scatter):
        slot = tile % 2
        hbm = rows_ref.at[pl.ds(pl.multiple_of(src_ref[tile * chunks_per_tile + k], SUBLANES), SUBLANES), :]
        local = pl.ds(k * SUBLANES, SUBLANES) if isinstance(k, int) else pl.ds(
            pl.multiple_of(k * SUBLANES, SUBLANES), SUBLANES)
        if scatter:
            return pltpu.make_async_copy(ybuf.at[slot, local, :], hbm, sem_out.at[slot])
        return pltpu.make_async_copy(hbm, xbuf.at[slot, local, :], sem_in.at[slot])

    def for_tile_chunks(tile, op, scatter):
        def body(k, carry):
            op(chunk_copy(tile, k, scatter))
            return carry

        lax.fori_loop(0, nval_ref[tile], body, 0)

    def start_tile_chunks(tile, scatter):
        full = nval_ref[tile] == chunks_per_tile

        @pl.when(full)
        def _():
            for k in range(chunks_per_tile):
                chunk_copy(tile, k, scatter).start()

        @pl.when(jnp.logical_not(full))
        def _():
            for_tile_chunks(tile, start, scatter)

    def wait_tile_chunks(tile, scatter):
        slot = tile % 2
        full = nval_ref[tile] == chunks_per_tile

        @pl.when(full)
        def _():
            whole = rows_ref.at[pl.ds(0, EXPERT_TILE), :]
            if scatter:
                pltpu.make_async_copy(ybuf.at[slot], whole, sem_out.at[slot]).wait()
            else:
                pltpu.make_async_copy(whole, xbuf.at[slot], sem_in.at[slot]).wait()

        @pl.when(jnp.logical_not(full))
        def _():
            for_tile_chunks(tile, wait, scatter)

    start = lambda cp: cp.start()
    wait = lambda cp: cp.wait()

    @pl.when(j == 0)
    def _():
        xbuf[...] = jnp.zeros_like(xbuf)
        start_tile_chunks(0, False)

    @pl.when(j + 1 < nact)
    def _():
        start_tile_chunks(j + 1, False)

    @pl.when(jnp.logical_and(j >= 2, j - 2 < nact))
    def _():
        wait_tile_chunks(j - 2, True)

    active = j < nact
    changed = jnp.logical_or(j == 0, exp_ref[j] != exp_ref[jnp.maximum(j - 1, 0)])

    @pl.when(jnp.logical_and(active, changed))
    def _():
        wgu_scr[:, 0:f] = wg_ref[...].astype(BF16)
        wgu_scr[:, f:2 * f] = wu_ref[...].astype(BF16)
        wd_scr[...] = wd_ref[...].astype(BF16)

    @pl.when(active)
    def _():
        wait_tile_chunks(j, False)
        half = wgu_scr.shape[0] // 2
        t_lo, t_hi = _unpack_halves(xbuf[j % 2])
        gu = _dot(t_lo, wgu_scr[0:half, :]) + _dot(t_hi, wgu_scr[half:2 * half, :])
        act = jax.nn.silu(gu[:, 0:f]) * gu[:, f:2 * f]
        ybuf[j % 2] = _pack_halves(_dot(act.astype(BF16), wd_scr[...]))
        start_tile_chunks(j, True)


def _experts_call(tile_exp, n_active, src, nval, xloc, w_gate, w_up, w_down, layer, n_tiles):
    _, n_exp, d_model, f = w_gate.shape
    xl_flat = xloc.reshape(-1, xloc.shape[-1])
    wspec = lambda a, b: pl.BlockSpec((None, None, a, b), lambda i, ex, *_: (layer, ex[i], 0, 0))
    buf = pltpu.VMEM((2, EXPERT_TILE, d_model // 2), jnp.uint32)
    out = pl.pallas_call(
        _experts_kernel,
        grid_spec=pltpu.PrefetchScalarGridSpec(
            num_scalar_prefetch=4,
            grid=(n_tiles + 2,),
            in_specs=[pl.BlockSpec(memory_space=pl.ANY), wspec(d_model, f), wspec(d_model, f), wspec(f, d_model)],
            out_specs=pl.BlockSpec(memory_space=pl.ANY),
            scratch_shapes=[buf, buf, pltpu.VMEM((d_model, 2 * f), BF16), pltpu.VMEM((f, d_model), BF16),
                            pltpu.SemaphoreType.DMA((2,)), pltpu.SemaphoreType.DMA((2,))],
        ),
        out_shape=jax.ShapeDtypeStruct(xl_flat.shape, xl_flat.dtype),
        input_output_aliases={4: 0},
        compiler_params=pltpu.CompilerParams(dimension_semantics=("arbitrary",), vmem_limit_bytes=VMEM_LIMIT),
        name="moe_experts",
    )(tile_exp, n_active, src, nval, xl_flat, w_gate, w_up, w_down)
    return out.reshape(xloc.shape)


def _moe_experts(xloc, cnt, w_gate, w_up, w_down, *, layer):
    n_exp = w_gate.shape[1]
    nt, n_local, _ = xloc.shape
    total = nt * MOE_TILE
    cpt = EXPERT_TILE // SUBLANES
    n_chunks = n_local // SUBLANES
    counts = cnt[:, 0, :n_exp].astype(jnp.int32)
    run = (counts + SUBLANES - 1) // SUBLANES
    lend = jnp.cumsum(run, axis=1)
    loff = lend - run
    gend = jnp.cumsum(run, axis=0)
    gcum = gend - run
    seg = gend[-1]
    seg_tiles = (seg + cpt - 1) // cpt
    seg_end = jnp.cumsum(seg_tiles)
    seg_off = seg_end - seg_tiles
    n_tiles = (2 * total + nt * n_exp * (SUBLANES - 1) + n_exp * (EXPERT_TILE - 1)) // EXPERT_TILE + 1
    n_steps = n_tiles + 2
    n_active = seg_end[-1].reshape(1).astype(jnp.int32)
    tile_idx = jnp.minimum(jnp.arange(n_steps, dtype=jnp.int32), n_active - 1)
    tile_exp = jnp.minimum(jnp.sum((tile_idx[:, None] >= seg_end[None, :]).astype(jnp.int32), axis=1), n_exp - 1)
    is_e = tile_exp[:, None] == jnp.arange(n_exp, dtype=jnp.int32)[None, :]
    pick = lambda table: jnp.sum(jnp.where(is_e[:, None, :], table[None, :, :], 0), axis=2)
    pick1 = lambda vec: jnp.sum(jnp.where(is_e, vec[None, :], 0), axis=1)
    g = ((jnp.arange(n_steps, dtype=jnp.int32) - pick1(seg_off)) * cpt)[:, None] + jnp.arange(cpt, dtype=jnp.int32)
    gend_t, gcum_t, loff_t = pick(gend), pick(gcum), pick(loff)
    owner = jnp.logical_and(g[:, :, None] >= gcum_t[:, None, :], g[:, :, None] < gend_t[:, None, :])
    tile_base = jnp.arange(nt, dtype=jnp.int32) * n_chunks
    src = jnp.sum(jnp.where(owner, (tile_base[None, :] + loff_t - gcum_t)[:, None, :] + g[:, :, None], 0), axis=2)
    src = (src * SUBLANES).reshape(-1)
    nval = jnp.clip(pick1(seg) - g[:, 0], 0, cpt)
    return _experts_call(tile_exp, n_active, src, nval, xloc, w_gate, w_up, w_down, layer, n_tiles)


def _finish_kernel(x1_ref, yl_ref, route_ref, gfin_ref, yp_ref, ys_ref, *, n_p_steps):
    i = pl.program_id(0)
    x = _moe_combine(x1_ref, yl_ref, route_ref)
    ms = jnp.mean(x * x, axis=-1, keepdims=True)
    y = x * lax.rsqrt(ms + EPS) * gfin_ref[...]
    pb, pl_ = yp_ref.shape[0], yp_ref.shape[1]
    sb, sl = ys_ref.shape[0], ys_ref.shape[1]

    @pl.when(i < n_p_steps)
    def _():
        for l in range(pl_):
            yp_ref[:, l, :] = y[l * pb:(l + 1) * pb, :]

    @pl.when(i >= n_p_steps)
    def _():
        for l in range(sl):
            ys_ref[:, l, :] = y[l * sb:(l + 1) * sb, :]


def _finish_call(x1, yl, route, g_final, *, geo):
    d_model = x1.shape[-1]
    g = geo
    rows = g["step_rows"]
    n_p, n_s = g["n_p_steps"], g["n_s_steps"]
    return pl.pallas_call(
        functools.partial(_finish_kernel, n_p_steps=n_p),
        grid=(n_p + n_s,),
        in_specs=[pl.BlockSpec((rows, d_model), lambda i: (i, 0)),
                  pl.BlockSpec((rows // MOE_TILE, g["n_local"], d_model // 2), lambda i: (i, 0, 0)),
                  pl.BlockSpec((rows, LANES), lambda i: (i, 0)),
                  pl.BlockSpec((1, d_model), lambda i: (0, 0))],
        out_specs=[pl.BlockSpec((g["batch"], g["p_tile_l"], d_model), lambda i: (0, jnp.minimum(i, n_p - 1), 0)),
                   pl.BlockSpec((g["s_bblk"], g["dec_seq"], d_model), lambda i: (jnp.clip(i - n_p, 0, n_s - 1), 0, 0))],
        out_shape=[jax.ShapeDtypeStruct((g["batch"], g["seq"], d_model), F32),
                   jax.ShapeDtypeStruct((g["dec_batch"], g["dec_seq"], d_model), F32)],
        compiler_params=pltpu.CompilerParams(dimension_semantics=("arbitrary",), vmem_limit_bytes=VMEM_LIMIT),
        name="finish",
    )(x1, yl, route, g_final)


def kernel(x_prompt, x_sample, state_ssm_re, state_ssm_im, state_pool, meta_tokens, norm_mix_g, w_in, ssm_a_re,
           ssm_a_im, ssm_log_dt, ssm_b_re, ssm_b_im, ssm_c_re, ssm_c_im, ssm_d, w_glu, b_glu, w_pool, pool_scale,
           w_out, norm_ffn_g, w_router_group, b_router_group, w_router_expert, b_router_expert, w_gate, w_up,
           w_down, norm_final_g):
    batch, seq, d_model = x_prompt.shape
    dec_batch, dec_seq, _ = x_sample.shape
    depth, _, _, nstate = state_ssm_re.shape
    n_meta = meta_tokens.shape[0]
    pool_buf, d_pool = state_pool.shape[2], state_pool.shape[3]
    n_pool_groups = w_pool.shape[1]
    windows = tuple(2 ** (k + 1) for k in range(n_pool_groups))
    assert pool_buf == max(windows) - 1 == POOL_HIST - 1
    n_expert_groups = w_router_group.shape[-1]
    n_experts = w_router_expert.shape[-1]
    cfg = dict(windows=windows, n_expert_groups=n_expert_groups, experts_per_group=n_experts // n_expert_groups)

    step_rows = 512
    p_tile_l, s_bblk, m_batch = step_rows // batch, step_rows // dec_seq, step_rows // n_meta
    assert m_batch >= batch and seq % p_tile_l == 0 and dec_batch % s_bblk == 0
    assert m_batch * n_meta == step_rows and step_rows % MOE_TILE == 0
    n_local = 2 * MOE_TILE + n_experts * (SUBLANES - 1)
    n_local = (n_local + LANES - 1) // LANES * LANES
    geo = dict(batch=batch, seq=seq, dec_batch=dec_batch, dec_seq=dec_seq, n_meta=n_meta, step_rows=step_rows,
               p_tile_l=p_tile_l, s_bblk=s_bblk, m_batch=m_batch, n_p_steps=seq // p_tile_l,
               n_s_steps=dec_batch // s_bblk, n_local=n_local)

    mq, pmat, a8 = _s5_prepare(ssm_a_re, ssm_a_im, ssm_log_dt, ssm_b_re, ssm_b_im, ssm_c_re, ssm_c_im)

    x = (x_prompt.astype(F32), x_sample.astype(F32), meta_tokens.astype(F32))

    row2 = lambda v: v.reshape(1, -1).astype(F32)
    outs = {k: [] for k in ("hp", "bp", "hs", "bs")}
    pending = None
    rows3 = lambda v: v.reshape(depth, 1, -1).astype(F32)
    w_r = jnp.concatenate([w_router_group, w_router_expert], axis=2).astype(F32)
    w_r = jnp.pad(w_r, ((0, 0), (0, 0), (0, LANES - w_r.shape[2])))
    b_r = jnp.concatenate([b_router_group, b_router_expert], axis=1).astype(F32)
    lw32 = dict(g_mix=rows3(norm_mix_g), w_in=w_in.astype(F32), mq=mq, pmat=pmat, a8=a8, ssm_d=rows3(ssm_d),
                w_glu=w_glu.astype(F32), b_glu=rows3(b_glu), w_pool=w_pool.astype(F32), pool_scale=rows3(pool_scale),
                w_out=w_out.astype(F32), g_ffn=rows3(norm_ffn_g), w_r=w_r,
                b_r=jnp.pad(b_r, ((0, 0), (0, LANES - b_r.shape[1])))[:, None, :])
    lw = dict(lw32, **{k: lw32[k].astype(BF16) for k in ("w_in", "mq", "pmat", "w_glu", "w_pool", "w_out")})
    for l in range(depth):
        h0_s = jnp.concatenate([state_ssm_re[l], state_ssm_im[l]], axis=-1).transpose(1, 0, 2).astype(F32)
        buf0_s = jnp.pad(state_pool[l].astype(F32).transpose(1, 0, 2), ((1, 0), (0, 0), (0, 0)))
        redo_last = l < depth - 1
        x1, route, xloc, cnt, h_p, buf_p, h_s, buf_s = _mixer_call(
            x, pending, h0_s, buf0_s, lw, geo=geo, cfg=cfg, skip_last=redo_last, layer=l)
        if redo_last:
            x1, route, xloc, cnt, h_p, buf_p = _mixer_tail_call(x, pending, h_p, buf_p, lw32, x1, route, xloc, cnt,
                                                                 geo=geo, cfg=cfg, layer=l)
        outs["hp"].append(h_p)
        outs["bp"].append(buf_p)
        outs["hs"].append(h_s)
        outs["bs"].append(buf_s)
        x, pending = x1, (_moe_experts(xloc, cnt, w_gate, w_up, w_down, layer=l), route)

    y_prompt, y_sample = _finish_call(x, pending[0], pending[1], row2(norm_final_g), geo=geo)
    st = lambda hs: jnp.stack(hs).transpose(0, 2, 1, 3)
    pl_out = lambda bs: jnp.stack(bs)[:, 1:].transpose(0, 2, 1, 3)
    hp, hs = st(outs["hp"]), st(outs["hs"])
    return (y_prompt, y_sample, hp[..., :nstate], hp[..., nstate:], pl_out(outs["bp"]),
            hs[..., :nstate], hs[..., nstate:], pl_out(outs["bs"]))
```
